```python
import jax, jax.numpy as jnp
from jax import lax
import numpy as np

D_MODEL = 1024
BATCH = 4
SEQ = 4096
DEPTH = 4

CHUNK = 64
Q_BLOCK = 128
EPS = 1e-6
POOL_WIDTH = D_MODEL // 4
POOL_WINDOWS = (2, 4, 8, 16)
POOL_GROUPS = len(POOL_WINDOWS)
POOL_GROUP_DIM = POOL_WIDTH // POOL_GROUPS
FOX_HEAD_DIM = 64
FOX_WIDTH = D_MODEL - POOL_WIDTH
FOX_HEADS = FOX_WIDTH // FOX_HEAD_DIM
SB_WIDTH = D_MODEL // 2
SB_HEAD_DIM = 64
SB_HEADS = SB_WIDTH // SB_HEAD_DIM
MLSTM_WIDTH = D_MODEL - SB_WIDTH
MLSTM_HEADS = 4
MLSTM_HEAD_DIM = MLSTM_WIDTH // MLSTM_HEADS
CONV_WIDTH = 4
D_FF = 7 * D_MODEL // 2
N_EXPERTS = 8
TOP_K = 2
EXPERT_ROWS = 128
EVEN_IN = POOL_WIDTH + 3 * FOX_WIDTH + FOX_HEADS
ODD_IN = 3 * SB_WIDTH + 4 * MLSTM_WIDTH + 2 * MLSTM_HEADS

kernel_name = 'hybrid_pool_fox_stickbreak_mlstm_moe_trunk'

F32 = jnp.float32


def rms_norm(x, g):
    xf = x.astype(F32)
    y = xf * lax.rsqrt(jnp.mean(xf * xf, axis=-1, keepdims=True) + EPS)
    return (y * g.astype(F32)).astype(x.dtype)


def split_cols(t, widths):
    idx = [int(i) for i in np.cumsum(widths)[:-1]]
    return jnp.split(t, idx, axis=-1)


def split_heads(t, n_heads):
    b, s, w = t.shape
    return t.reshape(b, s, n_heads, w // n_heads).transpose(0, 2, 1, 3)


def merge_heads(t):
    b, h, s, d = t.shape
    return t.transpose(0, 2, 1, 3).reshape(b, s, h * d)


def trailing_mean(x, window):
    s = x.shape[1]
    cs = jnp.cumsum(x.astype(F32), axis=1)
    lagged = jnp.pad(cs, ((0, 0), (window, 0), (0, 0)))[:, :s]
    count = jnp.minimum(jnp.arange(1, s + 1), window).astype(F32)
    return (cs - lagged) / count[None, :, None]


def pool_mixer(a, pool_w, pool_scale):
    b, s, _ = a.shape
    groups = []
    for g, window in enumerate(POOL_WINDOWS):
        ag = a[..., g * POOL_GROUP_DIM:(g + 1) * POOL_GROUP_DIM]
        groups.append(trailing_mean(ag, window) - ag.astype(F32))
    pooled = jnp.stack(groups, axis=2).astype(a.dtype)
    mixed = jnp.einsum('bsgc,gcd->bsgd', pooled, pool_w).reshape(b, s, POOL_WIDTH)
    return mixed * pool_scale


def forgetting_attention(q, k, v, f_pre, b_forget):
    q, k, v = split_heads(q, FOX_HEADS), split_heads(k, FOX_HEADS), split_heads(v, FOX_HEADS)
    log_f = jax.nn.log_sigmoid(f_pre.astype(F32) + b_forget.astype(F32))
    cum = jnp.cumsum(log_f, axis=1).transpose(0, 2, 1)
    s = q.shape[2]
    scale = FOX_HEAD_DIM ** -0.5
    outs = []
    for q0 in range(0, s, Q_BLOCK):
        q1 = q0 + Q_BLOCK
        logits = jnp.einsum('bhqd,bhkd->bhqk', q[:, :, q0:q1], k[:, :, :q1]).astype(F32) * scale
        logits = logits + cum[:, :, q0:q1, None] - cum[:, :, None, :q1]
        causal = jnp.arange(q1)[None, :] <= jnp.arange(q0, q1)[:, None]
        probs = jax.nn.softmax(jnp.where(causal, logits, -jnp.inf), axis=-1)
        outs.append(jnp.einsum('bhqk,bhkd->bhqd', probs.astype(v.dtype), v[:, :, :q1]))
    return merge_heads(jnp.concatenate(outs, axis=2))


def stick_breaking_attention(q, k, v):
    q, k, v = split_heads(q, SB_HEADS), split_heads(k, SB_HEADS), split_heads(v, SB_HEADS)
    s = q.shape[2]
    scale = SB_HEAD_DIM ** -0.5
    outs = []
    for q0 in range(0, s, Q_BLOCK):
        q1 = q0 + Q_BLOCK
        z = jnp.einsum('bhqd,bhkd->bhqk', q[:, :, q0:q1], k[:, :, :q1]).astype(F32) * scale
        valid = jnp.arange(q1)[None, :] < jnp.arange(q0, q1)[:, None]
        log_keep = jnp.where(valid, jax.nn.log_sigmoid(-z), 0.0)
        later = lax.cumsum(log_keep, axis=3, reverse=True) - log_keep
        w = jnp.where(valid, jnp.exp(jax.nn.log_sigmoid(z) + later), 0.0)
        outs.append(jnp.einsum('bhqk,bhkd->bhqd', w.astype(v.dtype), v[:, :, :q1]))
    return merge_heads(jnp.concatenate(outs, axis=2))


def causal_depthwise_conv(x, w):
    c = x.shape[-1]
    return lax.conv_general_dilated(
        x, w.astype(x.dtype)[:, None, :], window_strides=(1,),
        padding=[(CONV_WIDTH - 1, 0)], dimension_numbers=('NWC', 'WIO', 'NWC'),
        feature_group_count=c)


def mlstm(q, k, v, i_pre, f_pre, head_norm_g):
    b, s, _ = q.shape
    nc = s // CHUNK

    def chunks(t):
        return t.astype(F32).reshape(b, nc, CHUNK, MLSTM_HEADS, -1).transpose(0, 3, 1, 2, 4)

    def gate_chunks(t):
        return t.astype(F32).reshape(b, nc, CHUNK, MLSTM_HEADS).transpose(0, 3, 1, 2)

    q = chunks(q)
    k = chunks(k) * MLSTM_HEAD_DIM ** -0.5
    v = chunks(v)
    log_i = gate_chunks(i_pre)
    log_f = jax.nn.log_sigmoid(gate_chunks(f_pre))
    bcum = jnp.cumsum(log_f, axis=-1)
    g = bcum[..., -1]

    a = g[..., None] - bcum + log_i
    m_loc = jnp.max(a, axis=-1)
    w_loc = jnp.exp(a - m_loc[..., None])
    c_loc = jnp.einsum('bhnl,bhnld,bhnle->bhnde', w_loc, v, k)
    n_loc = jnp.einsum('bhnl,bhnle->bhne', w_loc, k)

    def step(carry, inp):
        c_st, n_st, m_st = carry
        g_c, m_l, c_l, n_l = inp
        m_new = jnp.maximum(g_c + m_st, m_l)
        decay = jnp.exp(g_c + m_st - m_new)
        fresh = jnp.exp(m_l - m_new)
        c_new = decay[..., None, None] * c_st + fresh[..., None, None] * c_l
        n_new = decay[..., None] * n_st + fresh[..., None] * n_l
        return (c_new, n_new, m_new), (c_st, n_st, m_st)

    init = (jnp.zeros((b, MLSTM_HEADS, MLSTM_HEAD_DIM, MLSTM_HEAD_DIM), F32),
            jnp.zeros((b, MLSTM_HEADS, MLSTM_HEAD_DIM), F32),
            jnp.zeros((b, MLSTM_HEADS), F32))
    xs = (jnp.moveaxis(g, 2, 0), jnp.moveaxis(m_loc, 2, 0),
          jnp.moveaxis(c_loc, 2, 0), jnp.moveaxis(n_loc, 2, 0))
    _, (c_prev, n_prev, m_prev) = lax.scan(step, init, xs)
    c_prev = jnp.moveaxis(c_prev, 0, 2)
    n_prev = jnp.moveaxis(n_prev, 0, 2)
    m_prev = jnp.moveaxis(m_prev, 0, 2)

    inter = bcum + m_prev[..., None]
    log_d = bcum[..., :, None] - bcum[..., None, :] + log_i[..., None, :]
    tri = jnp.tril(jnp.ones((CHUNK, CHUNK), dtype=bool))
    log_d = jnp.where(tri, log_d, -jnp.inf)
    m_out = jnp.maximum(inter, jnp.max(log_d, axis=-1))
    d_mat = jnp.exp(log_d - m_out[..., None])
    inter_w = jnp.exp(inter - m_out)
    qk = jnp.einsum('bhnld,bhnsd->bhnls', q, k) * d_mat
    num = jnp.einsum('bhnls,bhnsd->bhnld', qk, v) \
        + inter_w[..., None] * jnp.einsum('bhnde,bhnle->bhnld', c_prev, q)
    den = jnp.sum(qk, axis=-1) + inter_w * jnp.einsum('bhne,bhnle->bhnl', n_prev, q)
    h = num / jnp.maximum(jnp.abs(den), jnp.exp(-m_out))[..., None]
    h = h.transpose(0, 2, 3, 1, 4).reshape(b, s, MLSTM_HEADS, MLSTM_HEAD_DIM)
    h = h * lax.rsqrt(jnp.mean(h * h, axis=-1, keepdims=True) + EPS) * head_norm_g.astype(F32)
    return h.reshape(b, s, MLSTM_WIDTH)


def even_mixer(h, w_in, b_forget, pool_w, pool_scale, w_out):
    proj = h @ w_in
    a_in, fq, fk, fv, f_pre = split_cols(
        proj, (POOL_WIDTH, FOX_WIDTH, FOX_WIDTH, FOX_WIDTH, FOX_HEADS))
    a_out = pool_mixer(a_in, pool_w, pool_scale)
    b_out = forgetting_attention(fq, fk, fv, f_pre, b_forget)
    return jnp.concatenate([a_out.astype(h.dtype), b_out.astype(h.dtype)], axis=-1) @ w_out


def odd_mixer(h, w_in, b_igate, b_fgate, conv_w, head_norm_g, w_out):
    proj = h @ w_in
    sq, sk, sv, ml_qk, ml_v, ml_o, ml_i, ml_f = split_cols(
        proj, (SB_WIDTH, SB_WIDTH, SB_WIDTH, 2 * MLSTM_WIDTH, MLSTM_WIDTH, MLSTM_WIDTH,
               MLSTM_HEADS, MLSTM_HEADS))
    c_out = stick_breaking_attention(sq, sk, sv)
    ml_qk = jax.nn.silu(causal_depthwise_conv(ml_qk, conv_w))
    ml_q, ml_k = ml_qk[..., :MLSTM_WIDTH], ml_qk[..., MLSTM_WIDTH:]
    d_out = mlstm(ml_q, ml_k, ml_v, ml_i + b_igate, ml_f + b_fgate, head_norm_g)
    d_out = d_out * jax.nn.sigmoid(ml_o.astype(F32))
    return jnp.concatenate([c_out.astype(h.dtype), d_out.astype(h.dtype)], axis=-1) @ w_out


def swiglu(h, w_gate, w_up, w_down):
    return (jax.nn.silu(h @ w_gate) * (h @ w_up)) @ w_down


def moe_swiglu(h, w_router, w_gate, w_up, w_down):
    b, s, d = h.shape
    n_tok = b * s
    xt = h.reshape(n_tok, d)
    logits = xt.astype(F32) @ w_router.astype(F32)
    top_logit, top_idx = lax.top_k(logits, TOP_K)
    gates = jax.nn.softmax(top_logit, axis=-1).reshape(-1)
    flat_e = top_idx.reshape(-1)
    flat_tok = jnp.arange(n_tok * TOP_K, dtype=jnp.int32) // TOP_K
    onehot = jax.nn.one_hot(flat_e, N_EXPERTS, dtype=jnp.int32)
    rank = jnp.sum((jnp.cumsum(onehot, axis=0) - onehot) * onehot, axis=-1)
    counts = jnp.sum(onehot, axis=0)
    padded = (counts + EXPERT_ROWS - 1) // EXPERT_ROWS * EXPERT_ROWS
    group_end = jnp.cumsum(padded)
    pos = group_end[flat_e] - padded[flat_e] + rank
    n_rows = -(-n_tok * TOP_K // EXPERT_ROWS) * EXPERT_ROWS + N_EXPERTS * EXPERT_ROWS
    n_blocks = n_rows // EXPERT_ROWS
    row_tok = jnp.zeros((n_rows,), jnp.int32).at[pos].set(flat_tok)
    block_expert = jnp.minimum(
        jnp.searchsorted(group_end, jnp.arange(n_blocks) * EXPERT_ROWS, side='right'),
        N_EXPERTS - 1)

    def expert_block(args):
        tok, e = args
        xb = xt[tok]
        return (jax.nn.silu(xb @ w_gate[e]) * (xb @ w_up[e])) @ w_down[e]

    y_rows = lax.map(expert_block, (row_tok.reshape(n_blocks, EXPERT_ROWS), block_expert))
    y_rows = y_rows.reshape(n_rows, d)
    contrib = y_rows[pos].astype(F32) * gates[:, None]
    y = jax.ops.segment_sum(contrib, flat_tok, num_segments=n_tok)
    return y.reshape(b, s, d).astype(h.dtype)


def setup_inputs(seed: int = 0) -> dict:
    key = jax.random.key(seed)
    ks = jax.random.split(key, 24)
    n_even = (DEPTH + 1) // 2
    n_odd = DEPTH // 2

    def dense(k, shape, fan_in):
        return jax.random.normal(k, shape, F32) * fan_in ** -0.5

    def gain(k, shape):
        return 1.0 + 0.1 * jax.random.normal(k, shape, F32)

    return {
        'x': jax.random.normal(ks[0], (BATCH, SEQ, D_MODEL), F32),
        'norm_mix_g': gain(ks[1], (DEPTH, D_MODEL)),
        'norm_ffn_g': gain(ks[2], (DEPTH, D_MODEL)),
        'norm_final_g': gain(ks[3], (D_MODEL,)),
        'ev_w_in': dense(ks[4], (n_even, D_MODEL, EVEN_IN), D_MODEL),
        'ev_b_forget': jax.random.uniform(ks[5], (n_even, FOX_HEADS), F32, 1.0, 6.0),
        'ev_pool_w': dense(ks[6], (n_even, POOL_GROUPS, POOL_GROUP_DIM, POOL_GROUP_DIM), POOL_GROUP_DIM),
        'ev_pool_scale': gain(ks[7], (n_even, POOL_WIDTH)),
        'ev_w_out': dense(ks[8], (n_even, D_MODEL, D_MODEL), D_MODEL),
        'ffn_w_gate': dense(ks[9], (n_even, D_MODEL, D_FF), D_MODEL),
        'ffn_w_up': dense(ks[10], (n_even, D_MODEL, D_FF), D_MODEL),
        'ffn_w_down': dense(ks[11], (n_even, D_FF, D_MODEL), D_FF),
        'od_w_in': dense(ks[12], (n_odd, D_MODEL, ODD_IN), D_MODEL),
        'od_b_igate': 0.1 * jax.random.normal(ks[13], (n_odd, MLSTM_HEADS), F32),
        'od_b_fgate': jax.random.uniform(ks[14], (n_odd, MLSTM_HEADS), F32, 3.0, 6.0),
        'od_conv_w': dense(ks[15], (n_odd, CONV_WIDTH, 2 * MLSTM_WIDTH), CONV_WIDTH),
        'od_head_norm_g': gain(ks[16], (n_odd, MLSTM_HEADS, MLSTM_HEAD_DIM)),
        'od_w_out': dense(ks[17], (n_odd, D_MODEL, D_MODEL), D_MODEL),
        'moe_w_router': dense(ks[18], (n_odd, D_MODEL, N_EXPERTS), D_MODEL),
        'moe_w_gate': dense(ks[19], (n_odd, N_EXPERTS, D_MODEL, D_FF), D_MODEL),
        'moe_w_up': dense(ks[20], (n_odd, N_EXPERTS, D_MODEL, D_FF), D_MODEL),
        'moe_w_down': dense(ks[21], (n_odd, N_EXPERTS, D_FF, D_MODEL), D_FF),
    }


def reference(x, norm_mix_g, norm_ffn_g, norm_final_g, ev_w_in, ev_b_forget, ev_pool_w,
              ev_pool_scale, ev_w_out, ffn_w_gate, ffn_w_up, ffn_w_down, od_w_in, od_b_igate,
              od_b_fgate, od_conv_w, od_head_norm_g, od_w_out, moe_w_router, moe_w_gate,
              moe_w_up, moe_w_down):
    for layer in range(DEPTH):
        h = rms_norm(x, norm_mix_g[layer])
        if layer % 2 == 0:
            e = layer // 2
            x = x + even_mixer(h, ev_w_in[e], ev_b_forget[e], ev_pool_w[e], ev_pool_scale[e],
                               ev_w_out[e])
            h = rms_norm(x, norm_ffn_g[layer])
            x = x + swiglu(h, ffn_w_gate[e], ffn_w_up[e], ffn_w_down[e])
        else:
            o = layer // 2
            x = x + odd_mixer(h, od_w_in[o], od_b_igate[o], od_b_fgate[o], od_conv_w[o],
                              od_head_norm_g[o], od_w_out[o])
            h = rms_norm(x, norm_ffn_g[layer])
            x = x + moe_swiglu(h, moe_w_router[o], moe_w_gate[o], moe_w_up[o], moe_w_down[o])
    return rms_norm(x, norm_final_g)
```

```python
import functools

import jax
import jax.numpy as jnp
from jax import lax
from jax.experimental import pallas as pl
from jax.experimental.pallas import tpu as pltpu

F32 = jnp.float32
BF16 = jnp.bfloat16
EPS = 1e-6
NEG_INF = float("-inf")

LANES = 128
HEAD_DIM = 64
POOL_WINDOWS = (2, 4, 8, 16)
POOL_HALO = 16
CONV_WIDTH = 4
CONV_HALO = 8
MLSTM_CHUNK = 64
N_EXPERTS = 8
TOP_K = 2
VMEM_LIMIT = 56 * 1024 * 1024

HIGHEST = lax.Precision.HIGHEST
NT_DIMS = (((1,), (1,)), ((), ()))


def _params(*semantics):
    return pltpu.CompilerParams(dimension_semantics=semantics, vmem_limit_bytes=VMEM_LIMIT)


def _rms(x, g):
    return x * lax.rsqrt(jnp.mean(x * x, axis=-1, keepdims=True) + EPS) * g


def _log_sigmoid(x):
    return jnp.minimum(x, 0.0) - jnp.log(1.0 + jnp.exp(-jnp.abs(x)))


def _sigmoid(x):
    return 1.0 / (1.0 + jnp.exp(-x))


def _norm_proj_kernel(x_ref, g_ref, *refs, n_out, col_chunk):
    w_refs, o_refs = refs[:n_out], refs[n_out:]
    h = _rms(x_ref[...], g_ref[...]).astype(BF16)
    for w_ref, o_ref in zip(w_refs, o_refs):
        n = w_ref.shape[1]
        for c0 in range(0, n, col_chunk):
            c1 = min(c0 + col_chunk, n)
            o_ref[:, c0:c1] = jnp.dot(
                h, w_ref[:, c0:c1], preferred_element_type=F32).astype(o_ref.dtype)


def norm_proj(x, g, weights, out_dtypes, *, tm=512, col_chunk=512):
    t, d = x.shape
    n_out = len(weights)
    return pl.pallas_call(
        functools.partial(_norm_proj_kernel, n_out=n_out, col_chunk=col_chunk),
        out_shape=[jax.ShapeDtypeStruct((t, w.shape[1]), dt) for w, dt in zip(weights, out_dtypes)],
        grid=(t // tm,),
        in_specs=[pl.BlockSpec((tm, d), lambda i: (i, 0)),
                  pl.BlockSpec((1, d), lambda i: (0, 0))]
                 + [pl.BlockSpec(w.shape, lambda i: (0, 0)) for w in weights],
        out_specs=[pl.BlockSpec((tm, w.shape[1]), lambda i: (i, 0)) for w in weights],
        compiler_params=_params("parallel"),
        name="norm_proj",
    )(x, g.reshape(1, d), *weights)


def _pool_kernel(a_ref, f_ref, bf_ref, w_ref, scale_ref, o_ref, cum_ref, *, rows):
    s, width = a_ref.shape
    group = lax.broadcasted_iota(jnp.int32, (1, width), 1) // (width // len(POOL_WINDOWS))
    window = jnp.where(group == 0, POOL_WINDOWS[0],
                       jnp.where(group == 1, POOL_WINDOWS[1],
                                 jnp.where(group == 2, POOL_WINDOWS[2], POOL_WINDOWS[3])))
    tri = (lax.broadcasted_iota(jnp.int32, (rows, rows), 0)
           >= lax.broadcasted_iota(jnp.int32, (rows, rows), 1)).astype(F32)
    carry = jnp.zeros((1, f_ref.shape[1]), F32)
    for r in range(s // rows):
        r0 = r * rows
        if r == 0:
            ext = jnp.concatenate([jnp.zeros((POOL_HALO, width), F32), a_ref[0:rows, :]], axis=0)
        else:
            ext = a_ref[r0 - POOL_HALO:r0 + rows, :]
        s2 = ext + pltpu.roll(ext, 1, 0)
        s4 = s2 + pltpu.roll(s2, 2, 0)
        s8 = s4 + pltpu.roll(s4, 4, 0)
        s16 = s8 + pltpu.roll(s8, 8, 0)
        wsum = jnp.where(group == 0, s2, jnp.where(group == 1, s4, jnp.where(group == 2, s8, s16)))
        wsum = wsum[POOL_HALO:, :]
        a = ext[POOL_HALO:, :]
        t1 = r0 + 1 + lax.broadcasted_iota(jnp.int32, (rows, 1), 0)
        count = jnp.minimum(t1, window).astype(F32)
        pooled = wsum / count - a
        mixed = jnp.dot(pooled.astype(BF16), w_ref[...], preferred_element_type=F32)
        o_ref[r0:r0 + rows, :] = (mixed * scale_ref[...]).astype(o_ref.dtype)
        lf = _log_sigmoid(f_ref[r0:r0 + rows, :] + bf_ref[...])
        cum = jnp.dot(tri, lf, preferred_element_type=F32, precision=HIGHEST) + carry
        cum_ref[r0:r0 + rows, :] = cum
        carry = cum[rows - 1:rows, :]


def pool_and_cum(a_in, f_pre, b_forget, pool_w_bd, pool_scale, *, rows=256):
    b, s, width = a_in.shape
    fw = f_pre.shape[-1]
    return pl.pallas_call(
        functools.partial(_pool_kernel, rows=rows),
        out_shape=[jax.ShapeDtypeStruct((b, s, width), BF16),
                   jax.ShapeDtypeStruct((b, s, fw), F32)],
        grid=(b,),
        in_specs=[pl.BlockSpec((None, s, width), lambda i: (i, 0, 0)),
                  pl.BlockSpec((None, s, fw), lambda i: (i, 0, 0)),
                  pl.BlockSpec((1, fw), lambda i: (0, 0)),
                  pl.BlockSpec((width, width), lambda i: (0, 0)),
                  pl.BlockSpec((1, width), lambda i: (0, 0))],
        out_specs=[pl.BlockSpec((None, s, width), lambda i: (i, 0, 0)),
                   pl.BlockSpec((None, s, fw), lambda i: (i, 0, 0))],
        compiler_params=_params("parallel"),
        name="pool_and_cum",
    )(a_in, f_pre, b_forget, pool_w_bd, pool_scale)


def _fox_kernel(q_ref, k_ref, v_ref, cum_ref, o_ref, *, tq):
    i = pl.program_id(2)
    lane = lax.broadcasted_iota(jnp.int32, (1, LANES), 1)
    first = lane < HEAD_DIM
    q = (q_ref[...].astype(F32) * HEAD_DIM ** -0.5).astype(BF16)
    q_heads = (jnp.where(first, q, jnp.zeros_like(q)), jnp.where(first, jnp.zeros_like(q), q))
    row = lax.broadcasted_iota(jnp.int32, (tq, tq), 0)
    col = lax.broadcasted_iota(jnp.int32, (tq, tq), 1)

    def block(j, carry, masked):
        ms, ls, acc = carry
        start = pl.multiple_of(j * tq, tq)
        k = k_ref[pl.ds(start, tq), :]
        v = v_ref[pl.ds(start, tq), :]
        ck = cum_ref[j]
        new_ms, new_ls, alphas, pvs = [], [], [], []
        for h in range(2):
            sc = lax.dot_general(q_heads[h], k, NT_DIMS, preferred_element_type=F32)
            sc = sc - ck[h:h + 1, :]
            if masked:
                sc = jnp.where(col <= row, sc, NEG_INF)
            m_new = jnp.maximum(ms[h], jnp.max(sc, axis=-1, keepdims=True))
            alpha = jnp.exp(ms[h] - m_new)
            p = jnp.exp(sc - m_new)
            new_ls.append(alpha * ls[h] + jnp.sum(p, axis=-1, keepdims=True))
            new_ms.append(m_new)
            alphas.append(alpha)
            pvs.append(jnp.dot(p.astype(BF16), v, preferred_element_type=F32))
        acc = jnp.where(first, alphas[0], alphas[1]) * acc + jnp.where(first, pvs[0], pvs[1])
        return tuple(new_ms), tuple(new_ls), acc

    init = ((jnp.full((tq, 1), NEG_INF, F32),) * 2, (jnp.zeros((tq, 1), F32),) * 2,
            jnp.zeros((tq, LANES), F32))
    carry = lax.fori_loop(0, i, lambda j, c: block(j, c, False), init)
    _, ls, acc = block(i, carry, True)
    o_ref[...] = (acc / jnp.where(first, ls[0], ls[1])).astype(o_ref.dtype)


def fox_attention(qkv, cum_blocks, *, tq=512):
    b, s, w3 = qkv.shape
    n_pairs = w3 // 3 // LANES
    return pl.pallas_call(
        functools.partial(_fox_kernel, tq=tq),
        out_shape=jax.ShapeDtypeStruct((b, s, w3 // 3), BF16),
        grid=(b, n_pairs, s // tq),
        in_specs=[pl.BlockSpec((None, tq, LANES), lambda bi, p, i: (bi, i, p)),
                  pl.BlockSpec((None, s, LANES), lambda bi, p, i: (bi, 0, n_pairs + p)),
                  pl.BlockSpec((None, s, LANES), lambda bi, p, i: (bi, 0, 2 * n_pairs + p)),
                  pl.BlockSpec((None, None, s // tq, 2, tq), lambda bi, p, i: (bi, p, 0, 0, 0))],
        out_specs=pl.BlockSpec((None, tq, LANES), lambda bi, p, i: (bi, i, p)),
        compiler_params=_params("parallel", "parallel", "arbitrary"),
        name="fox_attention",
    )(qkv, qkv, qkv, cum_blocks)


def _sb_kernel(q_ref, k_ref, v_ref, o_ref, *, tq):
    i = pl.program_id(2)
    lane = lax.broadcasted_iota(jnp.int32, (1, LANES), 1)
    first = lane < HEAD_DIM
    q = (q_ref[...].astype(F32) * HEAD_DIM ** -0.5).astype(BF16)
    q_heads = (jnp.where(first, q, jnp.zeros_like(q)), jnp.where(first, jnp.zeros_like(q), q))
    row = lax.broadcasted_iota(jnp.int32, (tq, tq), 0)
    col = lax.broadcasted_iota(jnp.int32, (tq, tq), 1)
    valid = col < row
    suffix = (row > col).astype(BF16)

    def block(j, carry, diagonal):
        cs, acc = carry
        start = pl.multiple_of(j * tq, tq)
        k = k_ref[pl.ds(start, tq), :]
        v = v_ref[pl.ds(start, tq), :]
        new_cs, pvs = [], []
        for h in range(2):
            z = lax.dot_general(q_heads[h], k, NT_DIMS, preferred_element_type=F32)
            log_keep = -(jnp.maximum(z, 0.0) + jnp.log(1.0 + jnp.exp(-jnp.abs(z))))
            if diagonal:
                log_keep = jnp.where(valid, log_keep, 0.0)
            hi = log_keep.astype(BF16)
            lo = (log_keep - hi.astype(F32)).astype(BF16)
            later = (jnp.dot(hi, suffix, preferred_element_type=F32)
                     + jnp.dot(lo, suffix, preferred_element_type=F32) + cs[h])
            w = jnp.exp(z + log_keep + later)
            if diagonal:
                w = jnp.where(valid, w, 0.0)
            pvs.append(jnp.dot(w.astype(BF16), v, preferred_element_type=F32))
            new_cs.append(cs[h] + jnp.sum(log_keep, axis=-1, keepdims=True))
        return tuple(new_cs), acc + jnp.where(first, pvs[0], pvs[1])

    init = ((jnp.zeros((tq, 1), F32),) * 2, jnp.zeros((tq, LANES), F32))
    carry = block(i, init, True)
    _, acc = lax.fori_loop(0, i, lambda t, c: block(i - 1 - t, c, False), carry)
    o_ref[...] = acc.astype(o_ref.dtype)


def sb_attention(qkv, *, tq=256):
    b, s, w3 = qkv.shape
    n_pairs = w3 // 3 // LANES
    return pl.pallas_call(
        functools.partial(_sb_kernel, tq=tq),
        out_shape=jax.ShapeDtypeStruct((b, s, w3 // 3), BF16),
        grid=(b, n_pairs, s // tq),
        in_specs=[pl.BlockSpec((None, tq, LANES), lambda bi, p, i: (bi, i, p)),
                  pl.BlockSpec((None, s, LANES), lambda bi, p, i: (bi, 0, n_pairs + p)),
                  pl.BlockSpec((None, s, LANES), lambda bi, p, i: (bi, 0, 2 * n_pairs + p))],
        out_specs=pl.BlockSpec((None, tq, LANES), lambda bi, p, i: (bi, i, p)),
        compiler_params=_params("parallel", "parallel", "arbitrary"),
        name="sb_attention",
    )(qkv, qkv, qkv)


def _conv_kernel(x_ref, halo_ref, w_ref, colscale_ref, o_ref):
    i = pl.program_id(1)
    halo = jnp.where(i == 0, 0.0, halo_ref[...])
    ext = jnp.concatenate([halo, x_ref[...]], axis=0)
    y = ext * w_ref[CONV_WIDTH - 1:CONV_WIDTH, :]
    for back in range(1, CONV_WIDTH):
        y = y + pltpu.roll(ext, back, 0) * w_ref[CONV_WIDTH - 1 - back:CONV_WIDTH - back, :]
    y = y[CONV_HALO:, :]
    o_ref[...] = (y * _sigmoid(y) * colscale_ref[...]).astype(o_ref.dtype)


def conv_silu(x, conv_w, colscale, *, rows=512):
    b, s, c = x.shape
    per = rows // CONV_HALO
    return pl.pallas_call(
        _conv_kernel,
        out_shape=jax.ShapeDtypeStruct((b, s, c), BF16),
        grid=(b, s // rows),
        in_specs=[pl.BlockSpec((None, rows, c), lambda bi, i: (bi, i, 0)),
                  pl.BlockSpec((None, CONV_HALO, c),
                               lambda bi, i: (bi, jnp.maximum(i * per - 1, 0), 0)),
                  pl.BlockSpec((CONV_WIDTH, c), lambda bi, i: (0, 0)),
                  pl.BlockSpec((1, c), lambda bi, i: (0, 0))],
        out_specs=pl.BlockSpec((None, rows, c), lambda bi, i: (bi, i, 0)),
        compiler_params=_params("parallel", "parallel"),
        name="conv_silu",
    )(x, x, conv_w, colscale)


def _mlstm_kernel(bi_ref, bf_ref, q_ref, k_ref, v_ref, vt_ref, li_ref, lf_ref, o_ref, g_ref,
                  out_ref, c_scr, n_scr, m_scr, *, chunks):
    head = pl.program_id(1)
    step = pl.program_id(2)
    L = MLSTM_CHUNK

    @pl.when(step == 0)
    def _():
        c_scr[...] = jnp.zeros_like(c_scr)
        n_scr[...] = jnp.zeros_like(n_scr)
        m_scr[...] = jnp.zeros_like(m_scr)

    b_i = bi_ref[head]
    b_f = bf_ref[head]
    r_idx = lax.broadcasted_iota(jnp.int32, (L, L), 0)
    c_idx = lax.broadcasted_iota(jnp.int32, (L, L), 1)
    lower = r_idx >= c_idx
    tri_l = lower.astype(F32)
    tri_u = (r_idx <= c_idx).astype(F32)

    c_st = c_scr[...]
    n_st = n_scr[...]
    m_st = m_scr[...]
    for c in range(chunks):
        r0 = c * L
        qc = q_ref[r0:r0 + L, :]
        kc = k_ref[r0:r0 + L, :]
        vc = v_ref[r0:r0 + L, :]
        vtc = vt_ref[c]
        lf_row = _log_sigmoid(lf_ref[c:c + 1, :] + b_f)
        li_row = li_ref[c:c + 1, :] + b_i
        lf_rows = jnp.broadcast_to(lf_row, (L, L))
        bcum_col = lax.dot_general(tri_l, lf_rows, NT_DIMS, preferred_element_type=F32,
                                   precision=HIGHEST)
        bcum_row = jnp.dot(lf_rows[0:8, :], tri_u, preferred_element_type=F32,
                           precision=HIGHEST)[0:1, :]
        g = jnp.sum(lf_row, axis=-1, keepdims=True)

        log_d = jnp.where(lower, bcum_col - bcum_row + li_row, NEG_INF)
        inter = bcum_col[:, 0:1] + m_st
        m_out = jnp.maximum(inter, jnp.max(log_d, axis=-1, keepdims=True))
        d_mat = jnp.exp(log_d - m_out)
        inter_w = jnp.exp(inter - m_out)
        qk = lax.dot_general(qc, kc, NT_DIMS, preferred_element_type=F32) * d_mat
        num = (jnp.dot(qk.astype(BF16), vc, preferred_element_type=F32)
               + inter_w * lax.dot_general(qc, c_st.astype(BF16), NT_DIMS,
                                           preferred_element_type=F32))
        den = (jnp.sum(qk, axis=-1, keepdims=True)
               + inter_w * jnp.sum(qc.astype(F32) * n_st, axis=-1, keepdims=True))
        hh = num / jnp.maximum(jnp.abs(den), jnp.exp(-m_out))
        hn = _rms(hh, g_ref[...])
        out_ref[r0:r0 + L, :] = (hn * _sigmoid(o_ref[r0:r0 + L, :])).astype(out_ref.dtype)

        a_row = g - bcum_row + li_row
        m_loc = jnp.max(a_row, axis=-1, keepdims=True)
        w_row = jnp.exp(a_row - m_loc)
        c_loc = jnp.dot((vtc.astype(F32) * w_row).astype(BF16), kc, preferred_element_type=F32)
        n_loc = jnp.dot(jnp.broadcast_to(w_row, (8, L)).astype(BF16), kc,
                        preferred_element_type=F32)[0:1, :]
        m_new = jnp.maximum(g + m_st, m_loc)
        decay = jnp.exp(g + m_st - m_new)
        fresh = jnp.exp(m_loc - m_new)
        c_st = decay * c_st + fresh * c_loc
        n_st = decay * n_st + fresh * n_loc
        m_st = m_new
    c_scr[...] = c_st
    n_scr[...] = n_st
    m_scr[...] = m_st


def mlstm(qk, v, v_t, li, lf, o_gate, b_i, b_f, head_g, *, chunks=8):
    b, s, w = v.shape
    n_heads = w // LANES
    rows = chunks * MLSTM_CHUNK
    grid_spec = pltpu.PrefetchScalarGridSpec(
        num_scalar_prefetch=2,
        grid=(b, n_heads, s // rows),
        in_specs=[pl.BlockSpec((None, rows, LANES), lambda bi, h, c, *_: (bi, c, h)),
                  pl.BlockSpec((None, rows, LANES), lambda bi, h, c, *_: (bi, c, n_heads + h)),
                  pl.BlockSpec((None, rows, LANES), lambda bi, h, c, *_: (bi, c, h)),
                  pl.BlockSpec((None, None, chunks, LANES, MLSTM_CHUNK),
                               lambda bi, h, c, *_: (bi, h, c, 0, 0)),
                  pl.BlockSpec((None, None, chunks, MLSTM_CHUNK), lambda bi, h, c, *_: (bi, h, c, 0)),
                  pl.BlockSpec((None, None, chunks, MLSTM_CHUNK), lambda bi, h, c, *_: (bi, h, c, 0)),
                  pl.BlockSpec((None, rows, LANES), lambda bi, h, c, *_: (bi, c, h)),
                  pl.BlockSpec((None, 1, LANES), lambda bi, h, c, *_: (h, 0, 0))],
        out_specs=pl.BlockSpec((None, rows, LANES), lambda bi, h, c, *_: (bi, c, h)),
        scratch_shapes=[pltpu.VMEM((LANES, LANES), F32), pltpu.VMEM((1, LANES), F32),
                        pltpu.VMEM((1, 1), F32)])
    return pl.pallas_call(
        functools.partial(_mlstm_kernel, chunks=chunks),
        out_shape=jax.ShapeDtypeStruct((b, s, w), BF16),
        grid_spec=grid_spec,
        compiler_params=_params("parallel", "parallel", "arbitrary"),
        name="mlstm",
    )(b_i, b_f, qk, qk, v, v_t, li, lf, o_gate, head_g.reshape(n_heads, 1, LANES))


def _out_proj_kernel(x_ref, a_ref, b_ref, w_ref, o_ref):
    na = a_ref.shape[1]
    acc = jnp.dot(a_ref[...], w_ref[0:na, :], preferred_element_type=F32)
    acc = acc + jnp.dot(b_ref[...], w_ref[na:, :], preferred_element_type=F32)
    o_ref[...] = x_ref[...] + acc


def out_proj_residual(x, a, b_part, w, *, tm=512):
    t, d = x.shape
    return pl.pallas_call(
        _out_proj_kernel,
        out_shape=jax.ShapeDtypeStruct((t, d), F32),
        grid=(t // tm,),
        in_specs=[pl.BlockSpec((tm, d), lambda i: (i, 0)),
                  pl.BlockSpec((tm, a.shape[1]), lambda i: (i, 0)),
                  pl.BlockSpec((tm, b_part.shape[1]), lambda i: (i, 0)),
                  pl.BlockSpec(w.shape, lambda i: (0, 0))],
        out_specs=pl.BlockSpec((tm, d), lambda i: (i, 0)),
        compiler_params=_params("parallel"),
        name="out_proj_residual",
    )(x, a, b_part, w)


def _ffn_kernel(x_ref, g_ref, wg_ref, wu_ref, wd_ref, o_ref, h_scr, acc_scr):
    j = pl.program_id(1)

    @pl.when(j == 0)
    def _():
        h_scr[...] = _rms(x_ref[...], g_ref[...]).astype(BF16)
        acc_scr[...] = jnp.zeros_like(acc_scr)

    h = h_scr[...]
    gate = jnp.dot(h, wg_ref[...], preferred_element_type=F32)
    up = jnp.dot(h, wu_ref[...], preferred_element_type=F32)
    act = (gate * _sigmoid(gate) * up).astype(BF16)
    acc_scr[...] += jnp.dot(act, wd_ref[...], preferred_element_type=F32)

    @pl.when(j == pl.num_programs(1) - 1)
    def _():
        o_ref[...] = x_ref[...] + acc_scr[...]


def ffn_residual(x, g, w_gate, w_up, w_down, *, tm=1024, tf=512):
    t, d = x.shape
    ff = w_gate.shape[1]
    return pl.pallas_call(
        _ffn_kernel,
        out_shape=jax.ShapeDtypeStruct((t, d), F32),
        grid=(t // tm, ff // tf),
        in_specs=[pl.BlockSpec((tm, d), lambda i, j: (i, 0)),
                  pl.BlockSpec((1, d), lambda i, j: (0, 0)),
                  pl.BlockSpec((d, tf), lambda i, j: (0, j)),
                  pl.BlockSpec((d, tf), lambda i, j: (0, j)),
                  pl.BlockSpec((tf, d), lambda i, j: (j, 0))],
        out_specs=pl.BlockSpec((tm, d), lambda i, j: (i, 0)),
        scratch_shapes=[pltpu.VMEM((tm, d), BF16), pltpu.VMEM((tm, d), F32)],
        compiler_params=_params("parallel", "arbitrary"),
        name="ffn_residual",
    )(x, g.reshape(1, d), w_gate, w_up, w_down)


def _router_kernel(x_ref, g_ref, wr_ref, h_ref, route_ref):
    h = _rms(x_ref[...], g_ref[...])
    h_ref[...] = h
    logits = jnp.dot(h, wr_ref[...], preferred_element_type=F32, precision=HIGHEST)
    lane = lax.broadcasted_iota(jnp.int32, logits.shape, 1)
    logits = jnp.where(lane < N_EXPERTS, logits, NEG_INF)
    m1 = jnp.max(logits, axis=-1, keepdims=True)
    i1 = jnp.min(jnp.where(logits == m1, lane, LANES), axis=-1, keepdims=True)
    rest = jnp.where(lane == i1, NEG_INF, logits)
    m2 = jnp.max(rest, axis=-1, keepdims=True)
    i2 = jnp.min(jnp.where(rest == m2, lane, LANES), axis=-1, keepdims=True)
    e2 = jnp.exp(m2 - m1)
    g1 = 1.0 / (1.0 + e2)
    g2 = e2 / (1.0 + e2)
    route_ref[...] = jnp.where(lane == 0, i1.astype(F32),
                               jnp.where(lane == 1, i2.astype(F32),
                                         jnp.where(lane == 2, g1, jnp.where(lane == 3, g2, 0.0))))


def router(x, g, w_router_padded, *, tm=512):
    t, d = x.shape
    return pl.pallas_call(
        _router_kernel,
        out_shape=[jax.ShapeDtypeStruct((t, d), F32), jax.ShapeDtypeStruct((t, LANES), F32)],
        grid=(t // tm,),
        in_specs=[pl.BlockSpec((tm, d), lambda i: (i, 0)),
                  pl.BlockSpec((1, d), lambda i: (0, 0)),
                  pl.BlockSpec((d, LANES), lambda i: (0, 0))],
        out_specs=[pl.BlockSpec((tm, d), lambda i: (i, 0)),
                   pl.BlockSpec((tm, LANES), lambda i: (i, 0))],
        compiler_params=_params("parallel"),
        name="router",
    )(x, g.reshape(1, d), w_router_padded)


def _gather_kernel(idx_ref, src_ref, dst_ref, sem, *, rows):
    base = pl.program_id(0) * rows

    def issue(r, _):
        pltpu.make_async_copy(src_ref.at[pl.ds(idx_ref[0, r], 1)],
                              dst_ref.at[pl.ds(base + r, 1)], sem).start()
        return 0

    lax.fori_loop(0, rows, issue, 0)
    pltpu.make_async_copy(src_ref.at[pl.ds(0, rows)], dst_ref.at[pl.ds(base, rows)], sem).wait()


def gather_rows(src, idx, *, rows=512):
    n = idx.shape[0]
    d = src.shape[1]
    return pl.pallas_call(
        functools.partial(_gather_kernel, rows=rows),
        out_shape=jax.ShapeDtypeStruct((n, d), src.dtype),
        grid=(n // rows,),
        in_specs=[pl.BlockSpec((None, 1, rows), lambda i: (i, 0, 0), memory_space=pltpu.SMEM),
                  pl.BlockSpec(memory_space=pl.ANY)],
        out_specs=pl.BlockSpec(memory_space=pl.ANY),
        scratch_shapes=[pltpu.SemaphoreType.DMA],
        compiler_params=_params("arbitrary"),
        name="gather_rows",
    )(idx.reshape(n // rows, 1, rows), src)


def _expert_kernel(be_ref, nused_ref, x_ref, wg_ref, wu_ref, wd_ref, o_ref, h_scr, acc_scr):
    i = pl.program_id(0)
    j = pl.program_id(1)
    last = pl.num_programs(1) - 1
    used = i < nused_ref[0]

    @pl.when(jnp.logical_and(used, j == 0))
    def _():
        h_scr[...] = x_ref[...].astype(BF16)
        acc_scr[...] = jnp.zeros_like(acc_scr)

    @pl.when(used)
    def _():
        h = h_scr[...]
        gate = jnp.dot(h, wg_ref[...], preferred_element_type=F32)
        up = jnp.dot(h, wu_ref[...], preferred_element_type=F32)
        act = (gate * _sigmoid(gate) * up).astype(BF16)
        acc_scr[...] += jnp.dot(act, wd_ref[...], preferred_element_type=F32)

    @pl.when(jnp.logical_and(used, j == last))
    def _():
        o_ref[...] = acc_scr[...]

    @pl.when(jnp.logical_and(jnp.logical_not(used), j == last))
    def _():
        o_ref[...] = jnp.zeros_like(o_ref)


def expert_ffn(xg, block_expert, n_used, w_gate, w_up, w_down, *, tm, tf=512):
    n_rows, d = xg.shape
    ff = w_gate.shape[2]

    def row_block(i, j, be, nu):
        return (jnp.minimum(i, nu[0] - 1), 0)

    def up_block(i, j, be, nu):
        return (be[jnp.minimum(i, nu[0] - 1)], 0, jnp.where(i < nu[0], j, ff // tf - 1))

    def down_block(i, j, be, nu):
        return (be[jnp.minimum(i, nu[0] - 1)], jnp.where(i < nu[0], j, ff // tf - 1), 0)

    grid_spec = pltpu.PrefetchScalarGridSpec(
        num_scalar_prefetch=2,
        grid=(n_rows // tm, ff // tf),
        in_specs=[pl.BlockSpec((tm, d), row_block),
                  pl.BlockSpec((None, d, tf), up_block),
                  pl.BlockSpec((None, d, tf), up_block),
                  pl.BlockSpec((None, tf, d), down_block)],
        out_specs=pl.BlockSpec((tm, d), lambda i, j, be, nu: (i, 0)),
        scratch_shapes=[pltpu.VMEM((tm, d), BF16), pltpu.VMEM((tm, d), F32)])
    return pl.pallas_call(
        _expert_kernel,
        out_shape=jax.ShapeDtypeStruct((n_rows, d), F32),
        grid_spec=grid_spec,
        compiler_params=_params("arbitrary", "arbitrary"),
        name="expert_ffn",
    )(block_expert, n_used, xg, w_gate, w_up, w_down)


def _combine_kernel(pos_cur_ref, pos_next_ref, x_ref, route_ref, y_ref, o_ref, buf, sems, *, rows):
    i = pl.program_id(0)
    n = pl.num_programs(0)

    def issue(pos_ref, slot):
        def body(r, _):
            for kk in range(TOP_K):
                pltpu.make_async_copy(y_ref.at[pl.ds(pos_ref[0, TOP_K * r + kk], 1)],
                                      buf.at[slot, kk, pl.ds(r, 1)], sems.at[slot]).start()
            return 0
        lax.fori_loop(0, rows, body, 0)

    @pl.when(i == 0)
    def _():
        issue(pos_cur_ref, 0)

    @pl.when(i + 1 < n)
    def _():
        issue(pos_next_ref, (i + 1) % 2)

    slot = i % 2
    pltpu.make_async_copy(buf.at[slot], buf.at[slot], sems.at[slot]).wait()
    route = route_ref[...]
    g1 = route[:, 2:3]
    g2 = route[:, 3:4]
    o_ref[...] = x_ref[...] + g1 * buf[slot, 0] + g2 * buf[slot, 1]


def combine_residual(x, route, y_rows, pos, *, rows=256):
    t, d = x.shape
    n_steps = t // rows
    grid_spec = pltpu.PrefetchScalarGridSpec(
        num_scalar_prefetch=0,
        grid=(n_steps,),
        in_specs=[pl.BlockSpec((None, 1, TOP_K * rows), lambda i: (i, 0, 0),
                               memory_space=pltpu.SMEM),
                  pl.BlockSpec((None, 1, TOP_K * rows),
                               lambda i: (jnp.minimum(i + 1, n_steps - 1), 0, 0),
                               memory_space=pltpu.SMEM),
                  pl.BlockSpec((rows, d), lambda i: (i, 0)),
                  pl.BlockSpec((rows, LANES), lambda i: (i, 0)),
                  pl.BlockSpec(memory_space=pl.ANY)],
        out_specs=pl.BlockSpec((rows, d), lambda i: (i, 0)),
        scratch_shapes=[pltpu.VMEM((2, TOP_K, rows, d), F32), pltpu.SemaphoreType.DMA((2,))])
    return pl.pallas_call(
        functools.partial(_combine_kernel, rows=rows),
        out_shape=jax.ShapeDtypeStruct((t, d), F32),
        grid_spec=grid_spec,
        compiler_params=_params("arbitrary"),
        name="combine_residual",
    )(pos.reshape(n_steps, 1, TOP_K * rows), pos.reshape(n_steps, 1, TOP_K * rows), x, route, y_rows)


def _final_norm_kernel(x_ref, g_ref, o_ref):
    o_ref[...] = _rms(x_ref[...], g_ref[...])


def final_norm(x, g, *, tm=1024):
    t, d = x.shape
    return pl.pallas_call(
        _final_norm_kernel,
        out_shape=jax.ShapeDtypeStruct((t, d), F32),
        grid=(t // tm,),
        in_specs=[pl.BlockSpec((tm, d), lambda i: (i, 0)), pl.BlockSpec((1, d), lambda i: (0, 0))],
        out_specs=pl.BlockSpec((tm, d), lambda i: (i, 0)),
        compiler_params=_params("parallel"),
        name="final_norm",
    )(x, g.reshape(1, d))


def _pad_cols(w, n):
    return jnp.pad(w, ((0, 0), (0, n - w.shape[1])))


def even_layer(x, b, s, g_mix, g_ffn, w_in, b_forget, pool_w, pool_scale, w_out,
               w_gate, w_up, w_down, *, fox_tq=512):
    t, d = x.shape
    pool_width = pool_w.shape[0] * pool_w.shape[1]
    n_heads = b_forget.shape[0]
    fox_width = n_heads * HEAD_DIM
    w_a = w_in[:, :pool_width].astype(BF16)
    w_qkv = w_in[:, pool_width:pool_width + 3 * fox_width].astype(BF16)
    w_f = _pad_cols(w_in[:, pool_width + 3 * fox_width:], LANES).astype(BF16)
    qkv, a_in, f_pre = norm_proj(x, g_mix, [w_qkv, w_a, w_f], [BF16, F32, F32])

    pool_bd = jax.scipy.linalg.block_diag(*[pool_w[i] for i in range(pool_w.shape[0])]).astype(BF16)
    a_out, cum = pool_and_cum(a_in.reshape(b, s, pool_width), f_pre.reshape(b, s, LANES),
                              _pad_cols(b_forget.reshape(1, n_heads), LANES), pool_bd,
                              pool_scale.reshape(1, pool_width))
    cum_blocks = cum[:, :, :n_heads].reshape(b, s // fox_tq, fox_tq, n_heads // 2, 2)
    cum_blocks = cum_blocks.transpose(0, 3, 1, 4, 2)
    attn = fox_attention(qkv.reshape(b, s, 3 * fox_width), cum_blocks, tq=fox_tq)
    x = out_proj_residual(x, a_out.reshape(t, pool_width), attn.reshape(t, fox_width),
                          w_out.astype(BF16))
    return ffn_residual(x, g_ffn, w_gate.astype(BF16), w_up.astype(BF16), w_down.astype(BF16))


def odd_layer(x, b, s, g_mix, g_ffn, w_in, b_igate, b_fgate, conv_w, head_norm_g, w_out,
              w_router, w_gate, w_up, w_down):
    x = odd_mixer_block(x, b, s, g_mix, w_in, b_igate, b_fgate, conv_w, head_norm_g, w_out)
    return moe_block(x, g_ffn, w_router, w_gate, w_up, w_down)


def odd_mixer_block(x, b, s, g_mix, w_in, b_igate, b_fgate, conv_w, head_norm_g, w_out):
    t, d = x.shape
    n_ml = b_igate.shape[0]
    ml_width = n_ml * LANES
    sb_width = d - ml_width
    cols = [3 * sb_width, 2 * ml_width, ml_width, ml_width]
    edges = [0]
    for c in cols:
        edges.append(edges[-1] + c)
    w_sb, w_mqk, w_mv, w_mo = [w_in[:, edges[i]:edges[i + 1]].astype(BF16) for i in range(4)]
    w_gates = _pad_cols(w_in[:, edges[-1]:], LANES).astype(BF16)
    sqkv, ml_qk, ml_v, ml_o, gates = norm_proj(
        x, g_mix, [w_sb, w_mqk, w_mv, w_mo, w_gates], [BF16, F32, BF16, F32, F32])

    c_out = sb_attention(sqkv.reshape(b, s, 3 * sb_width))

    k_scale = jnp.concatenate([jnp.ones((ml_width,), F32),
                               jnp.full((ml_width,), LANES ** -0.5, F32)]).reshape(1, 2 * ml_width)
    qk = conv_silu(ml_qk.reshape(b, s, 2 * ml_width), conv_w, k_scale)
    nc = s // MLSTM_CHUNK
    v3 = ml_v.reshape(b, s, ml_width)
    v_t = v3.reshape(b, nc, MLSTM_CHUNK, n_ml, LANES).transpose(0, 3, 1, 4, 2)
    gate_rows = gates[:, :2 * n_ml].reshape(b, nc, MLSTM_CHUNK, 2 * n_ml).transpose(0, 3, 1, 2)
    d_out = mlstm(qk, v3, v_t, gate_rows[:, :n_ml], gate_rows[:, n_ml:],
                  ml_o.reshape(b, s, ml_width), b_igate, b_fgate, head_norm_g)
    return out_proj_residual(x, c_out.reshape(t, sb_width), d_out.reshape(t, ml_width),
                             w_out.astype(BF16))


def moe_block(x, g_ffn, w_router, w_gate, w_up, w_down, *, expert_rows=512):
    t, d = x.shape
    h, route = router(x, g_ffn, _pad_cols(w_router, LANES))
    flat_e = route[:, :TOP_K].astype(jnp.int32).reshape(-1)
    n_assign = t * TOP_K
    onehot = jax.nn.one_hot(flat_e, N_EXPERTS, dtype=jnp.int32)
    rank = jnp.sum((jnp.cumsum(onehot, axis=0) - onehot) * onehot, axis=-1)
    counts = jnp.sum(onehot, axis=0)
    padded = (counts + expert_rows - 1) // expert_rows * expert_rows
    group_end = jnp.cumsum(padded)
    pos = (group_end[flat_e] - padded[flat_e] + rank).astype(jnp.int32)
    n_rows = n_assign + N_EXPERTS * expert_rows
    n_blocks = n_rows // expert_rows
    flat_tok = jnp.arange(n_assign, dtype=jnp.int32) // TOP_K
    row_tok = jnp.zeros((n_rows,), jnp.int32).at[pos].set(flat_tok)
    block_expert = jnp.minimum(
        jnp.searchsorted(group_end, jnp.arange(n_blocks) * expert_rows, side='right'),
        N_EXPERTS - 1).astype(jnp.int32)
    n_used = (group_end[-1:] // expert_rows).astype(jnp.int32)

    xg = gather_rows(h, row_tok)
    y_rows = expert_ffn(xg, block_expert, n_used, w_gate.astype(BF16), w_up.astype(BF16),
                        w_down.astype(BF16), tm=expert_rows)
    return combine_residual(x, route, y_rows, pos)


def kernel(x, norm_mix_g, norm_ffn_g, norm_final_g, ev_w_in, ev_b_forget, ev_pool_w, ev_pool_scale, ev_w_out, ffn_w_gate, ffn_w_up, ffn_w_down, od_w_in, od_b_igate, od_b_fgate, od_conv_w, od_head_norm_g, od_w_out, moe_w_router, moe_w_gate, moe_w_up, moe_w_down):
    b, s, d = x.shape
    depth = norm_mix_g.shape[0]
    xt = x.reshape(b * s, d)
    for layer in range(depth):
        e = layer // 2
        if layer % 2 == 0:
            xt = even_layer(xt, b, s, norm_mix_g[layer], norm_ffn_g[layer], ev_w_in[e],
                            ev_b_forget[e], ev_pool_w[e], ev_pool_scale[e], ev_w_out[e],
                            ffn_w_gate[e], ffn_w_up[e], ffn_w_down[e])
        else:
            xt = odd_layer(xt, b, s, norm_mix_g[layer], norm_ffn_g[layer], od_w_in[e],
                           od_b_igate[e], od_b_fgate[e], od_conv_w[e], od_head_norm_g[e],
                           od_w_out[e], moe_w_router[e], moe_w_gate[e], moe_w_up[e], moe_w_down[e])
    return final_norm(xt, norm_final_g).reshape(b, s, d)
```

```python
import functools

import jax
import jax.numpy as jnp
from jax import lax
from jax.experimental import pallas as pl
from jax.experimental.pallas import tpu as pltpu

F32 = jnp.float32
BF16 = jnp.bfloat16
EPS = 1e-6
NEG_INF = float("-inf")

LANES = 128
HEAD_DIM = 64
POOL_WINDOWS = (2, 4, 8, 16)
POOL_HALO = 16
CONV_WIDTH = 4
CONV_HALO = 8
MLSTM_CHUNK = 64
N_EXPERTS = 8
TOP_K = 2
VMEM_LIMIT = 56 * 1024 * 1024

LOG2E = 1.4426950408889634
HIGHEST = lax.Precision.HIGHEST
NT_DIMS = (((1,), (1,)), ((), ()))


def _params(*semantics):
    return pltpu.CompilerParams(dimension_semantics=semantics, vmem_limit_bytes=VMEM_LIMIT)


def _rms(x, g):
    return x * lax.rsqrt(jnp.mean(x * x, axis=-1, keepdims=True) + EPS) * g


def _log_sigmoid(x):
    return jnp.minimum(x, 0.0) - jnp.log(1.0 + jnp.exp(-jnp.abs(x)))


def _sigmoid(x):
    return 1.0 / (1.0 + jnp.exp(-x))


def _norm_proj_kernel(x_ref, g_ref, *refs, n_out, col_chunk):
    w_refs, o_refs = refs[:n_out], refs[n_out:]
    h = _rms(x_ref[...], g_ref[...]).astype(BF16)
    for w_ref, o_ref in zip(w_refs, o_refs):
        n = w_ref.shape[1]
        for c0 in range(0, n, col_chunk):
            c1 = min(c0 + col_chunk, n)
            o_ref[:, c0:c1] = jnp.dot(
                h, w_ref[:, c0:c1], preferred_element_type=F32).astype(o_ref.dtype)


def norm_proj(x, g, weights, out_dtypes, *, tm=512, col_chunk=512):
    t, d = x.shape
    n_out = len(weights)
    return pl.pallas_call(
        functools.partial(_norm_proj_kernel, n_out=n_out, col_chunk=col_chunk),
        out_shape=[jax.ShapeDtypeStruct((t, w.shape[1]), dt) for w, dt in zip(weights, out_dtypes)],
        grid=(t // tm,),
        in_specs=[pl.BlockSpec((tm, d), lambda i: (i, 0)),
                  pl.BlockSpec((1, d), lambda i: (0, 0))]
                 + [pl.BlockSpec(w.shape, lambda i: (0, 0)) for w in weights],
        out_specs=[pl.BlockSpec((tm, w.shape[1]), lambda i: (i, 0)) for w in weights],
        compiler_params=_params("parallel"),
        name="norm_proj",
    )(x, g.reshape(1, d), *weights)


def _pool_kernel(a_ref, f_ref, bf_ref, w_ref, scale_ref, o_ref, cum_ref, *, rows):
    s, width = a_ref.shape
    group = lax.broadcasted_iota(jnp.int32, (1, width), 1) // (width // len(POOL_WINDOWS))
    window = jnp.where(group == 0, POOL_WINDOWS[0],
                       jnp.where(group == 1, POOL_WINDOWS[1],
                                 jnp.where(group == 2, POOL_WINDOWS[2], POOL_WINDOWS[3])))
    tri = (lax.broadcasted_iota(jnp.int32, (rows, rows), 0)
           >= lax.broadcasted_iota(jnp.int32, (rows, rows), 1)).astype(F32)
    carry = jnp.zeros((1, f_ref.shape[1]), F32)
    for r in range(s // rows):
        r0 = r * rows
        if r == 0:
            ext = jnp.concatenate([jnp.zeros((POOL_HALO, width), F32), a_ref[0:rows, :]], axis=0)
        else:
            ext = a_ref[r0 - POOL_HALO:r0 + rows, :]
        s2 = ext + pltpu.roll(ext, 1, 0)
        s4 = s2 + pltpu.roll(s2, 2, 0)
        s8 = s4 + pltpu.roll(s4, 4, 0)
        s16 = s8 + pltpu.roll(s8, 8, 0)
        wsum = jnp.where(group == 0, s2, jnp.where(group == 1, s4, jnp.where(group == 2, s8, s16)))
        wsum = wsum[POOL_HALO:, :]
        a = ext[POOL_HALO:, :]
        t1 = r0 + 1 + lax.broadcasted_iota(jnp.int32, (rows, 1), 0)
        count = jnp.minimum(t1, window).astype(F32)
        pooled = wsum / count - a
        mixed = jnp.dot(pooled.astype(BF16), w_ref[...], preferred_element_type=F32)
        o_ref[r0:r0 + rows, :] = (mixed * scale_ref[...]).astype(o_ref.dtype)
        lf = _log_sigmoid(f_ref[r0:r0 + rows, :] + bf_ref[...]) * LOG2E
        cum = jnp.dot(tri, lf, preferred_element_type=F32, precision=HIGHEST) + carry
        cum_ref[r0:r0 + rows, :] = cum
        carry = cum[rows - 1:rows, :]


def pool_and_cum(a_in, f_pre, b_forget, pool_w_bd, pool_scale, *, rows=256):
    b, s, width = a_in.shape
    fw = f_pre.shape[-1]
    return pl.pallas_call(
        functools.partial(_pool_kernel, rows=rows),
        out_shape=[jax.ShapeDtypeStruct((b, s, width), BF16),
                   jax.ShapeDtypeStruct((b, s, fw), F32)],
        grid=(b,),
        in_specs=[pl.BlockSpec((None, s, width), lambda i: (i, 0, 0)),
                  pl.BlockSpec((None, s, fw), lambda i: (i, 0, 0)),
                  pl.BlockSpec((1, fw), lambda i: (0, 0)),
                  pl.BlockSpec((width, width), lambda i: (0, 0)),
                  pl.BlockSpec((1, width), lambda i: (0, 0))],
        out_specs=[pl.BlockSpec((None, s, width), lambda i: (i, 0, 0)),
                   pl.BlockSpec((None, s, fw), lambda i: (i, 0, 0))],
        compiler_params=_params("parallel"),
        name="pool_and_cum",
    )(a_in, f_pre, b_forget, pool_w_bd, pool_scale)


def _fox_kernel(q_ref, k_ref, v_ref, cum_ref, o_ref, *, tq):
    i = pl.program_id(2)
    lane = lax.broadcasted_iota(jnp.int32, (1, LANES), 1)
    first = lane < HEAD_DIM
    q = q_ref[...]
    q_heads = (jnp.where(first, q, jnp.zeros_like(q)), jnp.where(first, jnp.zeros_like(q), q))
    row = lax.broadcasted_iota(jnp.int32, (tq, tq), 0)
    col = lax.broadcasted_iota(jnp.int32, (tq, tq), 1)

    def block(j, carry, masked):
        ms, ls, acc = carry
        start = pl.multiple_of(j * tq, tq)
        k = k_ref[pl.ds(start, tq), :]
        v = v_ref[pl.ds(start, tq), :]
        ck = cum_ref[j]
        new_ms, new_ls, alphas, pvs = [], [], [], []
        for h in range(2):
            sc = lax.dot_general(q_heads[h], k, NT_DIMS, preferred_element_type=F32)
            sc = sc - ck[h:h + 1, :]
            if masked:
                sc = jnp.where(col <= row, sc, NEG_INF)
            m_new = jnp.maximum(ms[h], jnp.max(sc, axis=-1, keepdims=True))
            alpha = jnp.exp2(ms[h] - m_new)
            p = jnp.exp2(sc - m_new)
            new_ls.append(alpha * ls[h] + jnp.sum(p, axis=-1, keepdims=True))
            new_ms.append(m_new)
            alphas.append(alpha)
            pvs.append(jnp.dot(p.astype(BF16), v, preferred_element_type=F32))
        acc = jnp.where(first, alphas[0], alphas[1]) * acc + jnp.where(first, pvs[0], pvs[1])
        return tuple(new_ms), tuple(new_ls), acc

    init = ((jnp.full((tq, 1), NEG_INF, F32),) * 2, (jnp.zeros((tq, 1), F32),) * 2,
            jnp.zeros((tq, LANES), F32))
    carry = lax.fori_loop(0, i, lambda j, c: block(j, c, False), init)
    _, ls, acc = block(i, carry, True)
    o_ref[...] = (acc / jnp.where(first, ls[0], ls[1])).astype(o_ref.dtype)


def fox_attention(qkv, cum_blocks, *, tq=512):
    b, s, w3 = qkv.shape
    n_pairs = w3 // 3 // LANES
    return pl.pallas_call(
        functools.partial(_fox_kernel, tq=tq),
        out_shape=jax.ShapeDtypeStruct((b, s, w3 // 3), BF16),
        grid=(b, n_pairs, s // tq),
        in_specs=[pl.BlockSpec((None, tq, LANES), lambda bi, p, i: (bi, i, p)),
                  pl.BlockSpec((None, s, LANES), lambda bi, p, i: (bi, 0, n_pairs + p)),
                  pl.BlockSpec((None, s, LANES), lambda bi, p, i: (bi, 0, 2 * n_pairs + p)),
                  pl.BlockSpec((None, None, s // tq, 2, tq), lambda bi, p, i: (bi, p, 0, 0, 0))],
        out_specs=pl.BlockSpec((None, tq, LANES), lambda bi, p, i: (bi, i, p)),
        compiler_params=_params("parallel", "parallel", "arbitrary"),
        name="fox_attention",
    )(qkv, qkv, qkv, cum_blocks)


def _sb_kernel(q_ref, k_ref, v_ref, o_ref, *, tq, tk):
    i = pl.program_id(2)
    per = tq // tk
    lane = lax.broadcasted_iota(jnp.int32, (1, LANES), 1)
    first = lane < HEAD_DIM
    q = q_ref[...]
    q_heads = (jnp.where(first, q, jnp.zeros_like(q)), jnp.where(first, jnp.zeros_like(q), q))
    row = lax.broadcasted_iota(jnp.int32, (tq, tk), 0)
    col = lax.broadcasted_iota(jnp.int32, (tq, tk), 1)
    neg_suffix = jnp.where(lax.broadcasted_iota(jnp.int32, (tk, tk), 0)
                           > lax.broadcasted_iota(jnp.int32, (tk, tk), 1), -1.0, 0.0).astype(BF16)

    def block(j, carry, diagonal):
        used, acc = carry
        start = pl.multiple_of(j * tk, tk)
        k = k_ref[pl.ds(start, tk), :]
        v = v_ref[pl.ds(start, tk), :]
        if diagonal:
            valid = col + (j * tk - i * tq) < row
        new_used, pvs = [], []
        for h in range(2):
            z = lax.dot_general(q_heads[h], k, NT_DIMS, preferred_element_type=F32)
            sp = jnp.maximum(z, 0.0) + jnp.log2(1.0 + jnp.exp2(-jnp.abs(z)))
            if diagonal:
                sp = jnp.where(valid, sp, 0.0)
            later = jnp.dot(sp.astype(BF16), neg_suffix, preferred_element_type=F32) - used[h]
            w = jnp.exp2(z - sp + later)
            if diagonal:
                w = jnp.where(valid, w, 0.0)
            pvs.append(jnp.dot(w.astype(BF16), v, preferred_element_type=F32))
            new_used.append(used[h] + jnp.sum(sp, axis=-1, keepdims=True))
        return tuple(new_used), acc + jnp.where(first, pvs[0], pvs[1])

    carry = ((jnp.zeros((tq, 1), F32),) * 2, jnp.zeros((tq, LANES), F32))
    for d in range(per):
        carry = block((i + 1) * per - 1 - d, carry, True)
    _, acc = lax.fori_loop(0, i * per, lambda t, c: block(i * per - 1 - t, c, False), carry)
    o_ref[...] = acc.astype(o_ref.dtype)


def sb_attention(qkv, *, tq=512, tk=256):
    b, s, w3 = qkv.shape
    n_pairs = w3 // 3 // LANES
    return pl.pallas_call(
        functools.partial(_sb_kernel, tq=tq, tk=tk),
        out_shape=jax.ShapeDtypeStruct((b, s, w3 // 3), BF16),
        grid=(b, n_pairs, s // tq),
        in_specs=[pl.BlockSpec((None, tq, LANES), lambda bi, p, i: (bi, i, p)),
                  pl.BlockSpec((None, s, LANES), lambda bi, p, i: (bi, 0, n_pairs + p)),
                  pl.BlockSpec((None, s, LANES), lambda bi, p, i: (bi, 0, 2 * n_pairs + p))],
        out_specs=pl.BlockSpec((None, tq, LANES), lambda bi, p, i: (bi, i, p)),
        compiler_params=_params("parallel", "parallel", "arbitrary"),
        name="sb_attention",
    )(qkv, qkv, qkv)


def _conv_kernel(x_ref, halo_ref, w_ref, colscale_ref, o_ref):
    i = pl.program_id(1)
    halo = jnp.where(i == 0, 0.0, halo_ref[...])
    ext = jnp.concatenate([halo, x_ref[...]], axis=0)
    y = ext * w_ref[CONV_WIDTH - 1:CONV_WIDTH, :]
    for back in range(1, CONV_WIDTH):
        y = y + pltpu.roll(ext, back, 0) * w_ref[CONV_WIDTH - 1 - back:CONV_WIDTH - back, :]
    y = y[CONV_HALO:, :]
    o_ref[...] = (y * _sigmoid(y) * colscale_ref[...]).astype(o_ref.dtype)


def conv_silu(x, conv_w, colscale, *, rows=512):
    b, s, c = x.shape
    per = rows // CONV_HALO
    return pl.pallas_call(
        _conv_kernel,
        out_shape=jax.ShapeDtypeStruct((b, s, c), BF16),
        grid=(b, s // rows),
        in_specs=[pl.BlockSpec((None, rows, c), lambda bi, i: (bi, i, 0)),
                  pl.BlockSpec((None, CONV_HALO, c),
                               lambda bi, i: (bi, jnp.maximum(i * per - 1, 0), 0)),
                  pl.BlockSpec((CONV_WIDTH, c), lambda bi, i: (0, 0)),
                  pl.BlockSpec((1, c), lambda bi, i: (0, 0))],
        out_specs=pl.BlockSpec((None, rows, c), lambda bi, i: (bi, i, 0)),
        compiler_params=_params("parallel", "parallel"),
        name="conv_silu",
    )(x, x, conv_w, colscale)


def _mlstm_kernel(bi_ref, bf_ref, q_ref, k_ref, v_ref, vt_ref, li_ref, lf_ref, o_ref, g_ref,
                  out_ref, c_scr, n_scr, m_scr, *, chunks):
    head = pl.program_id(1)
    step = pl.program_id(2)
    L = MLSTM_CHUNK

    @pl.when(step == 0)
    def _():
        c_scr[...] = jnp.zeros_like(c_scr)
        n_scr[...] = jnp.zeros_like(n_scr)
        m_scr[...] = jnp.zeros_like(m_scr)

    b_i = bi_ref[head]
    b_f = bf_ref[head]
    r_idx = lax.broadcasted_iota(jnp.int32, (L, L), 0)
    c_idx = lax.broadcasted_iota(jnp.int32, (L, L), 1)
    lower = r_idx >= c_idx
    tri_l = lower.astype(F32)
    tri_u = (r_idx <= c_idx).astype(F32)

    c_st = c_scr[...]
    n_st = n_scr[...]
    m_st = m_scr[...]
    for c in range(chunks):
        r0 = c * L
        qc = q_ref[r0:r0 + L, :]
        kc = k_ref[r0:r0 + L, :]
        vc = v_ref[r0:r0 + L, :]
        vtc = vt_ref[c]
        lf_row = _log_sigmoid(lf_ref[c:c + 1, :] + b_f)
        li_row = li_ref[c:c + 1, :] + b_i
        lf_rows = jnp.broadcast_to(lf_row, (L, L))
        bcum_col = lax.dot_general(tri_l, lf_rows, NT_DIMS, preferred_element_type=F32,
                                   precision=HIGHEST)
        bcum_row = jnp.dot(lf_rows[0:8, :], tri_u, preferred_element_type=F32,
                           precision=HIGHEST)[0:1, :]
        g = jnp.sum(lf_row, axis=-1, keepdims=True)

        log_d = jnp.where(lower, bcum_col - bcum_row + li_row, NEG_INF)
        inter = bcum_col[:, 0:1] + m_st
        m_out = jnp.maximum(inter, jnp.max(log_d, axis=-1, keepdims=True))
        d_mat = jnp.exp(log_d - m_out)
        inter_w = jnp.exp(inter - m_out)
        qk = lax.dot_general(qc, kc, NT_DIMS, preferred_element_type=F32) * d_mat
        num = (jnp.dot(qk.astype(BF16), vc, preferred_element_type=F32)
               + inter_w * lax.dot_general(qc, c_st.astype(BF16), NT_DIMS,
                                           preferred_element_type=F32))
        den = (jnp.sum(qk, axis=-1, keepdims=True)
               + inter_w * jnp.sum(qc.astype(F32) * n_st, axis=-1, keepdims=True))
        hh = num / jnp.maximum(jnp.abs(den), jnp.exp(-m_out))
        hn = _rms(hh, g_ref[...])
        out_ref[r0:r0 + L, :] = (hn * _sigmoid(o_ref[r0:r0 + L, :])).astype(out_ref.dtype)

        a_row = g - bcum_row + li_row
        m_loc = jnp.max(a_row, axis=-1, keepdims=True)
        w_row = jnp.exp(a_row - m_loc)
        c_loc = jnp.dot((vtc.astype(F32) * w_row).astype(BF16), kc, preferred_element_type=F32)
        n_loc = jnp.dot(jnp.broadcast_to(w_row, (8, L)).astype(BF16), kc,
                        preferred_element_type=F32)[0:1, :]
        m_new = jnp.maximum(g + m_st, m_loc)
        decay = jnp.exp(g + m_st - m_new)
        fresh = jnp.exp(m_loc - m_new)
        c_st = decay * c_st + fresh * c_loc
        n_st = decay * n_st + fresh * n_loc
        m_st = m_new
    c_scr[...] = c_st
    n_scr[...] = n_st
    m_scr[...] = m_st


def mlstm(qk, v, v_t, li, lf, o_gate, b_i, b_f, head_g, *, chunks=8):
    b, s, w = v.shape
    n_heads = w // LANES
    rows = chunks * MLSTM_CHUNK
    grid_spec = pltpu.PrefetchScalarGridSpec(
        num_scalar_prefetch=2,
        grid=(b, n_heads, s // rows),
        in_specs=[pl.BlockSpec((None, rows, LANES), lambda bi, h, c, *_: (bi, c, h)),
                  pl.BlockSpec((None, rows, LANES), lambda bi, h, c, *_: (bi, c, n_heads + h)),
                  pl.BlockSpec((None, rows, LANES), lambda bi, h, c, *_: (bi, c, h)),
                  pl.BlockSpec((None, None, chunks, LANES, MLSTM_CHUNK),
                               lambda bi, h, c, *_: (bi, h, c, 0, 0)),
                  pl.BlockSpec((None, None, chunks, MLSTM_CHUNK), lambda bi, h, c, *_: (bi, h, c, 0)),
                  pl.BlockSpec((None, None, chunks, MLSTM_CHUNK), lambda bi, h, c, *_: (bi, h, c, 0)),
                  pl.BlockSpec((None, rows, LANES), lambda bi, h, c, *_: (bi, c, h)),
                  pl.BlockSpec((None, 1, LANES), lambda bi, h, c, *_: (h, 0, 0))],
        out_specs=pl.BlockSpec((None, rows, LANES), lambda bi, h, c, *_: (bi, c, h)),
        scratch_shapes=[pltpu.VMEM((LANES, LANES), F32), pltpu.VMEM((1, LANES), F32),
                        pltpu.VMEM((1, 1), F32)])
    return pl.pallas_call(
        functools.partial(_mlstm_kernel, chunks=chunks),
        out_shape=jax.ShapeDtypeStruct((b, s, w), BF16),
        grid_spec=grid_spec,
        compiler_params=_params("parallel", "parallel", "arbitrary"),
        name="mlstm",
    )(b_i, b_f, qk, qk, v, v_t, li, lf, o_gate, head_g.reshape(n_heads, 1, LANES))


def _out_proj_kernel(x_ref, a_ref, b_ref, w_ref, o_ref):
    na = a_ref.shape[1]
    acc = jnp.dot(a_ref[...], w_ref[0:na, :], preferred_element_type=F32)
    acc = acc + jnp.dot(b_ref[...], w_ref[na:, :], preferred_element_type=F32)
    o_ref[...] = x_ref[...] + acc


def out_proj_residual(x, a, b_part, w, *, tm=512):
    t, d = x.shape
    return pl.pallas_call(
        _out_proj_kernel,
        out_shape=jax.ShapeDtypeStruct((t, d), F32),
        grid=(t // tm,),
        in_specs=[pl.BlockSpec((tm, d), lambda i: (i, 0)),
                  pl.BlockSpec((tm, a.shape[1]), lambda i: (i, 0)),
                  pl.BlockSpec((tm, b_part.shape[1]), lambda i: (i, 0)),
                  pl.BlockSpec(w.shape, lambda i: (0, 0))],
        out_specs=pl.BlockSpec((tm, d), lambda i: (i, 0)),
        compiler_params=_params("parallel"),
        name="out_proj_residual",
    )(x, a, b_part, w)


def _ffn_kernel(x_ref, g_ref, wg_ref, wu_ref, wd_ref, o_ref, h_scr, acc_scr):
    j = pl.program_id(1)

    @pl.when(j == 0)
    def _():
        h_scr[...] = _rms(x_ref[...], g_ref[...]).astype(BF16)
        acc_scr[...] = jnp.zeros_like(acc_scr)

    h = h_scr[...]
    gate = jnp.dot(h, wg_ref[...], preferred_element_type=F32)
    up = jnp.dot(h, wu_ref[...], preferred_element_type=F32)
    act = (gate * _sigmoid(gate) * up).astype(BF16)
    acc_scr[...] += jnp.dot(act, wd_ref[...], preferred_element_type=F32)

    @pl.when(j == pl.num_programs(1) - 1)
    def _():
        o_ref[...] = x_ref[...] + acc_scr[...]


def ffn_residual(x, g, w_gate, w_up, w_down, *, tm=1024, tf=512):
    t, d = x.shape
    ff = w_gate.shape[1]
    return pl.pallas_call(
        _ffn_kernel,
        out_shape=jax.ShapeDtypeStruct((t, d), F32),
        grid=(t // tm, ff // tf),
        in_specs=[pl.BlockSpec((tm, d), lambda i, j: (i, 0)),
                  pl.BlockSpec((1, d), lambda i, j: (0, 0)),
                  pl.BlockSpec((d, tf), lambda i, j: (0, j)),
                  pl.BlockSpec((d, tf), lambda i, j: (0, j)),
                  pl.BlockSpec((tf, d), lambda i, j: (j, 0))],
        out_specs=pl.BlockSpec((tm, d), lambda i, j: (i, 0)),
        scratch_shapes=[pltpu.VMEM((tm, d), BF16), pltpu.VMEM((tm, d), F32)],
        compiler_params=_params("parallel", "arbitrary"),
        name="ffn_residual",
    )(x, g.reshape(1, d), w_gate, w_up, w_down)


def _router_kernel(x_ref, g_ref, wr_ref, h_ref, route_ref):
    h = _rms(x_ref[...], g_ref[...])
    h_ref[...] = h
    logits = jnp.dot(h, wr_ref[...], preferred_element_type=F32, precision=HIGHEST)
    lane = lax.broadcasted_iota(jnp.int32, logits.shape, 1)
    logits = jnp.where(lane < N_EXPERTS, logits, NEG_INF)
    m1 = jnp.max(logits, axis=-1, keepdims=True)
    i1 = jnp.min(jnp.where(logits == m1, lane, LANES), axis=-1, keepdims=True)
    rest = jnp.where(lane == i1, NEG_INF, logits)
    m2 = jnp.max(rest, axis=-1, keepdims=True)
    i2 = jnp.min(jnp.where(rest == m2, lane, LANES), axis=-1, keepdims=True)
    e2 = jnp.exp(m2 - m1)
    g1 = 1.0 / (1.0 + e2)
    g2 = e2 / (1.0 + e2)
    route_ref[...] = jnp.where(lane == 0, i1.astype(F32),
                               jnp.where(lane == 1, i2.astype(F32),
                                         jnp.where(lane == 2, g1, jnp.where(lane == 3, g2, 0.0))))


def router(x, g, w_router_padded, *, tm=512):
    t, d = x.shape
    return pl.pallas_call(
        _router_kernel,
        out_shape=[jax.ShapeDtypeStruct((t, d), F32), jax.ShapeDtypeStruct((t, LANES), F32)],
        grid=(t // tm,),
        in_specs=[pl.BlockSpec((tm, d), lambda i: (i, 0)),
                  pl.BlockSpec((1, d), lambda i: (0, 0)),
                  pl.BlockSpec((d, LANES), lambda i: (0, 0))],
        out_specs=[pl.BlockSpec((tm, d), lambda i: (i, 0)),
                   pl.BlockSpec((tm, LANES), lambda i: (i, 0))],
        compiler_params=_params("parallel"),
        name="router",
    )(x, g.reshape(1, d), w_router_padded)


def _gather_kernel(idx_ref, src_ref, dst_ref, sem, *, rows):
    def issue(r, _):
        pltpu.make_async_copy(src_ref.at[pl.ds(idx_ref[0, r], 1)],
                              dst_ref.at[pl.ds(r, 1)], sem).start()
        return 0

    lax.fori_loop(0, rows, issue, 0)
    pltpu.make_async_copy(src_ref.at[pl.ds(0, rows)], dst_ref, sem).wait()


def gather_rows(src, idx, *, rows=512):
    n = idx.shape[0]
    d = src.shape[1]
    return pl.pallas_call(
        functools.partial(_gather_kernel, rows=rows),
        out_shape=jax.ShapeDtypeStruct((n, d), src.dtype),
        grid=(n // rows,),
        in_specs=[pl.BlockSpec((None, 1, rows), lambda i: (i, 0, 0), memory_space=pltpu.SMEM),
                  pl.BlockSpec(memory_space=pl.ANY)],
        out_specs=pl.BlockSpec((rows, d), lambda i: (i, 0)),
        scratch_shapes=[pltpu.SemaphoreType.DMA],
        compiler_params=_params("arbitrary"),
        name="gather_rows",
    )(idx.reshape(n // rows, 1, rows), src)


def _expert_kernel(be_ref, nused_ref, x_ref, wg_ref, wu_ref, wd_ref, o_ref, h_scr, acc_scr):
    i = pl.program_id(0)
    j = pl.program_id(1)
    last = pl.num_programs(1) - 1
    used = i < nused_ref[0]

    @pl.when(jnp.logical_and(used, j == 0))
    def _():
        h_scr[...] = x_ref[...].astype(BF16)
        acc_scr[...] = jnp.zeros_like(acc_scr)

    @pl.when(used)
    def _():
        h = h_scr[...]
        gate = jnp.dot(h, wg_ref[...], preferred_element_type=F32)
        up = jnp.dot(h, wu_ref[...], preferred_element_type=F32)
        act = (gate * _sigmoid(gate) * up).astype(BF16)
        acc_scr[...] += jnp.dot(act, wd_ref[...], preferred_element_type=F32)

    @pl.when(jnp.logical_and(used, j == last))
    def _():
        o_ref[...] = acc_scr[...]

    @pl.when(jnp.logical_and(jnp.logical_not(used), j == last))
    def _():
        o_ref[...] = jnp.zeros_like(o_ref)


def expert_ffn(xg, block_expert, n_used, w_gate, w_up, w_down, *, tm, tf=512):
    n_rows, d = xg.shape
    ff = w_gate.shape[2]

    def row_block(i, j, be, nu):
        return (jnp.minimum(i, nu[0] - 1), 0)

    def up_block(i, j, be, nu):
        return (be[jnp.minimum(i, nu[0] - 1)], 0, jnp.where(i < nu[0], j, ff // tf - 1))

    def down_block(i, j, be, nu):
        return (be[jnp.minimum(i, nu[0] - 1)], jnp.where(i < nu[0], j, ff // tf - 1), 0)

    grid_spec = pltpu.PrefetchScalarGridSpec(
        num_scalar_prefetch=2,
        grid=(n_rows // tm, ff // tf),
        in_specs=[pl.BlockSpec((tm, d), row_block),
                  pl.BlockSpec((None, d, tf), up_block),
                  pl.BlockSpec((None, d, tf), up_block),
                  pl.BlockSpec((None, tf, d), down_block)],
        out_specs=pl.BlockSpec((tm, d), lambda i, j, be, nu: (i, 0)),
        scratch_shapes=[pltpu.VMEM((tm, d), BF16), pltpu.VMEM((tm, d), F32)])
    return pl.pallas_call(
        _expert_kernel,
        out_shape=jax.ShapeDtypeStruct((n_rows, d), F32),
        grid_spec=grid_spec,
        compiler_params=_params("arbitrary", "arbitrary"),
        name="expert_ffn",
    )(block_expert, n_used, xg, w_gate, w_up, w_down)


def _combine_kernel(pos_cur_ref, pos_next_ref, x_ref, route_ref, y_ref, o_ref, buf, sems, *, rows):
    i = pl.program_id(0)
    n = pl.num_programs(0)

    def issue(pos_ref, slot):
        def body(r, _):
            for kk in range(TOP_K):
                pltpu.make_async_copy(y_ref.at[pl.ds(pos_ref[0, TOP_K * r + kk], 1)],
                                      buf.at[slot, kk, pl.ds(r, 1)], sems.at[slot]).start()
            return 0
        lax.fori_loop(0, rows, body, 0)

    @pl.when(i == 0)
    def _():
        issue(pos_cur_ref, 0)

    @pl.when(i + 1 < n)
    def _():
        issue(pos_next_ref, (i + 1) % 2)

    slot = i % 2
    pltpu.make_async_copy(buf.at[slot], buf.at[slot], sems.at[slot]).wait()
    route = route_ref[...]
    g1 = route[:, 2:3]
    g2 = route[:, 3:4]
    o_ref[...] = x_ref[...] + g1 * buf[slot, 0] + g2 * buf[slot, 1]


def combine_residual(x, route, y_rows, pos, *, rows=256):
    t, d = x.shape
    n_steps = t // rows
    grid_spec = pltpu.PrefetchScalarGridSpec(
        num_scalar_prefetch=0,
        grid=(n_steps,),
        in_specs=[pl.BlockSpec((None, 1, TOP_K * rows), lambda i: (i, 0, 0),
                               memory_space=pltpu.SMEM),
                  pl.BlockSpec((None, 1, TOP_K * rows),
                               lambda i: (jnp.minimum(i + 1, n_steps - 1), 0, 0),
                               memory_space=pltpu.SMEM),
                  pl.BlockSpec((rows, d), lambda i: (i, 0)),
                  pl.BlockSpec((rows, LANES), lambda i: (i, 0)),
                  pl.BlockSpec(memory_space=pl.ANY)],
        out_specs=pl.BlockSpec((rows, d), lambda i: (i, 0)),
        scratch_shapes=[pltpu.VMEM((2, TOP_K, rows, d), F32), pltpu.SemaphoreType.DMA((2,))])
    return pl.pallas_call(
        functools.partial(_combine_kernel, rows=rows),
        out_shape=jax.ShapeDtypeStruct((t, d), F32),
        grid_spec=grid_spec,
        compiler_params=_params("arbitrary"),
        name="combine_residual",
    )(pos.reshape(n_steps, 1, TOP_K * rows), pos.reshape(n_steps, 1, TOP_K * rows), x, route, y_rows)


def _final_norm_kernel(x_ref, g_ref, o_ref):
    o_ref[...] = _rms(x_ref[...], g_ref[...])


def final_norm(x, g, *, tm=1024):
    t, d = x.shape
    return pl.pallas_call(
        _final_norm_kernel,
        out_shape=jax.ShapeDtypeStruct((t, d), F32),
        grid=(t // tm,),
        in_specs=[pl.BlockSpec((tm, d), lambda i: (i, 0)), pl.BlockSpec((1, d), lambda i: (0, 0))],
        out_specs=pl.BlockSpec((tm, d), lambda i: (i, 0)),
        compiler_params=_params("parallel"),
        name="final_norm",
    )(x, g.reshape(1, d))


def _pad_cols(w, n):
    return jnp.pad(w, ((0, 0), (0, n - w.shape[1])))


def _scale_q_cols(w_qkv, width):
    scale = jnp.concatenate([jnp.full((width,), LOG2E * HEAD_DIM ** -0.5, F32),
                             jnp.ones((2 * width,), F32)])
    return w_qkv * scale


def even_layer(x, b, s, g_mix, g_ffn, w_in, b_forget, pool_w, pool_scale, w_out,
               w_gate, w_up, w_down, *, fox_tq=512):
    t, d = x.shape
    pool_width = pool_w.shape[0] * pool_w.shape[1]
    n_heads = b_forget.shape[0]
    fox_width = n_heads * HEAD_DIM
    w_a = w_in[:, :pool_width].astype(BF16)
    w_qkv = _scale_q_cols(w_in[:, pool_width:pool_width + 3 * fox_width], fox_width).astype(BF16)
    w_f = _pad_cols(w_in[:, pool_width + 3 * fox_width:], LANES).astype(BF16)
    qkv, a_in, f_pre = norm_proj(x, g_mix, [w_qkv, w_a, w_f], [BF16, F32, F32])

    pool_bd = jax.scipy.linalg.block_diag(*[pool_w[i] for i in range(pool_w.shape[0])]).astype(BF16)
    a_out, cum = pool_and_cum(a_in.reshape(b, s, pool_width), f_pre.reshape(b, s, LANES),
                              _pad_cols(b_forget.reshape(1, n_heads), LANES), pool_bd,
                              pool_scale.reshape(1, pool_width))
    cum_blocks = cum[:, :, :n_heads].reshape(b, s // fox_tq, fox_tq, n_heads // 2, 2)
    cum_blocks = cum_blocks.transpose(0, 3, 1, 4, 2)
    attn = fox_attention(qkv.reshape(b, s, 3 * fox_width), cum_blocks, tq=fox_tq)
    x = out_proj_residual(x, a_out.reshape(t, pool_width), attn.reshape(t, fox_width),
                          w_out.astype(BF16))
    return ffn_residual(x, g_ffn, w_gate.astype(BF16), w_up.astype(BF16), w_down.astype(BF16))


def odd_layer(x, b, s, g_mix, g_ffn, w_in, b_igate, b_fgate, conv_w, head_norm_g, w_out,
              w_router, w_gate, w_up, w_down):
    x = odd_mixer_block(x, b, s, g_mix, w_in, b_igate, b_fgate, conv_w, head_norm_g, w_out)
    return moe_block(x, g_ffn, w_router, w_gate, w_up, w_down)


def odd_mixer_block(x, b, s, g_mix, w_in, b_igate, b_fgate, conv_w, head_norm_g, w_out):
    t, d = x.shape
    n_ml = b_igate.shape[0]
    ml_width = n_ml * LANES
    sb_width = d - ml_width
    cols = [3 * sb_width, 2 * ml_width, ml_width, ml_width]
    edges = [0]
    for c in cols:
        edges.append(edges[-1] + c)
    w_sb, w_mqk, w_mv, w_mo = [w_in[:, edges[i]:edges[i + 1]] for i in range(4)]
    w_sb = _scale_q_cols(w_sb, sb_width)
    w_sb, w_mqk, w_mv, w_mo = [w.astype(BF16) for w in (w_sb, w_mqk, w_mv, w_mo)]
    w_gates = _pad_cols(w_in[:, edges[-1]:], LANES).astype(BF16)
    sqkv, ml_qk, ml_v, ml_o, gates = norm_proj(
        x, g_mix, [w_sb, w_mqk, w_mv, w_mo, w_gates], [BF16, F32, BF16, F32, F32])

    c_out = sb_attention(sqkv.reshape(b, s, 3 * sb_width))

    k_scale = jnp.concatenate([jnp.ones((ml_width,), F32),
                               jnp.full((ml_width,), LANES ** -0.5, F32)]).reshape(1, 2 * ml_width)
    qk = conv_silu(ml_qk.reshape(b, s, 2 * ml_width), conv_w, k_scale)
    nc = s // MLSTM_CHUNK
    v3 = ml_v.reshape(b, s, ml_width)
    v_t = v3.reshape(b, nc, MLSTM_CHUNK, n_ml, LANES).transpose(0, 3, 1, 4, 2)
    gate_rows = gates[:, :2 * n_ml].reshape(b, nc, MLSTM_CHUNK, 2 * n_ml).transpose(0, 3, 1, 2)
    d_out = mlstm(qk, v3, v_t, gate_rows[:, :n_ml], gate_rows[:, n_ml:],
                  ml_o.reshape(b, s, ml_width), b_igate, b_fgate, head_norm_g)
    return out_proj_residual(x, c_out.reshape(t, sb_width), d_out.reshape(t, ml_width),
                             w_out.astype(BF16))


def moe_block(x, g_ffn, w_router, w_gate, w_up, w_down, *, expert_rows=512):
    t, d = x.shape
    h, route = router(x, g_ffn, _pad_cols(w_router, LANES))
    flat_e = route[:, :TOP_K].astype(jnp.int32).reshape(-1)
    n_assign = t * TOP_K
    onehot = jax.nn.one_hot(flat_e, N_EXPERTS, dtype=jnp.int32)
    rank = jnp.sum((jnp.cumsum(onehot, axis=0) - onehot) * onehot, axis=-1)
    counts = jnp.sum(onehot, axis=0)
    padded = (counts + expert_rows - 1) // expert_rows * expert_rows
    group_end = jnp.cumsum(padded)
    pos = (group_end[flat_e] - padded[flat_e] + rank).astype(jnp.int32)
    n_rows = n_assign + N_EXPERTS * expert_rows
    n_blocks = n_rows // expert_rows
    flat_tok = jnp.arange(n_assign, dtype=jnp.int32) // TOP_K
    row_tok = jnp.zeros((n_rows,), jnp.int32).at[pos].set(flat_tok)
    block_expert = jnp.minimum(
        jnp.searchsorted(group_end, jnp.arange(n_blocks) * expert_rows, side='right'),
        N_EXPERTS - 1).astype(jnp.int32)
    n_used = (group_end[-1:] // expert_rows).astype(jnp.int32)

    xg = gather_rows(h, row_tok)
    y_rows = expert_ffn(xg, block_expert, n_used, w_gate.astype(BF16), w_up.astype(BF16),
                        w_down.astype(BF16), tm=expert_rows)
    return combine_residual(x, route, y_rows, pos)


def kernel(x, norm_mix_g, norm_ffn_g, norm_final_g, ev_w_in, ev_b_forget, ev_pool_w, ev_pool_scale, ev_w_out, ffn_w_gate, ffn_w_up, ffn_w_down, od_w_in, od_b_igate, od_b_fgate, od_conv_w, od_head_norm_g, od_w_out, moe_w_router, moe_w_gate, moe_w_up, moe_w_down):
    b, s, d = x.shape
    depth = norm_mix_g.shape[0]
    xt = x.reshape(b * s, d)
    for layer in range(depth):
        e = layer // 2
        if layer % 2 == 0:
            xt = even_layer(xt, b, s, norm_mix_g[layer], norm_ffn_g[layer], ev_w_in[e],
                            ev_b_forget[e], ev_pool_w[e], ev_pool_scale[e], ev_w_out[e],
                            ffn_w_gate[e], ffn_w_up[e], ffn_w_down[e])
        else:
            xt = odd_layer(xt, b, s, norm_mix_g[layer], norm_ffn_g[layer], od_w_in[e],
                           od_b_igate[e], od_b_fgate[e], od_conv_w[e], od_head_norm_g[e],
                           od_w_out[e], moe_w_router[e], moe_w_gate[e], moe_w_up[e], moe_w_down[e])
    return final_norm(xt, norm_final_g).reshape(b, s, d)
```

```python
import functools

import jax
import jax.numpy as jnp
from jax import lax
from jax.experimental import pallas as pl
from jax.experimental.pallas import tpu as pltpu

F32 = jnp.float32
BF16 = jnp.bfloat16
EPS = 1e-6
NEG_INF = float("-inf")

LANES = 128
HEAD_DIM = 64
POOL_WINDOWS = (2, 4, 8, 16)
POOL_HALO = 16
CONV_WIDTH = 4
CONV_HALO = 8
MLSTM_CHUNK = 64
N_EXPERTS = 8
TOP_K = 2
VMEM_LIMIT = 56 * 1024 * 1024

LOG2E = 1.4426950408889634
SOFTPLUS2_CLAMP = 64.0
HIGHEST = lax.Precision.HIGHEST
NT_DIMS = (((1,), (1,)), ((), ()))


def _params(*semantics):
    return pltpu.CompilerParams(dimension_semantics=semantics, vmem_limit_bytes=VMEM_LIMIT)


def _rms(x, g):
    return x * lax.rsqrt(jnp.mean(x * x, axis=-1, keepdims=True) + EPS) * g


def _log_sigmoid(x):
    return jnp.minimum(x, 0.0) - jnp.log(1.0 + jnp.exp(-jnp.abs(x)))


def _sigmoid(x):
    return 1.0 / (1.0 + jnp.exp(-x))


def _norm_proj_kernel(x_ref, g_ref, *refs, n_out, col_chunk):
    w_refs, o_refs = refs[:n_out], refs[n_out:]
    h = _rms(x_ref[...], g_ref[...]).astype(BF16)
    for w_ref, o_ref in zip(w_refs, o_refs):
        n = w_ref.shape[1]
        for c0 in range(0, n, col_chunk):
            c1 = min(c0 + col_chunk, n)
            o_ref[:, c0:c1] = jnp.dot(
                h, w_ref[:, c0:c1], preferred_element_type=F32).astype(o_ref.dtype)


def norm_proj(x, g, weights, out_dtypes, *, tm=512, col_chunk=512):
    t, d = x.shape
    n_out = len(weights)
    return pl.pallas_call(
        functools.partial(_norm_proj_kernel, n_out=n_out, col_chunk=col_chunk),
        out_shape=[jax.ShapeDtypeStruct((t, w.shape[1]), dt) for w, dt in zip(weights, out_dtypes)],
        grid=(t // tm,),
        in_specs=[pl.BlockSpec((tm, d), lambda i: (i, 0)),
                  pl.BlockSpec((1, d), lambda i: (0, 0))]
                 + [pl.BlockSpec(w.shape, lambda i: (0, 0)) for w in weights],
        out_specs=[pl.BlockSpec((tm, w.shape[1]), lambda i: (i, 0)) for w in weights],
        compiler_params=_params("parallel"),
        name="norm_proj",
    )(x, g.reshape(1, d), *weights)


def _pool_kernel(a_ref, f_ref, bf_ref, w_ref, scale_ref, o_ref, cum_ref, *, rows):
    s, width = a_ref.shape
    group = lax.broadcasted_iota(jnp.int32, (1, width), 1) // (width // len(POOL_WINDOWS))
    window = jnp.where(group == 0, POOL_WINDOWS[0],
                       jnp.where(group == 1, POOL_WINDOWS[1],
                                 jnp.where(group == 2, POOL_WINDOWS[2], POOL_WINDOWS[3])))
    tri = (lax.broadcasted_iota(jnp.int32, (rows, rows), 0)
           >= lax.broadcasted_iota(jnp.int32, (rows, rows), 1)).astype(F32)
    carry = jnp.zeros((1, f_ref.shape[1]), F32)
    for r in range(s // rows):
        r0 = r * rows
        if r == 0:
            ext = jnp.concatenate([jnp.zeros((POOL_HALO, width), F32), a_ref[0:rows, :]], axis=0)
        else:
            ext = a_ref[r0 - POOL_HALO:r0 + rows, :]
        s2 = ext + pltpu.roll(ext, 1, 0)
        s4 = s2 + pltpu.roll(s2, 2, 0)
        s8 = s4 + pltpu.roll(s4, 4, 0)
        s16 = s8 + pltpu.roll(s8, 8, 0)
        wsum = jnp.where(group == 0, s2, jnp.where(group == 1, s4, jnp.where(group == 2, s8, s16)))
        wsum = wsum[POOL_HALO:, :]
        a = ext[POOL_HALO:, :]
        t1 = r0 + 1 + lax.broadcasted_iota(jnp.int32, (rows, 1), 0)
        count = jnp.minimum(t1, window).astype(F32)
        pooled = wsum / count - a
        mixed = jnp.dot(pooled.astype(BF16), w_ref[...], preferred_element_type=F32)
        o_ref[r0:r0 + rows, :] = (mixed * scale_ref[...]).astype(o_ref.dtype)
        lf = _log_sigmoid(f_ref[r0:r0 + rows, :] + bf_ref[...]) * LOG2E
        cum = jnp.dot(tri, lf, preferred_element_type=F32, precision=HIGHEST) + carry
        cum_ref[r0:r0 + rows, :] = cum
        carry = cum[rows - 1:rows, :]


def pool_and_cum(a_in, f_pre, b_forget, pool_w_bd, pool_scale, *, rows=256):
    b, s, width = a_in.shape
    fw = f_pre.shape[-1]
    return pl.pallas_call(
        functools.partial(_pool_kernel, rows=rows),
        out_shape=[jax.ShapeDtypeStruct((b, s, width), BF16),
                   jax.ShapeDtypeStruct((b, s, fw), F32)],
        grid=(b,),
        in_specs=[pl.BlockSpec((None, s, width), lambda i: (i, 0, 0)),
                  pl.BlockSpec((None, s, fw), lambda i: (i, 0, 0)),
                  pl.BlockSpec((1, fw), lambda i: (0, 0)),
                  pl.BlockSpec((width, width), lambda i: (0, 0)),
                  pl.BlockSpec((1, width), lambda i: (0, 0))],
        out_specs=[pl.BlockSpec((None, s, width), lambda i: (i, 0, 0)),
                   pl.BlockSpec((None, s, fw), lambda i: (i, 0, 0))],
        compiler_params=_params("parallel"),
        name="pool_and_cum",
    )(a_in, f_pre, b_forget, pool_w_bd, pool_scale)


def _fox_kernel(q_ref, k_ref, v_ref, cum_ref, o_ref, *, tq):
    i = pl.program_id(2)
    lane = lax.broadcasted_iota(jnp.int32, (1, LANES), 1)
    first = lane < HEAD_DIM
    q = q_ref[...]
    q_heads = (jnp.where(first, q, jnp.zeros_like(q)), jnp.where(first, jnp.zeros_like(q), q))
    row = lax.broadcasted_iota(jnp.int32, (tq, tq), 0)
    col = lax.broadcasted_iota(jnp.int32, (tq, tq), 1)

    def logits(j):
        start = pl.multiple_of(j * tq, tq)
        k = k_ref[pl.ds(start, tq), :]
        return tuple(lax.dot_general(q_heads[h], k, NT_DIMS, preferred_element_type=F32)
                     for h in range(2))

    def block(j, scs, carry, masked):
        ms, accs = carry
        start = pl.multiple_of(j * tq, tq)
        v = v_ref[pl.ds(start, tq), :]
        ones = jnp.ones_like(v)
        v_heads = (jnp.where(first, v, ones), jnp.where(first, ones, v))
        ck = cum_ref[j]
        new_ms, new_accs = [], []
        for h in range(2):
            sc = scs[h] - ck[h:h + 1, :]
            if masked:
                sc = jnp.where(col <= row, sc, NEG_INF)
            m_new = jnp.maximum(ms[h], jnp.max(sc, axis=-1, keepdims=True))
            alpha = jnp.exp2(ms[h] - m_new)
            p = jnp.exp2((sc - m_new).astype(BF16))
            new_ms.append(m_new)
            new_accs.append(alpha * accs[h] + jnp.dot(p, v_heads[h], preferred_element_type=F32))
        return tuple(new_ms), tuple(new_accs)

    init = ((jnp.full((tq, 1), NEG_INF, F32),) * 2, (jnp.zeros((tq, LANES), F32),) * 2)
    def pair(t, c):
        return block(2 * t + 1, logits(2 * t + 1), block(2 * t, logits(2 * t), c, False), False)

    carry = lax.fori_loop(0, i // 2, pair, init)
    carry = lax.fori_loop(2 * (i // 2), i, lambda j, c: block(j, logits(j), c, False), carry)
    _, accs = block(i, logits(i), carry, True)
    out0 = accs[0] / accs[0][:, HEAD_DIM:HEAD_DIM + 1]
    out1 = accs[1] / accs[1][:, 0:1]
    o_ref[...] = jnp.where(first, out0, out1).astype(o_ref.dtype)


def fox_attention(qkv, cum_blocks, *, tq=512):
    b, s, w3 = qkv.shape
    n_pairs = w3 // 3 // LANES
    return pl.pallas_call(
        functools.partial(_fox_kernel, tq=tq),
        out_shape=jax.ShapeDtypeStruct((b, s, w3 // 3), BF16),
        grid=(b, n_pairs, s // tq),
        in_specs=[pl.BlockSpec((None, tq, LANES), lambda bi, p, i: (bi, i, p)),
                  pl.BlockSpec((None, s, LANES), lambda bi, p, i: (bi, 0, n_pairs + p)),
                  pl.BlockSpec((None, s, LANES), lambda bi, p, i: (bi, 0, 2 * n_pairs + p)),
                  pl.BlockSpec((None, None, s // tq, 2, tq), lambda bi, p, i: (bi, p, 0, 0, 0))],
        out_specs=pl.BlockSpec((None, tq, LANES), lambda bi, p, i: (bi, i, p)),
        compiler_params=_params("parallel", "parallel", "arbitrary"),
        name="fox_attention",
    )(qkv, qkv, qkv, cum_blocks)


def _sb_kernel(q_ref, k_ref, v_ref, o_ref, *, tq, tk):
    i = pl.program_id(2)
    per = tq // tk
    lane = lax.broadcasted_iota(jnp.int32, (1, LANES), 1)
    first = lane < HEAD_DIM
    q = q_ref[...]
    q_heads = (jnp.where(first, q, jnp.zeros_like(q)), jnp.where(first, jnp.zeros_like(q), q))
    row = lax.broadcasted_iota(jnp.int32, (tq, tk), 0)
    col = lax.broadcasted_iota(jnp.int32, (tq, tk), 1)
    neg_suffix = jnp.where(lax.broadcasted_iota(jnp.int32, (tk, tk), 0)
                           > lax.broadcasted_iota(jnp.int32, (tk, tk), 1), -1.0, 0.0).astype(BF16)

    def logits(j):
        start = pl.multiple_of(j * tk, tk)
        k = k_ref[pl.ds(start, tk), :]
        return tuple(lax.dot_general(q_heads[h], k, NT_DIMS, preferred_element_type=F32)
                     for h in range(2))

    def block(j, zs, carry, diagonal):
        used, acc = carry
        start = pl.multiple_of(j * tk, tk)
        v = v_ref[pl.ds(start, tk), :]
        if diagonal:
            valid = col + (j * tk - i * tq) < row
        new_used, pvs = [], []
        for h in range(2):
            z = zs[h]
            sp = jnp.maximum(z, jnp.log2(1.0 + jnp.exp2(jnp.minimum(z, SOFTPLUS2_CLAMP))))
            if diagonal:
                sp = jnp.where(valid, sp, 0.0)
            later_in = jnp.dot(sp.astype(BF16), neg_suffix, preferred_element_type=F32)
            w = jnp.exp2((z - sp + (later_in - used[h])).astype(BF16))
            if diagonal:
                w = jnp.where(valid, w, jnp.zeros_like(w))
            pvs.append(jnp.dot(w, v, preferred_element_type=F32))
            new_used.append(used[h] + (sp[:, 0:1] - later_in[:, 0:1]))
        return tuple(new_used), acc + jnp.where(first, pvs[0], pvs[1])

    carry = ((jnp.zeros((tq, 1), F32),) * 2, jnp.zeros((tq, LANES), F32))
    for d in range(per):
        j = (i + 1) * per - 1 - d
        carry = block(j, logits(j), carry, True)

    def pair(t, carry):
        j = i * per - 1 - 2 * t
        return block(j - 1, logits(j - 1), block(j, logits(j), carry, False), False)

    assert per % 2 == 0
    _, acc = lax.fori_loop(0, i * per // 2, pair, carry)
    o_ref[...] = acc.astype(o_ref.dtype)


def sb_attention(qkv, *, tq=512, tk=256):
    b, s, w3 = qkv.shape
    n_pairs = w3 // 3 // LANES
    return pl.pallas_call(
        functools.partial(_sb_kernel, tq=tq, tk=tk),
        out_shape=jax.ShapeDtypeStruct((b, s, w3 // 3), BF16),
        grid=(b, n_pairs, s // tq),
        in_specs=[pl.BlockSpec((None, tq, LANES), lambda bi, p, i: (bi, i, p)),
                  pl.BlockSpec((None, s, LANES), lambda bi, p, i: (bi, 0, n_pairs + p)),
                  pl.BlockSpec((None, s, LANES), lambda bi, p, i: (bi, 0, 2 * n_pairs + p))],
        out_specs=pl.BlockSpec((None, tq, LANES), lambda bi, p, i: (bi, i, p)),
        compiler_params=_params("parallel", "parallel", "arbitrary"),
        name="sb_attention",
    )(qkv, qkv, qkv)


def _conv_kernel(x_ref, halo_ref, w_ref, colscale_ref, o_ref):
    i = pl.program_id(1)
    halo = jnp.where(i == 0, 0.0, halo_ref[...])
    ext = jnp.concatenate([halo, x_ref[...]], axis=0)
    y = ext * w_ref[CONV_WIDTH - 1:CONV_WIDTH, :]
    for back in range(1, CONV_WIDTH):
        y = y + pltpu.roll(ext, back, 0) * w_ref[CONV_WIDTH - 1 - back:CONV_WIDTH - back, :]
    y = y[CONV_HALO:, :]
    o_ref[...] = (y * _sigmoid(y) * colscale_ref[...]).astype(o_ref.dtype)


def conv_silu(x, conv_w, colscale, *, rows=512):
    b, s, c = x.shape
    per = rows // CONV_HALO
    return pl.pallas_call(
        _conv_kernel,
        out_shape=jax.ShapeDtypeStruct((b, s, c), BF16),
        grid=(b, s // rows),
        in_specs=[pl.BlockSpec((None, rows, c), lambda bi, i: (bi, i, 0)),
                  pl.BlockSpec((None, CONV_HALO, c),
                               lambda bi, i: (bi, jnp.maximum(i * per - 1, 0), 0)),
                  pl.BlockSpec((CONV_WIDTH, c), lambda bi, i: (0, 0)),
                  pl.BlockSpec((1, c), lambda bi, i: (0, 0))],
        out_specs=pl.BlockSpec((None, rows, c), lambda bi, i: (bi, i, 0)),
        compiler_params=_params("parallel", "parallel"),
        name="conv_silu",
    )(x, x, conv_w, colscale)


def _mlstm_kernel(bi_ref, bf_ref, q_ref, k_ref, v_ref, vt_ref, li_ref, lf_ref, o_ref, g_ref,
                  out_ref, c_scr, n_scr, m_scr, *, chunks):
    head = pl.program_id(1)
    step = pl.program_id(2)
    L = MLSTM_CHUNK

    @pl.when(step == 0)
    def _():
        c_scr[...] = jnp.zeros_like(c_scr)
        n_scr[...] = jnp.zeros_like(n_scr)
        m_scr[...] = jnp.zeros_like(m_scr)

    b_i = bi_ref[head]
    b_f = bf_ref[head]
    r_idx = lax.broadcasted_iota(jnp.int32, (L, L), 0)
    c_idx = lax.broadcasted_iota(jnp.int32, (L, L), 1)
    lower = r_idx >= c_idx
    tri_l = lower.astype(F32)
    tri_u = (r_idx <= c_idx).astype(F32)

    c_st = c_scr[...]
    n_st = n_scr[...]
    m_st = m_scr[...]
    for c in range(chunks):
        r0 = c * L
        qc = q_ref[r0:r0 + L, :]
        kc = k_ref[r0:r0 + L, :]
        vc = v_ref[r0:r0 + L, :]
        vtc = vt_ref[c]
        lf_row = _log_sigmoid(lf_ref[c:c + 1, :] + b_f)
        li_row = li_ref[c:c + 1, :] + b_i
        lf_rows = jnp.broadcast_to(lf_row, (L, L))
        bcum_col = lax.dot_general(tri_l, lf_rows, NT_DIMS, preferred_element_type=F32,
                                   precision=HIGHEST)
        bcum_row = jnp.dot(lf_rows[0:8, :], tri_u, preferred_element_type=F32,
                           precision=HIGHEST)[0:1, :]
        g = jnp.sum(lf_row, axis=-1, keepdims=True)

        log_d = jnp.where(lower, bcum_col - bcum_row + li_row, NEG_INF)
        inter = bcum_col[:, 0:1] + m_st
        m_out = jnp.maximum(inter, jnp.max(log_d, axis=-1, keepdims=True))
        d_mat = jnp.exp(log_d - m_out)
        inter_w = jnp.exp(inter - m_out)
        qk = lax.dot_general(qc, kc, NT_DIMS, preferred_element_type=F32) * d_mat
        num = (jnp.dot(qk.astype(BF16), vc, preferred_element_type=F32)
               + inter_w * lax.dot_general(qc, c_st.astype(BF16), NT_DIMS,
                                           preferred_element_type=F32))
        den = (jnp.sum(qk, axis=-1, keepdims=True)
               + inter_w * jnp.sum(qc.astype(F32) * n_st, axis=-1, keepdims=True))
        hh = num / jnp.maximum(jnp.abs(den), jnp.exp(-m_out))
        hn = _rms(hh, g_ref[...])
        out_ref[r0:r0 + L, :] = (hn * _sigmoid(o_ref[r0:r0 + L, :])).astype(out_ref.dtype)

        a_row = g - bcum_row + li_row
        m_loc = jnp.max(a_row, axis=-1, keepdims=True)
        w_row = jnp.exp(a_row - m_loc)
        c_loc = jnp.dot((vtc.astype(F32) * w_row).astype(BF16), kc, preferred_element_type=F32)
        n_loc = jnp.dot(jnp.broadcast_to(w_row, (8, L)).astype(BF16), kc,
                        preferred_element_type=F32)[0:1, :]
        m_new = jnp.maximum(g + m_st, m_loc)
        decay = jnp.exp(g + m_st - m_new)
        fresh = jnp.exp(m_loc - m_new)
        c_st = decay * c_st + fresh * c_loc
        n_st = decay * n_st + fresh * n_loc
        m_st = m_new
    c_scr[...] = c_st
    n_scr[...] = n_st
    m_scr[...] = m_st


def mlstm(qk, v, v_t, li, lf, o_gate, b_i, b_f, head_g, *, chunks=8):
    b, s, w = v.shape
    n_heads = w // LANES
    rows = chunks * MLSTM_CHUNK
    grid_spec = pltpu.PrefetchScalarGridSpec(
        num_scalar_prefetch=2,
        grid=(b, n_heads, s // rows),
        in_specs=[pl.BlockSpec((None, rows, LANES), lambda bi, h, c, *_: (bi, c, h)),
                  pl.BlockSpec((None, rows, LANES), lambda bi, h, c, *_: (bi, c, n_heads + h)),
                  pl.BlockSpec((None, rows, LANES), lambda bi, h, c, *_: (bi, c, h)),
                  pl.BlockSpec((None, None, chunks, LANES, MLSTM_CHUNK),
                               lambda bi, h, c, *_: (bi, h, c, 0, 0)),
                  pl.BlockSpec((None, None, chunks, MLSTM_CHUNK), lambda bi, h, c, *_: (bi, h, c, 0)),
                  pl.BlockSpec((None, None, chunks, MLSTM_CHUNK), lambda bi, h, c, *_: (bi, h, c, 0)),
                  pl.BlockSpec((None, rows, LANES), lambda bi, h, c, *_: (bi, c, h)),
                  pl.BlockSpec((None, 1, LANES), lambda bi, h, c, *_: (h, 0, 0))],
        out_specs=pl.BlockSpec((None, rows, LANES), lambda bi, h, c, *_: (bi, c, h)),
        scratch_shapes=[pltpu.VMEM((LANES, LANES), F32), pltpu.VMEM((1, LANES), F32),
                        pltpu.VMEM((1, 1), F32)])
    return pl.pallas_call(
        functools.partial(_mlstm_kernel, chunks=chunks),
        out_shape=jax.ShapeDtypeStruct((b, s, w), BF16),
        grid_spec=grid_spec,
        compiler_params=_params("parallel", "parallel", "arbitrary"),
        name="mlstm",
    )(b_i, b_f, qk, qk, v, v_t, li, lf, o_gate, head_g.reshape(n_heads, 1, LANES))


def _out_proj_kernel(x_ref, a_ref, b_ref, w_ref, o_ref):
    na = a_ref.shape[1]
    acc = jnp.dot(a_ref[...], w_ref[0:na, :], preferred_element_type=F32)
    acc = acc + jnp.dot(b_ref[...], w_ref[na:, :], preferred_element_type=F32)
    o_ref[...] = x_ref[...] + acc


def out_proj_residual(x, a, b_part, w, *, tm=512):
    t, d = x.shape
    return pl.pallas_call(
        _out_proj_kernel,
        out_shape=jax.ShapeDtypeStruct((t, d), F32),
        grid=(t // tm,),
        in_specs=[pl.BlockSpec((tm, d), lambda i: (i, 0)),
                  pl.BlockSpec((tm, a.shape[1]), lambda i: (i, 0)),
                  pl.BlockSpec((tm, b_part.shape[1]), lambda i: (i, 0)),
                  pl.BlockSpec(w.shape, lambda i: (0, 0))],
        out_specs=pl.BlockSpec((tm, d), lambda i: (i, 0)),
        compiler_params=_params("parallel"),
        name="out_proj_residual",
    )(x, a, b_part, w)


def _ffn_kernel(x_ref, g_ref, wg_ref, wu_ref, wd_ref, o_ref, h_scr, acc_scr):
    j = pl.program_id(1)

    @pl.when(j == 0)
    def _():
        h_scr[...] = _rms(x_ref[...], g_ref[...]).astype(BF16)
        acc_scr[...] = jnp.zeros_like(acc_scr)

    h = h_scr[...]
    gate = jnp.dot(h, wg_ref[...], preferred_element_type=F32)
    up = jnp.dot(h, wu_ref[...], preferred_element_type=F32)
    act = (gate * _sigmoid(gate) * up).astype(BF16)
    acc_scr[...] += jnp.dot(act, wd_ref[...], preferred_element_type=F32)

    @pl.when(j == pl.num_programs(1) - 1)
    def _():
        o_ref[...] = x_ref[...] + acc_scr[...]


def ffn_residual(x, g, w_gate, w_up, w_down, layer, *, tm=512, tf=1792):
    t, d = x.shape
    ff = w_gate.shape[2]
    return pl.pallas_call(
        _ffn_kernel,
        out_shape=jax.ShapeDtypeStruct((t, d), F32),
        grid=(t // tm, ff // tf),
        in_specs=[pl.BlockSpec((tm, d), lambda i, j: (i, 0)),
                  pl.BlockSpec((1, d), lambda i, j: (0, 0)),
                  pl.BlockSpec((None, d, tf), lambda i, j: (layer, 0, j)),
                  pl.BlockSpec((None, d, tf), lambda i, j: (layer, 0, j)),
                  pl.BlockSpec((None, tf, d), lambda i, j: (layer, j, 0))],
        out_specs=pl.BlockSpec((tm, d), lambda i, j: (i, 0)),
        scratch_shapes=[pltpu.VMEM((tm, d), BF16), pltpu.VMEM((tm, d), F32)],
        compiler_params=_params("parallel", "arbitrary"),
        name="ffn_residual",
    )(x, g.reshape(1, d), w_gate, w_up, w_down)


def _router_kernel(x_ref, g_ref, wr_ref, h_ref, route_ref):
    h = _rms(x_ref[...], g_ref[...])
    h_ref[...] = h
    logits = jnp.dot(h, wr_ref[...], preferred_element_type=F32, precision=HIGHEST)
    lane = lax.broadcasted_iota(jnp.int32, logits.shape, 1)
    logits = jnp.where(lane < N_EXPERTS, logits, NEG_INF)
    m1 = jnp.max(logits, axis=-1, keepdims=True)
    i1 = jnp.min(jnp.where(logits == m1, lane, LANES), axis=-1, keepdims=True)
    rest = jnp.where(lane == i1, NEG_INF, logits)
    m2 = jnp.max(rest, axis=-1, keepdims=True)
    i2 = jnp.min(jnp.where(rest == m2, lane, LANES), axis=-1, keepdims=True)
    e2 = jnp.exp(m2 - m1)
    g1 = 1.0 / (1.0 + e2)
    g2 = e2 / (1.0 + e2)
    route_ref[...] = jnp.where(lane == 0, i1.astype(F32),
                               jnp.where(lane == 1, i2.astype(F32),
                                         jnp.where(lane == 2, g1, jnp.where(lane == 3, g2, 0.0))))


def router(x, g, w_router_padded, *, tm=512):
    t, d = x.shape
    return pl.pallas_call(
        _router_kernel,
        out_shape=[jax.ShapeDtypeStruct((t, d), F32), jax.ShapeDtypeStruct((t, LANES), F32)],
        grid=(t // tm,),
        in_specs=[pl.BlockSpec((tm, d), lambda i: (i, 0)),
                  pl.BlockSpec((1, d), lambda i: (0, 0)),
                  pl.BlockSpec((d, LANES), lambda i: (0, 0))],
        out_specs=[pl.BlockSpec((tm, d), lambda i: (i, 0)),
                   pl.BlockSpec((tm, LANES), lambda i: (i, 0))],
        compiler_params=_params("parallel"),
        name="router",
    )(x, g.reshape(1, d), w_router_padded)


def _expert_kernel(be_ref, nused_ref, tok_cur_ref, tok_next_ref, h_hbm, wg_ref, wu_ref, wd_ref,
                   o_ref, xbuf, sems, h_scr, acc_scr, *, tm):
    i = pl.program_id(0)
    j = pl.program_id(1)
    last = pl.num_programs(1) - 1
    n_used = nused_ref[0]
    used = i < n_used

    def issue_gather(tok_ref, slot):
        def body(r, _):
            pltpu.make_async_copy(h_hbm.at[pl.ds(tok_ref[0, r], 1)],
                                  xbuf.at[slot, pl.ds(r, 1)], sems.at[slot]).start()
            return 0
        lax.fori_loop(0, tm, body, 0, unroll=8)

    @pl.when(jnp.logical_and(i == 0, j == 0))
    def _():
        issue_gather(tok_cur_ref, 0)

    @pl.when(jnp.logical_and(i + 1 < n_used, j == 0))
    def _():
        issue_gather(tok_next_ref, (i + 1) % 2)

    @pl.when(jnp.logical_and(used, j == 0))
    def _():
        slot = i % 2
        pltpu.make_async_copy(xbuf.at[slot], xbuf.at[slot], sems.at[slot]).wait()
        h_scr[...] = xbuf[slot].astype(BF16)
        acc_scr[...] = jnp.zeros_like(acc_scr)

    @pl.when(used)
    def _():
        h = h_scr[...]
        gate = jnp.dot(h, wg_ref[...], preferred_element_type=F32)
        up = jnp.dot(h, wu_ref[...], preferred_element_type=F32)
        act = (gate * _sigmoid(gate) * up).astype(BF16)
        acc_scr[...] += jnp.dot(act, wd_ref[...], preferred_element_type=F32)

    @pl.when(jnp.logical_and(used, j == last))
    def _():
        o_ref[...] = acc_scr[...]

    @pl.when(jnp.logical_and(jnp.logical_not(used), j == last))
    def _():
        o_ref[...] = jnp.zeros_like(o_ref)


def expert_ffn(h, row_tok, block_expert, n_used, w_gate, w_up, w_down, layer, *, tm, tf=1792):
    n_rows = row_tok.shape[0]
    d = h.shape[1]
    ff = w_gate.shape[3]
    n_blocks = n_rows // tm
    last_j = ff // tf - 1

    def up_block(i, j, be, nu):
        return (layer, be[jnp.minimum(i, nu[0] - 1)], 0, jnp.where(i < nu[0], j, last_j))

    def down_block(i, j, be, nu):
        return (layer, be[jnp.minimum(i, nu[0] - 1)], jnp.where(i < nu[0], j, last_j), 0)

    grid_spec = pltpu.PrefetchScalarGridSpec(
        num_scalar_prefetch=2,
        grid=(n_blocks, ff // tf),
        in_specs=[pl.BlockSpec((None, 1, tm), lambda i, j, be, nu: (i, 0, 0),
                               memory_space=pltpu.SMEM),
                  pl.BlockSpec((None, 1, tm),
                               lambda i, j, be, nu: (jnp.minimum(i + 1, n_blocks - 1), 0, 0),
                               memory_space=pltpu.SMEM),
                  pl.BlockSpec(memory_space=pl.ANY),
                  pl.BlockSpec((None, None, d, tf), up_block),
                  pl.BlockSpec((None, None, d, tf), up_block),
                  pl.BlockSpec((None, None, tf, d), down_block)],
        out_specs=pl.BlockSpec((tm, d), lambda i, j, be, nu: (i, 0)),
        scratch_shapes=[pltpu.VMEM((2, tm, d), F32), pltpu.SemaphoreType.DMA((2,)),
                        pltpu.VMEM((tm, d), BF16), pltpu.VMEM((tm, d), F32)])
    tok_blocks = row_tok.reshape(n_blocks, 1, tm)
    return pl.pallas_call(
        functools.partial(_expert_kernel, tm=tm),
        out_shape=jax.ShapeDtypeStruct((n_rows, d), F32),
        grid_spec=grid_spec,
        compiler_params=_params("arbitrary", "arbitrary"),
        name="expert_ffn",
    )(block_expert, n_used, tok_blocks, tok_blocks, h, w_gate, w_up, w_down)


def _combine_kernel(pos_cur_ref, pos_next_ref, x_ref, route_ref, y_ref, o_ref, buf, sems, *, rows):
    i = pl.program_id(0)
    n = pl.num_programs(0)

    def issue(pos_ref, slot):
        def body(r, _):
            for kk in range(TOP_K):
                pltpu.make_async_copy(y_ref.at[pl.ds(pos_ref[0, TOP_K * r + kk], 1)],
                                      buf.at[slot, kk, pl.ds(r, 1)], sems.at[slot]).start()
            return 0
        lax.fori_loop(0, rows, body, 0, unroll=4)

    @pl.when(i == 0)
    def _():
        issue(pos_cur_ref, 0)

    @pl.when(i + 1 < n)
    def _():
        issue(pos_next_ref, (i + 1) % 2)

    slot = i % 2
    pltpu.make_async_copy(buf.at[slot], buf.at[slot], sems.at[slot]).wait()
    route = route_ref[...]
    g1 = route[:, 2:3]
    g2 = route[:, 3:4]
    o_ref[...] = x_ref[...] + g1 * buf[slot, 0] + g2 * buf[slot, 1]


def combine_residual(x, route, y_rows, pos, *, rows=256):
    t, d = x.shape
    n_steps = t // rows
    grid_spec = pltpu.PrefetchScalarGridSpec(
        num_scalar_prefetch=0,
        grid=(n_steps,),
        in_specs=[pl.BlockSpec((None, 1, TOP_K * rows), lambda i: (i, 0, 0),
                               memory_space=pltpu.SMEM),
                  pl.BlockSpec((None, 1, TOP_K * rows),
                               lambda i: (jnp.minimum(i + 1, n_steps - 1), 0, 0),
                               memory_space=pltpu.SMEM),
                  pl.BlockSpec((rows, d), lambda i: (i, 0)),
                  pl.BlockSpec((rows, LANES), lambda i: (i, 0)),
                  pl.BlockSpec(memory_space=pl.ANY)],
        out_specs=pl.BlockSpec((rows, d), lambda i: (i, 0)),
        scratch_shapes=[pltpu.VMEM((2, TOP_K, rows, d), F32), pltpu.SemaphoreType.DMA((2,))])
    return pl.pallas_call(
        functools.partial(_combine_kernel, rows=rows),
        out_shape=jax.ShapeDtypeStruct((t, d), F32),
        grid_spec=grid_spec,
        compiler_params=_params("arbitrary"),
        name="combine_residual",
    )(pos.reshape(n_steps, 1, TOP_K * rows), pos.reshape(n_steps, 1, TOP_K * rows), x, route, y_rows)


def _final_norm_kernel(x_ref, g_ref, o_ref):
    o_ref[...] = _rms(x_ref[...], g_ref[...])


def final_norm(x, g, *, tm=1024):
    t, d = x.shape
    return pl.pallas_call(
        _final_norm_kernel,
        out_shape=jax.ShapeDtypeStruct((t, d), F32),
        grid=(t // tm,),
        in_specs=[pl.BlockSpec((tm, d), lambda i: (i, 0)), pl.BlockSpec((1, d), lambda i: (0, 0))],
        out_specs=pl.BlockSpec((tm, d), lambda i: (i, 0)),
        compiler_params=_params("parallel"),
        name="final_norm",
    )(x, g.reshape(1, d))


def _pad_cols(w, n):
    return jnp.pad(w, ((0, 0), (0, n - w.shape[1])))


def _scale_q_cols(w_qkv, width):
    scale = jnp.concatenate([jnp.full((width,), LOG2E * HEAD_DIM ** -0.5, F32),
                             jnp.ones((2 * width,), F32)])
    return w_qkv * scale


def even_layer(x, b, s, g_mix, g_ffn, w_in, b_forget, pool_w, pool_scale, w_out,
               w_gate, w_up, w_down, e, *, fox_tq=512):
    t, d = x.shape
    pool_width = pool_w.shape[0] * pool_w.shape[1]
    n_heads = b_forget.shape[0]
    fox_width = n_heads * HEAD_DIM
    w_a = w_in[:, :pool_width].astype(BF16)
    w_qkv = _scale_q_cols(w_in[:, pool_width:pool_width + 3 * fox_width], fox_width).astype(BF16)
    w_f = _pad_cols(w_in[:, pool_width + 3 * fox_width:], LANES).astype(BF16)
    qkv, a_in, f_pre = norm_proj(x, g_mix, [w_qkv, w_a, w_f], [BF16, F32, F32])

    pool_bd = jax.scipy.linalg.block_diag(*[pool_w[i] for i in range(pool_w.shape[0])]).astype(BF16)
    a_out, cum = pool_and_cum(a_in.reshape(b, s, pool_width), f_pre.reshape(b, s, LANES),
                              _pad_cols(b_forget.reshape(1, n_heads), LANES), pool_bd,
                              pool_scale.reshape(1, pool_width))
    cum_blocks = cum[:, :, :n_heads].reshape(b, s // fox_tq, fox_tq, n_heads // 2, 2)
    cum_blocks = cum_blocks.transpose(0, 3, 1, 4, 2)
    attn = fox_attention(qkv.reshape(b, s, 3 * fox_width), cum_blocks, tq=fox_tq)
    x = out_proj_residual(x, a_out.reshape(t, pool_width), attn.reshape(t, fox_width),
                          w_out.astype(BF16))
    return ffn_residual(x, g_ffn, w_gate, w_up, w_down, e)


def odd_layer(x, b, s, g_mix, g_ffn, w_in, b_igate, b_fgate, conv_w, head_norm_g, w_out,
              w_router, w_gate, w_up, w_down, e):
    x = odd_mixer_block(x, b, s, g_mix, w_in, b_igate, b_fgate, conv_w, head_norm_g, w_out)
    return moe_block(x, g_ffn, w_router, w_gate, w_up, w_down, e)


def odd_mixer_block(x, b, s, g_mix, w_in, b_igate, b_fgate, conv_w, head_norm_g, w_out):
    t, d = x.shape
    n_ml = b_igate.shape[0]
    ml_width = n_ml * LANES
    sb_width = d - ml_width
    cols = [3 * sb_width, 2 * ml_width, ml_width, ml_width]
    edges = [0]
    for c in cols:
        edges.append(edges[-1] + c)
    w_sb, w_mqk, w_mv, w_mo = [w_in[:, edges[i]:edges[i + 1]] for i in range(4)]
    w_sb = _scale_q_cols(w_sb, sb_width)
    w_sb, w_mqk, w_mv, w_mo = [w.astype(BF16) for w in (w_sb, w_mqk, w_mv, w_mo)]
    w_gates = _pad_cols(w_in[:, edges[-1]:], LANES).astype(BF16)
    sqkv, ml_qk, ml_v, ml_o, gates = norm_proj(
        x, g_mix, [w_sb, w_mqk, w_mv, w_mo, w_gates], [BF16, F32, BF16, F32, F32])

    c_out = sb_attention(sqkv.reshape(b, s, 3 * sb_width))

    k_scale = jnp.concatenate([jnp.ones((ml_width,), F32),
                               jnp.full((ml_width,), LANES ** -0.5, F32)]).reshape(1, 2 * ml_width)
    qk = conv_silu(ml_qk.reshape(b, s, 2 * ml_width), conv_w, k_scale)
    nc = s // MLSTM_CHUNK
    v3 = ml_v.reshape(b, s, ml_width)
    v_t = v3.reshape(b, nc, MLSTM_CHUNK, n_ml, LANES).transpose(0, 3, 1, 4, 2)
    gate_rows = gates[:, :2 * n_ml].reshape(b, nc, MLSTM_CHUNK, 2 * n_ml).transpose(0, 3, 1, 2)
    d_out = mlstm(qk, v3, v_t, gate_rows[:, :n_ml], gate_rows[:, n_ml:],
                  ml_o.reshape(b, s, ml_width), b_igate, b_fgate, head_norm_g)
    return out_proj_residual(x, c_out.reshape(t, sb_width), d_out.reshape(t, ml_width),
                             w_out.astype(BF16))


def moe_block(x, g_ffn, w_router, w_gate, w_up, w_down, e, *, expert_rows=512):
    t, d = x.shape
    h, route = router(x, g_ffn, _pad_cols(w_router, LANES))
    flat_e = route[:, :TOP_K].astype(jnp.int32).reshape(-1)
    n_assign = t * TOP_K
    onehot = jax.nn.one_hot(flat_e, N_EXPERTS, dtype=jnp.int32)
    rank = jnp.sum((jnp.cumsum(onehot, axis=0) - onehot) * onehot, axis=-1)
    counts = jnp.sum(onehot, axis=0)
    padded = (counts + expert_rows - 1) // expert_rows * expert_rows
    group_end = jnp.cumsum(padded)
    pos = (group_end[flat_e] - padded[flat_e] + rank).astype(jnp.int32)
    n_rows = n_assign + N_EXPERTS * expert_rows
    n_blocks = n_rows // expert_rows
    flat_tok = jnp.arange(n_assign, dtype=jnp.int32) // TOP_K
    row_tok = jnp.zeros((n_rows,), jnp.int32).at[pos].set(flat_tok)
    block_expert = jnp.minimum(
        jnp.searchsorted(group_end, jnp.arange(n_blocks) * expert_rows, side='right'),
        N_EXPERTS - 1).astype(jnp.int32)
    n_used = (group_end[-1:] // expert_rows).astype(jnp.int32)

    y_rows = expert_ffn(h, row_tok, block_expert, n_used, w_gate, w_up, w_down, e, tm=expert_rows)
    return combine_residual(x, route, y_rows, pos)


def kernel(x, norm_mix_g, norm_ffn_g, norm_final_g, ev_w_in, ev_b_forget, ev_pool_w, ev_pool_scale, ev_w_out, ffn_w_gate, ffn_w_up, ffn_w_down, od_w_in, od_b_igate, od_b_fgate, od_conv_w, od_head_norm_g, od_w_out, moe_w_router, moe_w_gate, moe_w_up, moe_w_down):
    b, s, d = x.shape
    depth = norm_mix_g.shape[0]
    xt = x.reshape(b * s, d)
    ffn_w = [w.astype(BF16) for w in (ffn_w_gate, ffn_w_up, ffn_w_down)]
    moe_w = [w.astype(BF16) for w in (moe_w_gate, moe_w_up, moe_w_down)]
    for layer in range(depth):
        e = layer // 2
        if layer % 2 == 0:
            xt = even_layer(xt, b, s, norm_mix_g[layer], norm_ffn_g[layer], ev_w_in[e],
                            ev_b_forget[e], ev_pool_w[e], ev_pool_scale[e], ev_w_out[e],
                            *ffn_w, e)
        else:
            xt = odd_layer(xt, b, s, norm_mix_g[layer], norm_ffn_g[layer], od_w_in[e],
                           od_b_igate[e], od_b_fgate[e], od_conv_w[e], od_head_norm_g[e],
                           od_w_out[e], moe_w_router[e], *moe_w, e)
    return final_norm(xt, norm_final_g).reshape(b, s, d)
```

```python
import functools

import jax
import jax.numpy as jnp
from jax import lax
from jax.experimental import pallas as pl
from jax.experimental.pallas import tpu as pltpu

F32 = jnp.float32
BF16 = jnp.bfloat16
EPS = 1e-6
NEG_INF = float("-inf")

LANES = 128
HEAD_DIM = 64
POOL_WINDOWS = (2, 4, 8, 16)
POOL_HALO = 16
CONV_WIDTH = 4
CONV_HALO = 8
FF_CHUNK = 512
MLSTM_BLOCK = 512
N_EXPERTS = 8
TOP_K = 2
VMEM_LIMIT = 56 * 1024 * 1024

LOG2E = 1.4426950408889634
SOFTPLUS2_CLAMP = 64.0
HIGHEST = lax.Precision.HIGHEST
NT_DIMS = (((1,), (1,)), ((), ()))


def _params(*semantics):
    return pltpu.CompilerParams(dimension_semantics=semantics, vmem_limit_bytes=VMEM_LIMIT)


def _rms(x, g):
    return x * lax.rsqrt(jnp.mean(x * x, axis=-1, keepdims=True) + EPS) * g


def _log_sigmoid(x):
    return jnp.minimum(x, 0.0) - jnp.log(1.0 + jnp.exp(-jnp.abs(x)))


def _sigmoid(x):
    return 1.0 / (1.0 + jnp.exp(-x))


def _norm_proj_kernel(x_ref, g_ref, *refs, n_out, col_chunk):
    w_refs, o_refs = refs[:n_out], refs[n_out:]
    h = _rms(x_ref[...], g_ref[...]).astype(BF16)
    for w_ref, o_ref in zip(w_refs, o_refs):
        n = w_ref.shape[1]
        for c0 in range(0, n, col_chunk):
            c1 = min(c0 + col_chunk, n)
            o_ref[:, c0:c1] = jnp.dot(
                h, w_ref[:, c0:c1], preferred_element_type=F32).astype(o_ref.dtype)


def norm_proj(x, g, weights, out_dtypes, *, tm=512, col_chunk=512):
    t, d = x.shape
    n_out = len(weights)
    return pl.pallas_call(
        functools.partial(_norm_proj_kernel, n_out=n_out, col_chunk=col_chunk),
        out_shape=[jax.ShapeDtypeStruct((t, w.shape[1]), dt) for w, dt in zip(weights, out_dtypes)],
        grid=(t // tm,),
        in_specs=[pl.BlockSpec((tm, d), lambda i: (i, 0)),
                  pl.BlockSpec((1, d), lambda i: (0, 0))]
                 + [pl.BlockSpec(w.shape, lambda i: (0, 0)) for w in weights],
        out_specs=[pl.BlockSpec((tm, w.shape[1]), lambda i: (i, 0)) for w in weights],
        compiler_params=_params("parallel"),
        name="norm_proj",
    )(x, g.reshape(1, d), *weights)


def _pool_kernel(a_ref, f_ref, bf_ref, w_ref, scale_ref, o_ref, cum_ref, *, rows):
    s, width = a_ref.shape
    group = lax.broadcasted_iota(jnp.int32, (1, width), 1) // (width // len(POOL_WINDOWS))
    window = jnp.where(group == 0, POOL_WINDOWS[0],
                       jnp.where(group == 1, POOL_WINDOWS[1],
                                 jnp.where(group == 2, POOL_WINDOWS[2], POOL_WINDOWS[3])))
    tri = (lax.broadcasted_iota(jnp.int32, (rows, rows), 0)
           >= lax.broadcasted_iota(jnp.int32, (rows, rows), 1)).astype(F32)
    carry = jnp.zeros((1, f_ref.shape[1]), F32)
    for r in range(s // rows):
        r0 = r * rows
        if r == 0:
            ext = jnp.concatenate([jnp.zeros((POOL_HALO, width), F32), a_ref[0:rows, :]], axis=0)
        else:
            ext = a_ref[r0 - POOL_HALO:r0 + rows, :]
        s2 = ext + pltpu.roll(ext, 1, 0)
        s4 = s2 + pltpu.roll(s2, 2, 0)
        s8 = s4 + pltpu.roll(s4, 4, 0)
        s16 = s8 + pltpu.roll(s8, 8, 0)
        wsum = jnp.where(group == 0, s2, jnp.where(group == 1, s4, jnp.where(group == 2, s8, s16)))
        wsum = wsum[POOL_HALO:, :]
        a = ext[POOL_HALO:, :]
        t1 = r0 + 1 + lax.broadcasted_iota(jnp.int32, (rows, 1), 0)
        count = jnp.minimum(t1, window).astype(F32)
        pooled = wsum / count - a
        mixed = jnp.dot(pooled.astype(BF16), w_ref[...], preferred_element_type=F32)
        o_ref[r0:r0 + rows, :] = (mixed * scale_ref[...]).astype(o_ref.dtype)
        lf = _log_sigmoid(f_ref[r0:r0 + rows, :] + bf_ref[...]) * LOG2E
        cum = jnp.dot(tri, lf, preferred_element_type=F32, precision=HIGHEST) + carry
        cum_ref[r0:r0 + rows, :] = cum
        carry = cum[rows - 1:rows, :]


def pool_and_cum(a_in, f_pre, b_forget, pool_w_bd, pool_scale, *, rows=256):
    b, s, width = a_in.shape
    fw = f_pre.shape[-1]
    return pl.pallas_call(
        functools.partial(_pool_kernel, rows=rows),
        out_shape=[jax.ShapeDtypeStruct((b, s, width), BF16),
                   jax.ShapeDtypeStruct((b, s, fw), F32)],
        grid=(b,),
        in_specs=[pl.BlockSpec((None, s, width), lambda i: (i, 0, 0)),
                  pl.BlockSpec((None, s, fw), lambda i: (i, 0, 0)),
                  pl.BlockSpec((1, fw), lambda i: (0, 0)),
                  pl.BlockSpec((width, width), lambda i: (0, 0)),
                  pl.BlockSpec((1, width), lambda i: (0, 0))],
        out_specs=[pl.BlockSpec((None, s, width), lambda i: (i, 0, 0)),
                   pl.BlockSpec((None, s, fw), lambda i: (i, 0, 0))],
        compiler_params=_params("parallel"),
        name="pool_and_cum",
    )(a_in, f_pre, b_forget, pool_w_bd, pool_scale)


def _fox_kernel(q_ref, k_ref, v_ref, cum_ref, o_ref, *, tq):
    i = pl.program_id(2)
    lane = lax.broadcasted_iota(jnp.int32, (1, LANES), 1)
    first = lane < HEAD_DIM
    q = q_ref[...]
    q_heads = (jnp.where(first, q, jnp.zeros_like(q)), jnp.where(first, jnp.zeros_like(q), q))
    row = lax.broadcasted_iota(jnp.int32, (tq, tq), 0)
    col = lax.broadcasted_iota(jnp.int32, (tq, tq), 1)

    def logits(j):
        start = pl.multiple_of(j * tq, tq)
        k = k_ref[pl.ds(start, tq), :]
        return tuple(lax.dot_general(q_heads[h], k, NT_DIMS, preferred_element_type=F32)
                     for h in range(2))

    def block(j, scs, carry, masked):
        ms, accs = carry
        start = pl.multiple_of(j * tq, tq)
        v = v_ref[pl.ds(start, tq), :]
        ones = jnp.ones_like(v)
        v_heads = (jnp.where(first, v, ones), jnp.where(first, ones, v))
        ck = cum_ref[j]
        new_ms, new_accs = [], []
        for h in range(2):
            sc = scs[h] - ck[h:h + 1, :]
            if masked:
                sc = jnp.where(col <= row, sc, NEG_INF)
            m_new = jnp.maximum(ms[h], jnp.max(sc, axis=-1, keepdims=True))
            alpha = jnp.exp2(ms[h] - m_new)
            p = jnp.exp2((sc - m_new).astype(BF16))
            new_ms.append(m_new)
            new_accs.append(alpha * accs[h] + jnp.dot(p, v_heads[h], preferred_element_type=F32))
        return tuple(new_ms), tuple(new_accs)

    init = ((jnp.full((tq, 1), NEG_INF, F32),) * 2, (jnp.zeros((tq, LANES), F32),) * 2)
    def pair(t, c):
        return block(2 * t + 1, logits(2 * t + 1), block(2 * t, logits(2 * t), c, False), False)

    carry = lax.fori_loop(0, i // 2, pair, init)
    carry = lax.fori_loop(2 * (i // 2), i, lambda j, c: block(j, logits(j), c, False), carry)
    _, accs = block(i, logits(i), carry, True)
    out0 = accs[0] / accs[0][:, HEAD_DIM:HEAD_DIM + 1]
    out1 = accs[1] / accs[1][:, 0:1]
    o_ref[...] = jnp.where(first, out0, out1).astype(o_ref.dtype)


def fox_attention(qkv, cum_blocks, *, tq=512):
    b, s, w3 = qkv.shape
    n_pairs = w3 // 3 // LANES
    return pl.pallas_call(
        functools.partial(_fox_kernel, tq=tq),
        out_shape=jax.ShapeDtypeStruct((b, s, w3 // 3), BF16),
        grid=(b, n_pairs, s // tq),
        in_specs=[pl.BlockSpec((None, tq, LANES), lambda bi, p, i: (bi, i, p)),
                  pl.BlockSpec((None, s, LANES), lambda bi, p, i: (bi, 0, n_pairs + p)),
                  pl.BlockSpec((None, s, LANES), lambda bi, p, i: (bi, 0, 2 * n_pairs + p)),
                  pl.BlockSpec((None, None, s // tq, 2, tq), lambda bi, p, i: (bi, p, 0, 0, 0))],
        out_specs=pl.BlockSpec((None, tq, LANES), lambda bi, p, i: (bi, i, p)),
        compiler_params=_params("parallel", "parallel", "arbitrary"),
        name="fox_attention",
    )(qkv, qkv, qkv, cum_blocks)


def _sb_kernel(q_ref, k_ref, v_ref, o_ref, *, tq, tk):
    i = pl.program_id(2)
    per = tq // tk
    lane = lax.broadcasted_iota(jnp.int32, (1, LANES), 1)
    first = lane < HEAD_DIM
    q = q_ref[...]
    q_heads = (jnp.where(first, q, jnp.zeros_like(q)), jnp.where(first, jnp.zeros_like(q), q))
    row = lax.broadcasted_iota(jnp.int32, (tq, tk), 0)
    col = lax.broadcasted_iota(jnp.int32, (tq, tk), 1)
    neg_suffix = jnp.where(lax.broadcasted_iota(jnp.int32, (tk, tk), 0)
                           > lax.broadcasted_iota(jnp.int32, (tk, tk), 1), -1.0, 0.0).astype(BF16)

    def logits(j):
        start = pl.multiple_of(j * tk, tk)
        k = k_ref[pl.ds(start, tk), :]
        return tuple(lax.dot_general(q_heads[h], k, NT_DIMS, preferred_element_type=F32)
                     for h in range(2))

    def block(j, zs, carry, diagonal):
        used, acc = carry
        start = pl.multiple_of(j * tk, tk)
        v = v_ref[pl.ds(start, tk), :]
        if diagonal:
            valid = col + (j * tk - i * tq) < row
        new_used, pvs = [], []
        for h in range(2):
            z = zs[h]
            sp = jnp.maximum(z, jnp.log2(1.0 + jnp.exp2(jnp.minimum(z, SOFTPLUS2_CLAMP))))
            if diagonal:
                sp = jnp.where(valid, sp, 0.0)
            later_in = jnp.dot(sp.astype(BF16), neg_suffix, preferred_element_type=F32)
            w = jnp.exp2((z - sp + (later_in - used[h])).astype(BF16))
            if diagonal:
                w = jnp.where(valid, w, jnp.zeros_like(w))
            pvs.append(jnp.dot(w, v, preferred_element_type=F32))
            new_used.append(used[h] + (sp[:, 0:1] - later_in[:, 0:1]))
        return tuple(new_used), acc + jnp.where(first, pvs[0], pvs[1])

    carry = ((jnp.zeros((tq, 1), F32),) * 2, jnp.zeros((tq, LANES), F32))
    for d in range(per):
        j = (i + 1) * per - 1 - d
        carry = block(j, logits(j), carry, True)

    def pair(t, carry):
        j = i * per - 1 - 2 * t
        return block(j - 1, logits(j - 1), block(j, logits(j), carry, False), False)

    assert per % 2 == 0
    _, acc = lax.fori_loop(0, i * per // 2, pair, carry)
    o_ref[...] = acc.astype(o_ref.dtype)


def sb_attention(qkv, *, tq=512, tk=256):
    b, s, w3 = qkv.shape
    n_pairs = w3 // 3 // LANES
    return pl.pallas_call(
        functools.partial(_sb_kernel, tq=tq, tk=tk),
        out_shape=jax.ShapeDtypeStruct((b, s, w3 // 3), BF16),
        grid=(b, n_pairs, s // tq),
        in_specs=[pl.BlockSpec((None, tq, LANES), lambda bi, p, i: (bi, i, p)),
                  pl.BlockSpec((None, s, LANES), lambda bi, p, i: (bi, 0, n_pairs + p)),
                  pl.BlockSpec((None, s, LANES), lambda bi, p, i: (bi, 0, 2 * n_pairs + p))],
        out_specs=pl.BlockSpec((None, tq, LANES), lambda bi, p, i: (bi, i, p)),
        compiler_params=_params("parallel", "parallel", "arbitrary"),
        name="sb_attention",
    )(qkv, qkv, qkv)


def _conv_kernel(x_ref, halo_ref, w_ref, colscale_ref, o_ref):
    i = pl.program_id(1)
    halo = jnp.where(i == 0, 0.0, halo_ref[...])
    ext = jnp.concatenate([halo, x_ref[...]], axis=0)
    y = ext * w_ref[CONV_WIDTH - 1:CONV_WIDTH, :]
    for back in range(1, CONV_WIDTH):
        y = y + pltpu.roll(ext, back, 0) * w_ref[CONV_WIDTH - 1 - back:CONV_WIDTH - back, :]
    y = y[CONV_HALO:, :]
    o_ref[...] = (y * _sigmoid(y) * colscale_ref[...]).astype(o_ref.dtype)


def conv_silu(x, conv_w, colscale, *, rows=512):
    b, s, c = x.shape
    per = rows // CONV_HALO
    return pl.pallas_call(
        _conv_kernel,
        out_shape=jax.ShapeDtypeStruct((b, s, c), BF16),
        grid=(b, s // rows),
        in_specs=[pl.BlockSpec((None, rows, c), lambda bi, i: (bi, i, 0)),
                  pl.BlockSpec((None, CONV_HALO, c),
                               lambda bi, i: (bi, jnp.maximum(i * per - 1, 0), 0)),
                  pl.BlockSpec((CONV_WIDTH, c), lambda bi, i: (0, 0)),
                  pl.BlockSpec((1, c), lambda bi, i: (0, 0))],
        out_specs=pl.BlockSpec((None, rows, c), lambda bi, i: (bi, i, 0)),
        compiler_params=_params("parallel", "parallel"),
        name="conv_silu",
    )(x, x, conv_w, colscale)


def _mlstm_kernel(bi_ref, bf_ref, q_ref, k_ref, v_ref, vt_ref, li_ref, lf_ref, o_ref, g_ref,
                  out_ref, c_scr, n_scr, m_scr):
    head = pl.program_id(1)
    step = pl.program_id(2)
    L = q_ref.shape[0]

    @pl.when(step == 0)
    def _():
        c_scr[...] = jnp.zeros_like(c_scr)
        n_scr[...] = jnp.zeros_like(n_scr)
        m_scr[...] = jnp.zeros_like(m_scr)

    r_idx = lax.broadcasted_iota(jnp.int32, (L, L), 0)
    c_idx = lax.broadcasted_iota(jnp.int32, (L, L), 1)
    lower = r_idx >= c_idx
    tri_u = (r_idx <= c_idx).astype(F32)

    q = q_ref[...]
    k = k_ref[...]
    c_st = c_scr[...]
    n_st = n_scr[...]
    m_st = m_scr[...]
    lf_row = _log_sigmoid(lf_ref[...] + bf_ref[head])
    li_row = li_ref[...] + bi_ref[head]
    bcum_row = jnp.dot(jnp.broadcast_to(lf_row, (8, L)), tri_u, preferred_element_type=F32,
                       precision=HIGHEST)[0:1, :]
    bcum_col = jnp.broadcast_to(bcum_row, (LANES, L)).T[:, 0:1]
    g = jnp.sum(lf_row, axis=-1, keepdims=True)

    log_d = jnp.where(lower, bcum_col - bcum_row + li_row, NEG_INF)
    inter = bcum_col + m_st
    m_out = jnp.maximum(inter, jnp.max(log_d, axis=-1, keepdims=True))
    d_mat = jnp.exp(log_d - m_out)
    inter_w = jnp.exp(inter - m_out)
    qk = lax.dot_general(q, k, NT_DIMS, preferred_element_type=F32) * d_mat
    num = (jnp.dot(qk.astype(BF16), v_ref[...], preferred_element_type=F32)
           + inter_w * lax.dot_general(q, c_st.astype(BF16), NT_DIMS, preferred_element_type=F32))
    den = (jnp.sum(qk, axis=-1, keepdims=True)
           + inter_w * jnp.sum(q.astype(F32) * n_st, axis=-1, keepdims=True))
    hh = num / jnp.maximum(jnp.abs(den), jnp.exp(-m_out))
    hn = _rms(hh, g_ref[...])
    out_ref[...] = (hn * _sigmoid(o_ref[...])).astype(out_ref.dtype)

    a_row = g - bcum_row + li_row
    m_loc = jnp.max(a_row, axis=-1, keepdims=True)
    w_row = jnp.exp(a_row - m_loc)
    c_loc = jnp.dot((vt_ref[...].astype(F32) * w_row).astype(BF16), k, preferred_element_type=F32)
    n_loc = jnp.dot(jnp.broadcast_to(w_row, (8, L)).astype(BF16), k,
                    preferred_element_type=F32)[0:1, :]
    m_new = jnp.maximum(g + m_st, m_loc)
    decay = jnp.exp(g + m_st - m_new)
    fresh = jnp.exp(m_loc - m_new)
    c_scr[...] = decay * c_st + fresh * c_loc
    n_scr[...] = decay * n_st + fresh * n_loc
    m_scr[...] = m_new


def mlstm(qk, v, v_t, li, lf, o_gate, b_i, b_f, head_g):
    b, s, w = v.shape
    n_heads = w // LANES
    rows = v_t.shape[-1]
    grid_spec = pltpu.PrefetchScalarGridSpec(
        num_scalar_prefetch=2,
        grid=(b, n_heads, s // rows),
        in_specs=[pl.BlockSpec((None, rows, LANES), lambda bi, h, c, *_: (bi, c, h)),
                  pl.BlockSpec((None, rows, LANES), lambda bi, h, c, *_: (bi, c, n_heads + h)),
                  pl.BlockSpec((None, rows, LANES), lambda bi, h, c, *_: (bi, c, h)),
                  pl.BlockSpec((None, None, None, LANES, rows),
                               lambda bi, h, c, *_: (bi, h, c, 0, 0)),
                  pl.BlockSpec((None, None, None, 1, rows), lambda bi, h, c, *_: (bi, h, c, 0, 0)),
                  pl.BlockSpec((None, None, None, 1, rows), lambda bi, h, c, *_: (bi, h, c, 0, 0)),
                  pl.BlockSpec((None, rows, LANES), lambda bi, h, c, *_: (bi, c, h)),
                  pl.BlockSpec((None, 1, LANES), lambda bi, h, c, *_: (h, 0, 0))],
        out_specs=pl.BlockSpec((None, rows, LANES), lambda bi, h, c, *_: (bi, c, h)),
        scratch_shapes=[pltpu.VMEM((LANES, LANES), F32), pltpu.VMEM((1, LANES), F32),
                        pltpu.VMEM((1, 1), F32)])
    return pl.pallas_call(
        _mlstm_kernel,
        out_shape=jax.ShapeDtypeStruct((b, s, w), BF16),
        grid_spec=grid_spec,
        compiler_params=_params("parallel", "parallel", "arbitrary"),
        name="mlstm",
    )(b_i, b_f, qk, qk, v, v_t, li, lf, o_gate, head_g.reshape(n_heads, 1, LANES))


def _out_proj_kernel(x_ref, a_ref, b_ref, w_ref, o_ref):
    na = a_ref.shape[1]
    acc = jnp.dot(a_ref[...], w_ref[0:na, :], preferred_element_type=F32)
    acc = acc + jnp.dot(b_ref[...], w_ref[na:, :], preferred_element_type=F32)
    o_ref[...] = x_ref[...] + acc


def out_proj_residual(x, a, b_part, w, *, tm=512):
    t, d = x.shape
    return pl.pallas_call(
        _out_proj_kernel,
        out_shape=jax.ShapeDtypeStruct((t, d), F32),
        grid=(t // tm,),
        in_specs=[pl.BlockSpec((tm, d), lambda i: (i, 0)),
                  pl.BlockSpec((tm, a.shape[1]), lambda i: (i, 0)),
                  pl.BlockSpec((tm, b_part.shape[1]), lambda i: (i, 0)),
                  pl.BlockSpec(w.shape, lambda i: (0, 0))],
        out_specs=pl.BlockSpec((tm, d), lambda i: (i, 0)),
        compiler_params=_params("parallel"),
        name="out_proj_residual",
    )(x, a, b_part, w)


def _swiglu_accumulate(h_scr, wg_ref, wu_ref, wd_ref, act_scr, o_ref):
    h = h_scr[...]
    tf = wg_ref.shape[1]
    for c0 in range(0, tf, FF_CHUNK):
        c1 = min(c0 + FF_CHUNK, tf)
        gate = jnp.dot(h, wg_ref[:, c0:c1], preferred_element_type=F32)
        up = jnp.dot(h, wu_ref[:, c0:c1], preferred_element_type=F32)
        act_scr[:, c0:c1] = (gate * _sigmoid(gate) * up).astype(BF16)
    o_ref[...] += jnp.dot(act_scr[...], wd_ref[...], preferred_element_type=F32)


def _ffn_kernel(x_ref, g_ref, wg_ref, wu_ref, wd_ref, o_ref, h_scr, act_scr):
    @pl.when(pl.program_id(1) == 0)
    def _():
        x = x_ref[...]
        h_scr[...] = _rms(x, g_ref[...]).astype(BF16)
        o_ref[...] = x

    _swiglu_accumulate(h_scr, wg_ref, wu_ref, wd_ref, act_scr, o_ref)


def ffn_residual(x, g, w_gate, w_up, w_down, layer, *, tm=1024, tf=1792):
    t, d = x.shape
    ff = w_gate.shape[2]
    return pl.pallas_call(
        _ffn_kernel,
        out_shape=jax.ShapeDtypeStruct((t, d), F32),
        grid=(t // tm, ff // tf),
        in_specs=[pl.BlockSpec((tm, d), lambda i, j: (i, 0)),
                  pl.BlockSpec((1, d), lambda i, j: (0, 0)),
                  pl.BlockSpec((None, d, tf), lambda i, j: (layer, 0, j)),
                  pl.BlockSpec((None, d, tf), lambda i, j: (layer, 0, j)),
                  pl.BlockSpec((None, tf, d), lambda i, j: (layer, j, 0))],
        out_specs=pl.BlockSpec((tm, d), lambda i, j: (i, 0)),
        scratch_shapes=[pltpu.VMEM((tm, d), BF16), pltpu.VMEM((tm, tf), BF16)],
        compiler_params=_params("parallel", "arbitrary"),
        name="ffn_residual",
    )(x, g.reshape(1, d), w_gate, w_up, w_down)


def _router_kernel(x_ref, g_ref, wr_ref, h_ref, route_ref):
    h = _rms(x_ref[...], g_ref[...])
    h_ref[...] = h
    logits = jnp.dot(h, wr_ref[...], preferred_element_type=F32, precision=HIGHEST)
    lane = lax.broadcasted_iota(jnp.int32, logits.shape, 1)
    logits = jnp.where(lane < N_EXPERTS, logits, NEG_INF)
    m1 = jnp.max(logits, axis=-1, keepdims=True)
    i1 = jnp.min(jnp.where(logits == m1, lane, LANES), axis=-1, keepdims=True)
    rest = jnp.where(lane == i1, NEG_INF, logits)
    m2 = jnp.max(rest, axis=-1, keepdims=True)
    i2 = jnp.min(jnp.where(rest == m2, lane, LANES), axis=-1, keepdims=True)
    e2 = jnp.exp(m2 - m1)
    g1 = 1.0 / (1.0 + e2)
    g2 = e2 / (1.0 + e2)
    route_ref[...] = jnp.where(lane == 0, i1.astype(F32),
                               jnp.where(lane == 1, i2.astype(F32),
                                         jnp.where(lane == 2, g1, jnp.where(lane == 3, g2, 0.0))))


def router(x, g, w_router_padded, *, tm=512):
    t, d = x.shape
    return pl.pallas_call(
        _router_kernel,
        out_shape=[jax.ShapeDtypeStruct((t, d), F32), jax.ShapeDtypeStruct((t, LANES), F32)],
        grid=(t // tm,),
        in_specs=[pl.BlockSpec((tm, d), lambda i: (i, 0)),
                  pl.BlockSpec((1, d), lambda i: (0, 0)),
                  pl.BlockSpec((d, LANES), lambda i: (0, 0))],
        out_specs=[pl.BlockSpec((tm, d), lambda i: (i, 0)),
                   pl.BlockSpec((tm, LANES), lambda i: (i, 0))],
        compiler_params=_params("parallel"),
        name="router",
    )(x, g.reshape(1, d), w_router_padded)


def _expert_kernel(be_ref, nused_ref, tok_cur_ref, tok_next_ref, h_hbm, wg_ref, wu_ref, wd_ref,
                   o_ref, xbuf, sems, h_scr, act_scr, *, tm):
    i = pl.program_id(0)
    j = pl.program_id(1)
    n_used = nused_ref[0]
    used = i < n_used

    def issue_gather(tok_ref, slot):
        def body(r, _):
            pltpu.make_async_copy(h_hbm.at[pl.ds(tok_ref[0, r], 1)],
                                  xbuf.at[slot, pl.ds(r, 1)], sems.at[slot]).start()
            return 0
        lax.fori_loop(0, tm, body, 0, unroll=8)

    @pl.when(jnp.logical_and(i == 0, j == 0))
    def _():
        issue_gather(tok_cur_ref, 0)

    @pl.when(jnp.logical_and(i + 1 < n_used, j == 0))
    def _():
        issue_gather(tok_next_ref, (i + 1) % 2)

    @pl.when(jnp.logical_and(used, j == 0))
    def _():
        slot = i % 2
        pltpu.make_async_copy(xbuf.at[slot], xbuf.at[slot], sems.at[slot]).wait()
        h_scr[...] = xbuf[slot].astype(BF16)

    @pl.when(j == 0)
    def _():
        o_ref[...] = jnp.zeros_like(o_ref)

    @pl.when(used)
    def _():
        _swiglu_accumulate(h_scr, wg_ref, wu_ref, wd_ref, act_scr, o_ref)


def expert_ffn(h, row_tok, block_expert, n_used, w_gate, w_up, w_down, layer, *, tm, tf=1792):
    n_rows = row_tok.shape[0]
    d = h.shape[1]
    ff = w_gate.shape[3]
    n_blocks = n_rows // tm
    last_j = ff // tf - 1

    def up_block(i, j, be, nu):
        return (layer, be[jnp.minimum(i, nu[0] - 1)], 0, jnp.where(i < nu[0], j, last_j))

    def down_block(i, j, be, nu):
        return (layer, be[jnp.minimum(i, nu[0] - 1)], jnp.where(i < nu[0], j, last_j), 0)

    grid_spec = pltpu.PrefetchScalarGridSpec(
        num_scalar_prefetch=2,
        grid=(n_blocks, ff // tf),
        in_specs=[pl.BlockSpec((None, 1, tm), lambda i, j, be, nu: (i, 0, 0),
                               memory_space=pltpu.SMEM),
                  pl.BlockSpec((None, 1, tm),
                               lambda i, j, be, nu: (jnp.minimum(i + 1, n_blocks - 1), 0, 0),
                               memory_space=pltpu.SMEM),
                  pl.BlockSpec(memory_space=pl.ANY),
                  pl.BlockSpec((None, None, d, tf), up_block),
                  pl.BlockSpec((None, None, d, tf), up_block),
                  pl.BlockSpec((None, None, tf, d), down_block)],
        out_specs=pl.BlockSpec((tm, d), lambda i, j, be, nu: (i, 0)),
        scratch_shapes=[pltpu.VMEM((2, tm, d), F32), pltpu.SemaphoreType.DMA((2,)),
                        pltpu.VMEM((tm, d), BF16), pltpu.VMEM((tm, tf), BF16)])
    tok_blocks = row_tok.reshape(n_blocks, 1, tm)
    return pl.pallas_call(
        functools.partial(_expert_kernel, tm=tm),
        out_shape=jax.ShapeDtypeStruct((n_rows, d), F32),
        grid_spec=grid_spec,
        compiler_params=_params("arbitrary", "arbitrary"),
        name="expert_ffn",
    )(block_expert, n_used, tok_blocks, tok_blocks, h, w_gate, w_up, w_down)


def _combine_kernel(pos_cur_ref, pos_next_ref, x_ref, route_ref, y_ref, o_ref, buf, sems, *, rows):
    i = pl.program_id(0)
    n = pl.num_programs(0)

    def issue(pos_ref, slot):
        def body(r, _):
            for kk in range(TOP_K):
                pltpu.make_async_copy(y_ref.at[pl.ds(pos_ref[0, TOP_K * r + kk], 1)],
                                      buf.at[slot, kk, pl.ds(r, 1)], sems.at[slot]).start()
            return 0
        lax.fori_loop(0, rows, body, 0, unroll=4)

    @pl.when(i == 0)
    def _():
        issue(pos_cur_ref, 0)

    @pl.when(i + 1 < n)
    def _():
        issue(pos_next_ref, (i + 1) % 2)

    slot = i % 2
    pltpu.make_async_copy(buf.at[slot], buf.at[slot], sems.at[slot]).wait()
    route = route_ref[...]
    g1 = route[:, 2:3]
    g2 = route[:, 3:4]
    o_ref[...] = x_ref[...] + g1 * buf[slot, 0] + g2 * buf[slot, 1]


def combine_residual(x, route, y_rows, pos, *, rows=256):
    t, d = x.shape
    n_steps = t // rows
    grid_spec = pltpu.PrefetchScalarGridSpec(
        num_scalar_prefetch=0,
        grid=(n_steps,),
        in_specs=[pl.BlockSpec((None, 1, TOP_K * rows), lambda i: (i, 0, 0),
                               memory_space=pltpu.SMEM),
                  pl.BlockSpec((None, 1, TOP_K * rows),
                               lambda i: (jnp.minimum(i + 1, n_steps - 1), 0, 0),
                               memory_space=pltpu.SMEM),
                  pl.BlockSpec((rows, d), lambda i: (i, 0)),
                  pl.BlockSpec((rows, LANES), lambda i: (i, 0)),
                  pl.BlockSpec(memory_space=pl.ANY)],
        out_specs=pl.BlockSpec((rows, d), lambda i: (i, 0)),
        scratch_shapes=[pltpu.VMEM((2, TOP_K, rows, d), F32), pltpu.SemaphoreType.DMA((2,))])
    return pl.pallas_call(
        functools.partial(_combine_kernel, rows=rows),
        out_shape=jax.ShapeDtypeStruct((t, d), F32),
        grid_spec=grid_spec,
        compiler_params=_params("arbitrary"),
        name="combine_residual",
    )(pos.reshape(n_steps, 1, TOP_K * rows), pos.reshape(n_steps, 1, TOP_K * rows), x, route, y_rows)


def _final_norm_kernel(x_ref, g_ref, o_ref):
    o_ref[...] = _rms(x_ref[...], g_ref[...])


def final_norm(x, g, *, tm=1024):
    t, d = x.shape
    return pl.pallas_call(
        _final_norm_kernel,
        out_shape=jax.ShapeDtypeStruct((t, d), F32),
        grid=(t // tm,),
        in_specs=[pl.BlockSpec((tm, d), lambda i: (i, 0)), pl.BlockSpec((1, d), lambda i: (0, 0))],
        out_specs=pl.BlockSpec((tm, d), lambda i: (i, 0)),
        compiler_params=_params("parallel"),
        name="final_norm",
    )(x, g.reshape(1, d))


def _pad_cols(w, n):
    return jnp.pad(w, ((0, 0), (0, n - w.shape[1])))


def _scale_q_cols(w_qkv, width):
    scale = jnp.concatenate([jnp.full((width,), LOG2E * HEAD_DIM ** -0.5, F32),
                             jnp.ones((2 * width,), F32)])
    return w_qkv * scale


def even_layer(x, b, s, g_mix, g_ffn, w_in, b_forget, pool_w, pool_scale, w_out,
               w_gate, w_up, w_down, e, *, fox_tq=512):
    t, d = x.shape
    pool_width = pool_w.shape[0] * pool_w.shape[1]
    n_heads = b_forget.shape[0]
    fox_width = n_heads * HEAD_DIM
    w_a = w_in[:, :pool_width].astype(BF16)
    w_qkv = _scale_q_cols(w_in[:, pool_width:pool_width + 3 * fox_width], fox_width).astype(BF16)
    w_f = _pad_cols(w_in[:, pool_width + 3 * fox_width:], LANES).astype(BF16)
    qkv, a_in, f_pre = norm_proj(x, g_mix, [w_qkv, w_a, w_f], [BF16, F32, F32])

    pool_bd = jax.scipy.linalg.block_diag(*[pool_w[i] for i in range(pool_w.shape[0])]).astype(BF16)
    a_out, cum = pool_and_cum(a_in.reshape(b, s, pool_width), f_pre.reshape(b, s, LANES),
                              _pad_cols(b_forget.reshape(1, n_heads), LANES), pool_bd,
                              pool_scale.reshape(1, pool_width))
    cum_blocks = cum[:, :, :n_heads].reshape(b, s // fox_tq, fox_tq, n_heads // 2, 2)
    cum_blocks = cum_blocks.transpose(0, 3, 1, 4, 2)
    attn = fox_attention(qkv.reshape(b, s, 3 * fox_width), cum_blocks, tq=fox_tq)
    x = out_proj_residual(x, a_out.reshape(t, pool_width), attn.reshape(t, fox_width),
                          w_out.astype(BF16))
    return ffn_residual(x, g_ffn, w_gate, w_up, w_down, e)


def odd_layer(x, b, s, g_mix, g_ffn, w_in, b_igate, b_fgate, conv_w, head_norm_g, w_out,
              w_router, w_gate, w_up, w_down, e):
    x = odd_mixer_block(x, b, s, g_mix, w_in, b_igate, b_fgate, conv_w, head_norm_g, w_out)
    return moe_block(x, g_ffn, w_router, w_gate, w_up, w_down, e)


def odd_mixer_block(x, b, s, g_mix, w_in, b_igate, b_fgate, conv_w, head_norm_g, w_out):
    t, d = x.shape
    n_ml = b_igate.shape[0]
    ml_width = n_ml * LANES
    sb_width = d - ml_width
    cols = [3 * sb_width, 2 * ml_width, ml_width, ml_width]
    edges = [0]
    for c in cols:
        edges.append(edges[-1] + c)
    w_sb, w_mqk, w_mv, w_mo = [w_in[:, edges[i]:edges[i + 1]] for i in range(4)]
    w_sb = _scale_q_cols(w_sb, sb_width)
    w_sb, w_mqk, w_mv, w_mo = [w.astype(BF16) for w in (w_sb, w_mqk, w_mv, w_mo)]
    w_gates = _pad_cols(w_in[:, edges[-1]:], LANES).astype(BF16)
    sqkv, ml_qk, ml_v, ml_o, gates = norm_proj(
        x, g_mix, [w_sb, w_mqk, w_mv, w_mo, w_gates], [BF16, F32, BF16, F32, F32])

    c_out = sb_attention(sqkv.reshape(b, s, 3 * sb_width))

    k_scale = jnp.concatenate([jnp.ones((ml_width,), F32),
                               jnp.full((ml_width,), LANES ** -0.5, F32)]).reshape(1, 2 * ml_width)
    qk = conv_silu(ml_qk.reshape(b, s, 2 * ml_width), conv_w, k_scale)
    nb = s // MLSTM_BLOCK
    v3 = ml_v.reshape(b, s, ml_width)
    v_t = v3.reshape(b, nb, MLSTM_BLOCK, n_ml, LANES).transpose(0, 3, 1, 4, 2)
    gate_rows = gates[:, :2 * n_ml].reshape(b, nb, MLSTM_BLOCK, 2 * n_ml).transpose(0, 3, 1, 2)
    gate_rows = gate_rows.reshape(b, 2 * n_ml, nb, 1, MLSTM_BLOCK)
    d_out = mlstm(qk, v3, v_t, gate_rows[:, :n_ml], gate_rows[:, n_ml:],
                  ml_o.reshape(b, s, ml_width), b_igate, b_fgate, head_norm_g)
    return out_proj_residual(x, c_out.reshape(t, sb_width), d_out.reshape(t, ml_width),
                             w_out.astype(BF16))


def moe_block(x, g_ffn, w_router, w_gate, w_up, w_down, e, *, expert_rows=1024):
    t, d = x.shape
    h, route = router(x, g_ffn, _pad_cols(w_router, LANES))
    flat_e = route[:, :TOP_K].astype(jnp.int32).reshape(-1)
    n_assign = t * TOP_K
    onehot = jax.nn.one_hot(flat_e, N_EXPERTS, dtype=jnp.int32)
    rank = jnp.sum((jnp.cumsum(onehot, axis=0) - onehot) * onehot, axis=-1)
    counts = jnp.sum(onehot, axis=0)
    padded = (counts + expert_rows - 1) // expert_rows * expert_rows
    group_end = jnp.cumsum(padded)
    pos = (group_end[flat_e] - padded[flat_e] + rank).astype(jnp.int32)
    n_rows = n_assign + N_EXPERTS * expert_rows
    n_blocks = n_rows // expert_rows
    flat_tok = jnp.arange(n_assign, dtype=jnp.int32) // TOP_K
    row_tok = jnp.zeros((n_rows,), jnp.int32).at[pos].set(flat_tok)
    block_expert = jnp.minimum(
        jnp.searchsorted(group_end, jnp.arange(n_blocks) * expert_rows, side='right'),
        N_EXPERTS - 1).astype(jnp.int32)
    n_used = (group_end[-1:] // expert_rows).astype(jnp.int32)

    y_rows = expert_ffn(h, row_tok, block_expert, n_used, w_gate, w_up, w_down, e, tm=expert_rows)
    return combine_residual(x, route, y_rows, pos)


def kernel(x, norm_mix_g, norm_ffn_g, norm_final_g, ev_w_in, ev_b_forget, ev_pool_w, ev_pool_scale, ev_w_out, ffn_w_gate, ffn_w_up, ffn_w_down, od_w_in, od_b_igate, od_b_fgate, od_conv_w, od_head_norm_g, od_w_out, moe_w_router, moe_w_gate, moe_w_up, moe_w_down):
    b, s, d = x.shape
    depth = norm_mix_g.shape[0]
    xt = x.reshape(b * s, d)
    ffn_w = [w.astype(BF16) for w in (ffn_w_gate, ffn_w_up, ffn_w_down)]
    moe_w = [w.astype(BF16) for w in (moe_w_gate, moe_w_up, moe_w_down)]
    for layer in range(depth):
        e = layer // 2
        if layer % 2 == 0:
            xt = even_layer(xt, b, s, norm_mix_g[layer], norm_ffn_g[layer], ev_w_in[e],
                            ev_b_forget[e], ev_pool_w[e], ev_pool_scale[e], ev_w_out[e],
                            *ffn_w, e)
        else:
            xt = odd_layer(xt, b, s, norm_mix_g[layer], norm_ffn_g[layer], od_w_in[e],
                           od_b_igate[e], od_b_fgate[e], od_conv_w[e], od_head_norm_g[e],
                           od_w_out[e], moe_w_router[e], *moe_w, e)
    return final_norm(xt, norm_final_g).reshape(b, s, d)
```

```python
import functools

import jax
import jax.numpy as jnp
from jax import lax
from jax.experimental import pallas as pl
from jax.experimental.pallas import tpu as pltpu

F32 = jnp.float32
BF16 = jnp.bfloat16
EPS = 1e-6
NEG_INF = float("-inf")

LANES = 128
HEAD_DIM = 64
POOL_WINDOWS = (2, 4, 8, 16)
POOL_HALO = 16
CONV_WIDTH = 4
CONV_HALO = 8
FF_CHUNK = 512
MLSTM_BLOCK = 512
N_EXPERTS = 8
TOP_K = 2
VMEM_LIMIT = 56 * 1024 * 1024

LOG2E = 1.4426950408889634
SOFTPLUS2_CLAMP = 64.0
HIGHEST = lax.Precision.HIGHEST
NT_DIMS = (((1,), (1,)), ((), ()))


def _params(*semantics):
    return pltpu.CompilerParams(dimension_semantics=semantics, vmem_limit_bytes=VMEM_LIMIT)


def _rms(x, g):
    return x * lax.rsqrt(jnp.mean(x * x, axis=-1, keepdims=True) + EPS) * g


def _log_sigmoid(x):
    return jnp.minimum(x, 0.0) - jnp.log(1.0 + jnp.exp(-jnp.abs(x)))


def _sigmoid(x):
    return 1.0 / (1.0 + jnp.exp(-x))


def _norm_proj_kernel(x_ref, g_ref, *refs, n_out, col_chunk):
    w_refs, o_refs = refs[:n_out], refs[n_out:]
    h = _rms(x_ref[...], g_ref[...]).astype(BF16)
    for w_ref, o_ref in zip(w_refs, o_refs):
        n = w_ref.shape[1]
        for c0 in range(0, n, col_chunk):
            c1 = min(c0 + col_chunk, n)
            o_ref[:, c0:c1] = jnp.dot(
                h, w_ref[:, c0:c1], preferred_element_type=F32).astype(o_ref.dtype)


def norm_proj(x, g, weights, out_dtypes, *, tm=512, col_chunk=512):
    t, d = x.shape
    n_out = len(weights)
    return pl.pallas_call(
        functools.partial(_norm_proj_kernel, n_out=n_out, col_chunk=col_chunk),
        out_shape=[jax.ShapeDtypeStruct((t, w.shape[1]), dt) for w, dt in zip(weights, out_dtypes)],
        grid=(t // tm,),
        in_specs=[pl.BlockSpec((tm, d), lambda i: (i, 0)),
                  pl.BlockSpec((1, d), lambda i: (0, 0))]
                 + [pl.BlockSpec(w.shape, lambda i: (0, 0)) for w in weights],
        out_specs=[pl.BlockSpec((tm, w.shape[1]), lambda i: (i, 0)) for w in weights],
        compiler_params=_params("parallel"),
        name="norm_proj",
    )(x, g.reshape(1, d), *weights)


def _pool_kernel(a_ref, f_ref, bf_ref, w_ref, scale_ref, o_ref, cum_ref, *, rows):
    s, width = a_ref.shape
    group = lax.broadcasted_iota(jnp.int32, (1, width), 1) // (width // len(POOL_WINDOWS))
    window = jnp.where(group == 0, POOL_WINDOWS[0],
                       jnp.where(group == 1, POOL_WINDOWS[1],
                                 jnp.where(group == 2, POOL_WINDOWS[2], POOL_WINDOWS[3])))
    tri = (lax.broadcasted_iota(jnp.int32, (rows, rows), 0)
           >= lax.broadcasted_iota(jnp.int32, (rows, rows), 1)).astype(F32)
    carry = jnp.zeros((1, f_ref.shape[1]), F32)
    for r in range(s // rows):
        r0 = r * rows
        if r == 0:
            ext = jnp.concatenate([jnp.zeros((POOL_HALO, width), F32), a_ref[0:rows, :]], axis=0)
        else:
            ext = a_ref[r0 - POOL_HALO:r0 + rows, :]
        s2 = ext + pltpu.roll(ext, 1, 0)
        s4 = s2 + pltpu.roll(s2, 2, 0)
        s8 = s4 + pltpu.roll(s4, 4, 0)
        s16 = s8 + pltpu.roll(s8, 8, 0)
        wsum = jnp.where(group == 0, s2, jnp.where(group == 1, s4, jnp.where(group == 2, s8, s16)))
        wsum = wsum[POOL_HALO:, :]
        a = ext[POOL_HALO:, :]
        t1 = r0 + 1 + lax.broadcasted_iota(jnp.int32, (rows, 1), 0)
        count = jnp.minimum(t1, window).astype(F32)
        pooled = wsum / count - a
        mixed = jnp.dot(pooled.astype(BF16), w_ref[...], preferred_element_type=F32)
        o_ref[r0:r0 + rows, :] = (mixed * scale_ref[...]).astype(o_ref.dtype)
        lf = _log_sigmoid(f_ref[r0:r0 + rows, :] + bf_ref[...]) * LOG2E
        cum = jnp.dot(tri, lf, preferred_element_type=F32, precision=HIGHEST) + carry
        cum_ref[r0:r0 + rows, :] = cum
        carry = cum[rows - 1:rows, :]


def pool_and_cum(a_in, f_pre, b_forget, pool_w_bd, pool_scale, *, rows=256):
    b, s, width = a_in.shape
    fw = f_pre.shape[-1]
    return pl.pallas_call(
        functools.partial(_pool_kernel, rows=rows),
        out_shape=[jax.ShapeDtypeStruct((b, s, width), BF16),
                   jax.ShapeDtypeStruct((b, s, fw), F32)],
        grid=(b,),
        in_specs=[pl.BlockSpec((None, s, width), lambda i: (i, 0, 0)),
                  pl.BlockSpec((None, s, fw), lambda i: (i, 0, 0)),
                  pl.BlockSpec((1, fw), lambda i: (0, 0)),
                  pl.BlockSpec((width, width), lambda i: (0, 0)),
                  pl.BlockSpec((1, width), lambda i: (0, 0))],
        out_specs=[pl.BlockSpec((None, s, width), lambda i: (i, 0, 0)),
                   pl.BlockSpec((None, s, fw), lambda i: (i, 0, 0))],
        compiler_params=_params("parallel"),
        name="pool_and_cum",
    )(a_in, f_pre, b_forget, pool_w_bd, pool_scale)


def _fox_kernel(q_ref, k_ref, v_ref, cum_ref, o_ref, *, tq):
    i = pl.program_id(2)
    lane = lax.broadcasted_iota(jnp.int32, (1, LANES), 1)
    first = lane < HEAD_DIM
    q = q_ref[...]
    q_heads = (jnp.where(first, q, jnp.zeros_like(q)), jnp.where(first, jnp.zeros_like(q), q))
    row = lax.broadcasted_iota(jnp.int32, (tq, tq), 0)
    col = lax.broadcasted_iota(jnp.int32, (tq, tq), 1)

    def logits(j):
        start = pl.multiple_of(j * tq, tq)
        k = k_ref[pl.ds(start, tq), :]
        return tuple(lax.dot_general(q_heads[h], k, NT_DIMS, preferred_element_type=F32)
                     for h in range(2))

    def block(j, scs, carry, masked):
        ms, accs = carry
        start = pl.multiple_of(j * tq, tq)
        v = v_ref[pl.ds(start, tq), :]
        ones = jnp.ones_like(v)
        v_heads = (jnp.where(first, v, ones), jnp.where(first, ones, v))
        ck = cum_ref[j]
        new_ms, new_accs = [], []
        for h in range(2):
            sc = scs[h] - ck[h:h + 1, :]
            if masked:
                sc = jnp.where(col <= row, sc, NEG_INF)
            m_new = jnp.maximum(ms[h], jnp.max(sc, axis=-1, keepdims=True))
            alpha = jnp.exp2(ms[h] - m_new)
            p = jnp.exp2((sc - m_new).astype(BF16))
            new_ms.append(m_new)
            new_accs.append(alpha * accs[h] + jnp.dot(p, v_heads[h], preferred_element_type=F32))
        return tuple(new_ms), tuple(new_accs)

    init = ((jnp.full((tq, 1), NEG_INF, F32),) * 2, (jnp.zeros((tq, LANES), F32),) * 2)
    def pair(t, c):
        return block(2 * t + 1, logits(2 * t + 1), block(2 * t, logits(2 * t), c, False), False)

    carry = lax.fori_loop(0, i // 2, pair, init)
    carry = lax.fori_loop(2 * (i // 2), i, lambda j, c: block(j, logits(j), c, False), carry)
    _, accs = block(i, logits(i), carry, True)
    out0 = accs[0] / accs[0][:, HEAD_DIM:HEAD_DIM + 1]
    out1 = accs[1] / accs[1][:, 0:1]
    o_ref[...] = jnp.where(first, out0, out1).astype(o_ref.dtype)


def fox_attention(qkv, cum_blocks, *, tq=512):
    b, s, w3 = qkv.shape
    n_pairs = w3 // 3 // LANES
    return pl.pallas_call(
        functools.partial(_fox_kernel, tq=tq),
        out_shape=jax.ShapeDtypeStruct((b, s, w3 // 3), BF16),
        grid=(b, n_pairs, s // tq),
        in_specs=[pl.BlockSpec((None, tq, LANES), lambda bi, p, i: (bi, i, p)),
                  pl.BlockSpec((None, s, LANES), lambda bi, p, i: (bi, 0, n_pairs + p)),
                  pl.BlockSpec((None, s, LANES), lambda bi, p, i: (bi, 0, 2 * n_pairs + p)),
                  pl.BlockSpec((None, None, s // tq, 2, tq), lambda bi, p, i: (bi, p, 0, 0, 0))],
        out_specs=pl.BlockSpec((None, tq, LANES), lambda bi, p, i: (bi, i, p)),
        compiler_params=_params("parallel", "parallel", "arbitrary"),
        name="fox_attention",
    )(qkv, qkv, qkv, cum_blocks)


def _sb_kernel(q_ref, k_ref, v_ref, o_ref, *, tq, tk):
    i = pl.program_id(2)
    per = tq // tk
    lane = lax.broadcasted_iota(jnp.int32, (1, LANES), 1)
    first = lane < HEAD_DIM
    q = q_ref[...]
    q_heads = (jnp.where(first, q, jnp.zeros_like(q)), jnp.where(first, jnp.zeros_like(q), q))
    row = lax.broadcasted_iota(jnp.int32, (tq, tk), 0)
    col = lax.broadcasted_iota(jnp.int32, (tq, tk), 1)
    neg_suffix = jnp.where(lax.broadcasted_iota(jnp.int32, (tk, tk), 0)
                           > lax.broadcasted_iota(jnp.int32, (tk, tk), 1), -1.0, 0.0).astype(BF16)

    def logits(j):
        start = pl.multiple_of(j * tk, tk)
        k = k_ref[pl.ds(start, tk), :]
        return tuple(lax.dot_general(q_heads[h], k, NT_DIMS, preferred_element_type=F32)
                     for h in range(2))

    def block(j, zs, carry, diagonal):
        used, acc = carry
        start = pl.multiple_of(j * tk, tk)
        v = v_ref[pl.ds(start, tk), :]
        if diagonal:
            valid = col + (j * tk - i * tq) < row
        new_used, pvs = [], []
        for h in range(2):
            z = zs[h]
            sp = jnp.maximum(z, jnp.log2(1.0 + jnp.exp2(jnp.minimum(z, SOFTPLUS2_CLAMP))))
            if diagonal:
                sp = jnp.where(valid, sp, 0.0)
            later_in = jnp.dot(sp.astype(BF16), neg_suffix, preferred_element_type=F32)
            w = jnp.exp2((z - sp + (later_in - used[h])).astype(BF16))
            if diagonal:
                w = jnp.where(valid, w, jnp.zeros_like(w))
            pvs.append(jnp.dot(w, v, preferred_element_type=F32))
            new_used.append(used[h] + (sp[:, 0:1] - later_in[:, 0:1]))
        return tuple(new_used), acc + jnp.where(first, pvs[0], pvs[1])

    carry = ((jnp.zeros((tq, 1), F32),) * 2, jnp.zeros((tq, LANES), F32))
    for d in range(per):
        j = (i + 1) * per - 1 - d
        carry = block(j, logits(j), carry, True)

    def pair(t, carry):
        j = i * per - 1 - 2 * t
        return block(j - 1, logits(j - 1), block(j, logits(j), carry, False), False)

    assert per % 2 == 0
    _, acc = lax.fori_loop(0, i * per // 2, pair, carry)
    o_ref[...] = acc.astype(o_ref.dtype)


def sb_attention(qkv, *, tq=512, tk=256):
    b, s, w3 = qkv.shape
    n_pairs = w3 // 3 // LANES
    return pl.pallas_call(
        functools.partial(_sb_kernel, tq=tq, tk=tk),
        out_shape=jax.ShapeDtypeStruct((b, s, w3 // 3), BF16),
        grid=(b, n_pairs, s // tq),
        in_specs=[pl.BlockSpec((None, tq, LANES), lambda bi, p, i: (bi, i, p)),
                  pl.BlockSpec((None, s, LANES), lambda bi, p, i: (bi, 0, n_pairs + p)),
                  pl.BlockSpec((None, s, LANES), lambda bi, p, i: (bi, 0, 2 * n_pairs + p))],
        out_specs=pl.BlockSpec((None, tq, LANES), lambda bi, p, i: (bi, i, p)),
        compiler_params=_params("parallel", "parallel", "arbitrary"),
        name="sb_attention",
    )(qkv, qkv, qkv)


def _conv_kernel(x_ref, halo_ref, w_ref, colscale_ref, o_ref):
    i = pl.program_id(1)
    halo = jnp.where(i == 0, 0.0, halo_ref[...])
    ext = jnp.concatenate([halo, x_ref[...]], axis=0)
    y = ext * w_ref[CONV_WIDTH - 1:CONV_WIDTH, :]
    for back in range(1, CONV_WIDTH):
        y = y + pltpu.roll(ext, back, 0) * w_ref[CONV_WIDTH - 1 - back:CONV_WIDTH - back, :]
    y = y[CONV_HALO:, :]
    o_ref[...] = (y * _sigmoid(y) * colscale_ref[...]).astype(o_ref.dtype)


def conv_silu(x, conv_w, colscale, *, rows=512):
    b, s, c = x.shape
    per = rows // CONV_HALO
    return pl.pallas_call(
        _conv_kernel,
        out_shape=jax.ShapeDtypeStruct((b, s, c), BF16),
        grid=(b, s // rows),
        in_specs=[pl.BlockSpec((None, rows, c), lambda bi, i: (bi, i, 0)),
                  pl.BlockSpec((None, CONV_HALO, c),
                               lambda bi, i: (bi, jnp.maximum(i * per - 1, 0), 0)),
                  pl.BlockSpec((CONV_WIDTH, c), lambda bi, i: (0, 0)),
                  pl.BlockSpec((1, c), lambda bi, i: (0, 0))],
        out_specs=pl.BlockSpec((None, rows, c), lambda bi, i: (bi, i, 0)),
        compiler_params=_params("parallel", "parallel"),
        name="conv_silu",
    )(x, x, conv_w, colscale)


def _mlstm_kernel(bi_ref, bf_ref, q_ref, k_ref, v_ref, vt_ref, li_ref, lf_ref, o_ref, g_ref,
                  out_ref, c_scr, n_scr, m_scr):
    head = pl.program_id(1)
    step = pl.program_id(2)
    L = q_ref.shape[0]

    @pl.when(step == 0)
    def _():
        c_scr[...] = jnp.zeros_like(c_scr)
        n_scr[...] = jnp.zeros_like(n_scr)
        m_scr[...] = jnp.zeros_like(m_scr)

    r_idx = lax.broadcasted_iota(jnp.int32, (L, L), 0)
    c_idx = lax.broadcasted_iota(jnp.int32, (L, L), 1)
    lower = r_idx >= c_idx
    tri_u = (r_idx <= c_idx).astype(F32)

    q = q_ref[...]
    k = k_ref[...]
    c_st = c_scr[...]
    n_st = n_scr[...]
    m_st = m_scr[...]
    lf_row = _log_sigmoid(lf_ref[...] + bf_ref[head])
    li_row = li_ref[...] + bi_ref[head]
    bcum_row = jnp.dot(jnp.broadcast_to(lf_row, (8, L)), tri_u, preferred_element_type=F32,
                       precision=HIGHEST)[0:1, :]
    bcum_col = jnp.broadcast_to(bcum_row, (LANES, L)).T[:, 0:1]
    g = jnp.sum(lf_row, axis=-1, keepdims=True)

    log_d = jnp.where(lower, bcum_col - bcum_row + li_row, NEG_INF)
    inter = bcum_col + m_st
    m_out = jnp.maximum(inter, jnp.max(log_d, axis=-1, keepdims=True))
    d_mat = jnp.exp(log_d - m_out)
    inter_w = jnp.exp(inter - m_out)
    qk = lax.dot_general(q, k, NT_DIMS, preferred_element_type=F32) * d_mat
    num = (jnp.dot(qk.astype(BF16), v_ref[...], preferred_element_type=F32)
           + inter_w * lax.dot_general(q, c_st.astype(BF16), NT_DIMS, preferred_element_type=F32))
    den = (jnp.sum(qk, axis=-1, keepdims=True)
           + inter_w * jnp.sum(q.astype(F32) * n_st, axis=-1, keepdims=True))
    hh = num / jnp.maximum(jnp.abs(den), jnp.exp(-m_out))
    hn = _rms(hh, g_ref[...])
    out_ref[...] = (hn * _sigmoid(o_ref[...])).astype(out_ref.dtype)

    a_row = g - bcum_row + li_row
    m_loc = jnp.max(a_row, axis=-1, keepdims=True)
    w_row = jnp.exp(a_row - m_loc)
    c_loc = jnp.dot((vt_ref[...].astype(F32) * w_row).astype(BF16), k, preferred_element_type=F32)
    n_loc = jnp.dot(jnp.broadcast_to(w_row, (8, L)).astype(BF16), k,
                    preferred_element_type=F32)[0:1, :]
    m_new = jnp.maximum(g + m_st, m_loc)
    decay = jnp.exp(g + m_st - m_new)
    fresh = jnp.exp(m_loc - m_new)
    c_scr[...] = decay * c_st + fresh * c_loc
    n_scr[...] = decay * n_st + fresh * n_loc
    m_scr[...] = m_new


def mlstm(qk, v, v_t, li, lf, o_gate, b_i, b_f, head_g):
    b, s, w = v.shape
    n_heads = w // LANES
    rows = v_t.shape[-1]
    grid_spec = pltpu.PrefetchScalarGridSpec(
        num_scalar_prefetch=2,
        grid=(b, n_heads, s // rows),
        in_specs=[pl.BlockSpec((None, rows, LANES), lambda bi, h, c, *_: (bi, c, h)),
                  pl.BlockSpec((None, rows, LANES), lambda bi, h, c, *_: (bi, c, n_heads + h)),
                  pl.BlockSpec((None, rows, LANES), lambda bi, h, c, *_: (bi, c, h)),
                  pl.BlockSpec((None, None, None, LANES, rows),
                               lambda bi, h, c, *_: (bi, h, c, 0, 0)),
                  pl.BlockSpec((None, None, None, 1, rows), lambda bi, h, c, *_: (bi, h, c, 0, 0)),
                  pl.BlockSpec((None, None, None, 1, rows), lambda bi, h, c, *_: (bi, h, c, 0, 0)),
                  pl.BlockSpec((None, rows, LANES), lambda bi, h, c, *_: (bi, c, h)),
                  pl.BlockSpec((None, 1, LANES), lambda bi, h, c, *_: (h, 0, 0))],
        out_specs=pl.BlockSpec((None, rows, LANES), lambda bi, h, c, *_: (bi, c, h)),
        scratch_shapes=[pltpu.VMEM((LANES, LANES), F32), pltpu.VMEM((1, LANES), F32),
                        pltpu.VMEM((1, 1), F32)])
    return pl.pallas_call(
        _mlstm_kernel,
        out_shape=jax.ShapeDtypeStruct((b, s, w), BF16),
        grid_spec=grid_spec,
        compiler_params=_params("parallel", "parallel", "arbitrary"),
        name="mlstm",
    )(b_i, b_f, qk, qk, v, v_t, li, lf, o_gate, head_g.reshape(n_heads, 1, LANES))


def _out_proj_kernel(x_ref, a_ref, b_ref, w_ref, o_ref):
    na = a_ref.shape[1]
    acc = jnp.dot(a_ref[...], w_ref[0:na, :], preferred_element_type=F32)
    acc = acc + jnp.dot(b_ref[...], w_ref[na:, :], preferred_element_type=F32)
    o_ref[...] = x_ref[...] + acc


def out_proj_residual(x, a, b_part, w, *, tm=512):
    t, d = x.shape
    return pl.pallas_call(
        _out_proj_kernel,
        out_shape=jax.ShapeDtypeStruct((t, d), F32),
        grid=(t // tm,),
        in_specs=[pl.BlockSpec((tm, d), lambda i: (i, 0)),
                  pl.BlockSpec((tm, a.shape[1]), lambda i: (i, 0)),
                  pl.BlockSpec((tm, b_part.shape[1]), lambda i: (i, 0)),
                  pl.BlockSpec(w.shape, lambda i: (0, 0))],
        out_specs=pl.BlockSpec((tm, d), lambda i: (i, 0)),
        compiler_params=_params("parallel"),
        name="out_proj_residual",
    )(x, a, b_part, w)


def _ff_chunks(tf):
    return [(c0, min(c0 + FF_CHUNK, tf)) for c0 in range(0, tf, FF_CHUNK)]


def _swiglu_accumulate(h_scr, wg_ref, wu_ref, wd_ref, act_scr, o_ref, after_chunk=None):
    h = h_scr[...]
    for c, (c0, c1) in enumerate(_ff_chunks(wg_ref.shape[1])):
        gate = jnp.dot(h, wg_ref[:, c0:c1], preferred_element_type=F32)
        up = jnp.dot(h, wu_ref[:, c0:c1], preferred_element_type=F32)
        act_scr[:, c0:c1] = (gate * _sigmoid(gate) * up).astype(BF16)
        if after_chunk is not None:
            after_chunk(c)
    o_ref[...] += jnp.dot(act_scr[...], wd_ref[...], preferred_element_type=F32)


def _ffn_kernel(x_ref, g_ref, wg_ref, wu_ref, wd_ref, o_ref, h_scr, act_scr):
    @pl.when(pl.program_id(1) == 0)
    def _():
        x = x_ref[...]
        h_scr[...] = _rms(x, g_ref[...]).astype(BF16)
        o_ref[...] = x

    _swiglu_accumulate(h_scr, wg_ref, wu_ref, wd_ref, act_scr, o_ref)


def ffn_residual(x, g, w_gate, w_up, w_down, layer, *, tm=1024, tf=1792):
    t, d = x.shape
    ff = w_gate.shape[2]
    return pl.pallas_call(
        _ffn_kernel,
        out_shape=jax.ShapeDtypeStruct((t, d), F32),
        grid=(t // tm, ff // tf),
        in_specs=[pl.BlockSpec((tm, d), lambda i, j: (i, 0)),
                  pl.BlockSpec((1, d), lambda i, j: (0, 0)),
                  pl.BlockSpec((None, d, tf), lambda i, j: (layer, 0, j)),
                  pl.BlockSpec((None, d, tf), lambda i, j: (layer, 0, j)),
                  pl.BlockSpec((None, tf, d), lambda i, j: (layer, j, 0))],
        out_specs=pl.BlockSpec((tm, d), lambda i, j: (i, 0)),
        scratch_shapes=[pltpu.VMEM((tm, d), BF16), pltpu.VMEM((tm, tf), BF16)],
        compiler_params=_params("parallel", "arbitrary"),
        name="ffn_residual",
    )(x, g.reshape(1, d), w_gate, w_up, w_down)


def _router_kernel(x_ref, g_ref, wr_ref, h_ref, route_ref):
    h = _rms(x_ref[...], g_ref[...])
    h_ref[...] = h
    logits = jnp.dot(h, wr_ref[...], preferred_element_type=F32, precision=HIGHEST)
    lane = lax.broadcasted_iota(jnp.int32, logits.shape, 1)
    logits = jnp.where(lane < N_EXPERTS, logits, NEG_INF)
    m1 = jnp.max(logits, axis=-1, keepdims=True)
    i1 = jnp.min(jnp.where(logits == m1, lane, LANES), axis=-1, keepdims=True)
    rest = jnp.where(lane == i1, NEG_INF, logits)
    m2 = jnp.max(rest, axis=-1, keepdims=True)
    i2 = jnp.min(jnp.where(rest == m2, lane, LANES), axis=-1, keepdims=True)
    e2 = jnp.exp(m2 - m1)
    g1 = 1.0 / (1.0 + e2)
    g2 = e2 / (1.0 + e2)
    route_ref[...] = jnp.where(lane == 0, i1.astype(F32),
                               jnp.where(lane == 1, i2.astype(F32),
                                         jnp.where(lane == 2, g1, jnp.where(lane == 3, g2, 0.0))))


def router(x, g, w_router_padded, *, tm=512):
    t, d = x.shape
    return pl.pallas_call(
        _router_kernel,
        out_shape=[jax.ShapeDtypeStruct((t, d), F32), jax.ShapeDtypeStruct((t, LANES), F32)],
        grid=(t // tm,),
        in_specs=[pl.BlockSpec((tm, d), lambda i: (i, 0)),
                  pl.BlockSpec((1, d), lambda i: (0, 0)),
                  pl.BlockSpec((d, LANES), lambda i: (0, 0))],
        out_specs=[pl.BlockSpec((tm, d), lambda i: (i, 0)),
                   pl.BlockSpec((tm, LANES), lambda i: (i, 0))],
        compiler_params=_params("parallel"),
        name="router",
    )(x, g.reshape(1, d), w_router_padded)


def _expert_kernel(be_ref, nused_ref, tok_cur_ref, tok_next_ref, h_hbm, wg_ref, wu_ref, wd_ref,
                   o_ref, xbuf, sems, h_scr, act_scr, *, tm, n_j):
    i = pl.program_id(0)
    j = pl.program_id(1)
    n_used = nused_ref[0]
    used = i < n_used

    chunks_per_step = len(_ff_chunks(wg_ref.shape[1]))
    batch = tm // (chunks_per_step * n_j)

    def issue_gather(tok_ref, slot, first, count):
        def body(r, _):
            pltpu.make_async_copy(h_hbm.at[pl.ds(tok_ref[0, first + r], 1)],
                                  xbuf.at[slot, pl.ds(first + r, 1)], sems.at[slot]).start()
            return 0
        lax.fori_loop(0, count, body, 0, unroll=8)

    @pl.when(jnp.logical_and(i == 0, j == 0))
    def _():
        issue_gather(tok_cur_ref, 0, 0, tm)

    @pl.when(jnp.logical_and(used, j == 0))
    def _():
        slot = i % 2
        pltpu.make_async_copy(xbuf.at[slot], xbuf.at[slot], sems.at[slot]).wait()
        h_scr[...] = xbuf[slot].astype(BF16)

    @pl.when(j == 0)
    def _():
        o_ref[...] = jnp.zeros_like(o_ref)

    def issue_next_batch(c):
        @pl.when(i + 1 < n_used)
        def _():
            issue_gather(tok_next_ref, (i + 1) % 2, (j * chunks_per_step + c) * batch, batch)

    @pl.when(used)
    def _():
        _swiglu_accumulate(h_scr, wg_ref, wu_ref, wd_ref, act_scr, o_ref, issue_next_batch)


def expert_ffn(h, row_tok, block_expert, n_used, w_gate, w_up, w_down, layer, *, tm, tf=1792):
    n_rows = row_tok.shape[0]
    d = h.shape[1]
    ff = w_gate.shape[3]
    n_blocks = n_rows // tm
    last_j = ff // tf - 1

    def up_block(i, j, be, nu):
        return (layer, be[jnp.minimum(i, nu[0] - 1)], 0, jnp.where(i < nu[0], j, last_j))

    def down_block(i, j, be, nu):
        return (layer, be[jnp.minimum(i, nu[0] - 1)], jnp.where(i < nu[0], j, last_j), 0)

    grid_spec = pltpu.PrefetchScalarGridSpec(
        num_scalar_prefetch=2,
        grid=(n_blocks, ff // tf),
        in_specs=[pl.BlockSpec((None, 1, tm), lambda i, j, be, nu: (i, 0, 0),
                               memory_space=pltpu.SMEM),
                  pl.BlockSpec((None, 1, tm),
                               lambda i, j, be, nu: (jnp.minimum(i + 1, n_blocks - 1), 0, 0),
                               memory_space=pltpu.SMEM),
                  pl.BlockSpec(memory_space=pl.ANY),
                  pl.BlockSpec((None, None, d, tf), up_block),
                  pl.BlockSpec((None, None, d, tf), up_block),
                  pl.BlockSpec((None, None, tf, d), down_block)],
        out_specs=pl.BlockSpec((tm, d), lambda i, j, be, nu: (i, 0)),
        scratch_shapes=[pltpu.VMEM((2, tm, d), F32), pltpu.SemaphoreType.DMA((2,)),
                        pltpu.VMEM((tm, d), BF16), pltpu.VMEM((tm, tf), BF16)])
    tok_blocks = row_tok.reshape(n_blocks, 1, tm)
    return pl.pallas_call(
        functools.partial(_expert_kernel, tm=tm, n_j=ff // tf),
        out_shape=jax.ShapeDtypeStruct((n_rows, d), F32),
        grid_spec=grid_spec,
        compiler_params=_params("arbitrary", "arbitrary"),
        name="expert_ffn",
    )(block_expert, n_used, tok_blocks, tok_blocks, h, w_gate, w_up, w_down)


def _combine_kernel(pos_cur_ref, pos_next_ref, x_ref, route_ref, y_ref, *rest, rows, final):
    if final:
        g_ref, o_ref, buf, sems = rest
    else:
        o_ref, buf, sems = rest
    i = pl.program_id(0)
    n = pl.num_programs(0)

    def issue(pos_ref, slot):
        def body(r, _):
            for kk in range(TOP_K):
                pltpu.make_async_copy(y_ref.at[pl.ds(pos_ref[0, TOP_K * r + kk], 1)],
                                      buf.at[slot, kk, pl.ds(r, 1)], sems.at[slot]).start()
            return 0
        lax.fori_loop(0, rows, body, 0, unroll=4)

    @pl.when(i == 0)
    def _():
        issue(pos_cur_ref, 0)

    @pl.when(i + 1 < n)
    def _():
        issue(pos_next_ref, (i + 1) % 2)

    slot = i % 2
    pltpu.make_async_copy(buf.at[slot], buf.at[slot], sems.at[slot]).wait()
    route = route_ref[...]
    g1 = route[:, 2:3]
    g2 = route[:, 3:4]
    out = x_ref[...] + g1 * buf[slot, 0] + g2 * buf[slot, 1]
    o_ref[...] = _rms(out, g_ref[...]) if final else out


def combine_residual(x, route, y_rows, pos, final_g=None, *, rows=256):
    t, d = x.shape
    final = final_g is not None
    n_steps = t // rows
    grid_spec = pltpu.PrefetchScalarGridSpec(
        num_scalar_prefetch=0,
        grid=(n_steps,),
        in_specs=[pl.BlockSpec((None, 1, TOP_K * rows), lambda i: (i, 0, 0),
                               memory_space=pltpu.SMEM),
                  pl.BlockSpec((None, 1, TOP_K * rows),
                               lambda i: (jnp.minimum(i + 1, n_steps - 1), 0, 0),
                               memory_space=pltpu.SMEM),
                  pl.BlockSpec((rows, d), lambda i: (i, 0)),
                  pl.BlockSpec((rows, LANES), lambda i: (i, 0)),
                  pl.BlockSpec(memory_space=pl.ANY)]
                 + ([pl.BlockSpec((1, d), lambda i: (0, 0))] if final else []),
        out_specs=pl.BlockSpec((rows, d), lambda i: (i, 0)),
        scratch_shapes=[pltpu.VMEM((2, TOP_K, rows, d), F32), pltpu.SemaphoreType.DMA((2,))])
    pos_blocks = pos.reshape(n_steps, 1, TOP_K * rows)
    extra = (final_g.reshape(1, d),) if final else ()
    return pl.pallas_call(
        functools.partial(_combine_kernel, rows=rows, final=final),
        out_shape=jax.ShapeDtypeStruct((t, d), F32),
        grid_spec=grid_spec,
        compiler_params=_params("arbitrary"),
        name="combine_residual",
    )(pos_blocks, pos_blocks, x, route, y_rows, *extra)


def _pad_cols(w, n):
    return jnp.pad(w, ((0, 0), (0, n - w.shape[1])))


def _scale_q_cols(w_qkv, width):
    scale = jnp.concatenate([jnp.full((width,), LOG2E * HEAD_DIM ** -0.5, F32),
                             jnp.ones((2 * width,), F32)])
    return w_qkv * scale


def even_layer(x, b, s, g_mix, g_ffn, w_in, b_forget, pool_w, pool_scale, w_out,
               w_gate, w_up, w_down, e, *, fox_tq=512):
    t, d = x.shape
    pool_width = pool_w.shape[0] * pool_w.shape[1]
    n_heads = b_forget.shape[0]
    fox_width = n_heads * HEAD_DIM
    w_a = w_in[:, :pool_width].astype(BF16)
    w_qkv = _scale_q_cols(w_in[:, pool_width:pool_width + 3 * fox_width], fox_width).astype(BF16)
    w_f = _pad_cols(w_in[:, pool_width + 3 * fox_width:], LANES).astype(BF16)
    qkv, a_in, f_pre = norm_proj(x, g_mix, [w_qkv, w_a, w_f], [BF16, F32, F32])

    pool_bd = jax.scipy.linalg.block_diag(*[pool_w[i] for i in range(pool_w.shape[0])]).astype(BF16)
    a_out, cum = pool_and_cum(a_in.reshape(b, s, pool_width), f_pre.reshape(b, s, LANES),
                              _pad_cols(b_forget.reshape(1, n_heads), LANES), pool_bd,
                              pool_scale.reshape(1, pool_width))
    cum_blocks = cum[:, :, :n_heads].reshape(b, s // fox_tq, fox_tq, n_heads // 2, 2)
    cum_blocks = cum_blocks.transpose(0, 3, 1, 4, 2)
    attn = fox_attention(qkv.reshape(b, s, 3 * fox_width), cum_blocks, tq=fox_tq)
    x = out_proj_residual(x, a_out.reshape(t, pool_width), attn.reshape(t, fox_width),
                          w_out.astype(BF16))
    return ffn_residual(x, g_ffn, w_gate, w_up, w_down, e)


def odd_layer(x, b, s, g_mix, g_ffn, w_in, b_igate, b_fgate, conv_w, head_norm_g, w_out,
              w_router, w_gate, w_up, w_down, e, final_g=None):
    x = odd_mixer_block(x, b, s, g_mix, w_in, b_igate, b_fgate, conv_w, head_norm_g, w_out)
    return moe_block(x, g_ffn, w_router, w_gate, w_up, w_down, e, final_g)


def odd_mixer_block(x, b, s, g_mix, w_in, b_igate, b_fgate, conv_w, head_norm_g, w_out):
    t, d = x.shape
    n_ml = b_igate.shape[0]
    ml_width = n_ml * LANES
    sb_width = d - ml_width
    cols = [3 * sb_width, 2 * ml_width, ml_width, ml_width]
    edges = [0]
    for c in cols:
        edges.append(edges[-1] + c)
    w_sb, w_mqk, w_mv, w_mo = [w_in[:, edges[i]:edges[i + 1]] for i in range(4)]
    w_sb = _scale_q_cols(w_sb, sb_width)
    w_sb, w_mqk, w_mv, w_mo = [w.astype(BF16) for w in (w_sb, w_mqk, w_mv, w_mo)]
    w_gates = _pad_cols(w_in[:, edges[-1]:], LANES).astype(BF16)
    sqkv, ml_qk, ml_v, ml_o, gates = norm_proj(
        x, g_mix, [w_sb, w_mqk, w_mv, w_mo, w_gates], [BF16, F32, BF16, F32, F32])

    c_out = sb_attention(sqkv.reshape(b, s, 3 * sb_width))

    k_scale = jnp.concatenate([jnp.ones((ml_width,), F32),
                               jnp.full((ml_width,), LANES ** -0.5, F32)]).reshape(1, 2 * ml_width)
    qk = conv_silu(ml_qk.reshape(b, s, 2 * ml_width), conv_w, k_scale)
    nb = s // MLSTM_BLOCK
    v3 = ml_v.reshape(b, s, ml_width)
    v_t = v3.reshape(b, nb, MLSTM_BLOCK, n_ml, LANES).transpose(0, 3, 1, 4, 2)
    gate_rows = gates[:, :2 * n_ml].reshape(b, nb, MLSTM_BLOCK, 2 * n_ml).transpose(0, 3, 1, 2)
    gate_rows = gate_rows.reshape(b, 2 * n_ml, nb, 1, MLSTM_BLOCK)
    d_out = mlstm(qk, v3, v_t, gate_rows[:, :n_ml], gate_rows[:, n_ml:],
                  ml_o.reshape(b, s, ml_width), b_igate, b_fgate, head_norm_g)
    return out_proj_residual(x, c_out.reshape(t, sb_width), d_out.reshape(t, ml_width),
                             w_out.astype(BF16))


def moe_block(x, g_ffn, w_router, w_gate, w_up, w_down, e, final_g=None, *, expert_rows=1024):
    t, d = x.shape
    h, route = router(x, g_ffn, _pad_cols(w_router, LANES))
    flat_e = route[:, :TOP_K].astype(jnp.int32).reshape(-1)
    n_assign = t * TOP_K
    onehot = jax.nn.one_hot(flat_e, N_EXPERTS, dtype=jnp.int32)
    rank = jnp.sum((jnp.cumsum(onehot, axis=0) - onehot) * onehot, axis=-1)
    counts = jnp.sum(onehot, axis=0)
    padded = (counts + expert_rows - 1) // expert_rows * expert_rows
    group_end = jnp.cumsum(padded)
    pos = (group_end[flat_e] - padded[flat_e] + rank).astype(jnp.int32)
    n_rows = n_assign + N_EXPERTS * expert_rows
    n_blocks = n_rows // expert_rows
    flat_tok = jnp.arange(n_assign, dtype=jnp.int32) // TOP_K
    row_tok = jnp.zeros((n_rows,), jnp.int32).at[pos].set(flat_tok)
    block_expert = jnp.minimum(
        jnp.searchsorted(group_end, jnp.arange(n_blocks) * expert_rows, side='right'),
        N_EXPERTS - 1).astype(jnp.int32)
    n_used = (group_end[-1:] // expert_rows).astype(jnp.int32)

    y_rows = expert_ffn(h, row_tok, block_expert, n_used, w_gate, w_up, w_down, e, tm=expert_rows)
    return combine_residual(x, route, y_rows, pos, final_g)


def kernel(x, norm_mix_g, norm_ffn_g, norm_final_g, ev_w_in, ev_b_forget, ev_pool_w, ev_pool_scale, ev_w_out, ffn_w_gate, ffn_w_up, ffn_w_down, od_w_in, od_b_igate, od_b_fgate, od_conv_w, od_head_norm_g, od_w_out, moe_w_router, moe_w_gate, moe_w_up, moe_w_down):
    b, s, d = x.shape
    depth = norm_mix_g.shape[0]
    xt = x.reshape(b * s, d)
    ffn_w = [w.astype(BF16) for w in (ffn_w_gate, ffn_w_up, ffn_w_down)]
    moe_w = [w.astype(BF16) for w in (moe_w_gate, moe_w_up, moe_w_down)]
    for layer in range(depth):
        e = layer // 2
        if layer % 2 == 0:
            xt = even_layer(xt, b, s, norm_mix_g[layer], norm_ffn_g[layer], ev_w_in[e],
                            ev_b_forget[e], ev_pool_w[e], ev_pool_scale[e], ev_w_out[e],
                            *ffn_w, e)
        else:
            xt = odd_layer(xt, b, s, norm_mix_g[layer], norm_ffn_g[layer], od_w_in[e],
                           od_b_igate[e], od_b_fgate[e], od_conv_w[e], od_head_norm_g[e],
                           od_w_out[e], moe_w_router[e], *moe_w, e,
                           norm_final_g if layer == depth - 1 else None)
    assert depth % 2 == 0
    return xt.reshape(b, s, d)
```

```python
import functools

import jax
import jax.numpy as jnp
from jax import lax
from jax.experimental import pallas as pl
from jax.experimental.pallas import tpu as pltpu

F32 = jnp.float32
BF16 = jnp.bfloat16
EPS = 1e-6
NEG_INF = float("-inf")

LANES = 128
HEAD_DIM = 64
POOL_WINDOWS = (2, 4, 8, 16)
POOL_HALO = 16
CONV_WIDTH = 4
CONV_HALO = 8
FF_CHUNK = 512
MLSTM_BLOCK = 512
N_EXPERTS = 8
TOP_K = 2
VMEM_LIMIT = 56 * 1024 * 1024

LOG2E = 1.4426950408889634
SOFTPLUS2_CLAMP = 64.0
HIGHEST = lax.Precision.HIGHEST
NT_DIMS = (((1,), (1,)), ((), ()))


def _params(*semantics):
    return pltpu.CompilerParams(dimension_semantics=semantics, vmem_limit_bytes=VMEM_LIMIT)


def _rms(x, g):
    return x * lax.rsqrt(jnp.mean(x * x, axis=-1, keepdims=True) + EPS) * g


def _log_sigmoid(x):
    return jnp.minimum(x, 0.0) - jnp.log(1.0 + jnp.exp(-jnp.abs(x)))


def _sigmoid(x):
    return 1.0 / (1.0 + jnp.exp(-x))


def _norm_proj_kernel(x_ref, g_ref, *refs, n_out, col_chunk):
    w_refs, o_refs = refs[:n_out], refs[n_out:]
    h = _rms(x_ref[...], g_ref[...]).astype(BF16)
    for w_ref, o_ref in zip(w_refs, o_refs):
        n = w_ref.shape[1]
        for c0 in range(0, n, col_chunk):
            c1 = min(c0 + col_chunk, n)
            o_ref[:, c0:c1] = jnp.dot(
                h, w_ref[:, c0:c1], preferred_element_type=F32).astype(o_ref.dtype)


def norm_proj(x, g, weights, out_dtypes, *, tm=512, col_chunk=512):
    t, d = x.shape
    n_out = len(weights)
    return pl.pallas_call(
        functools.partial(_norm_proj_kernel, n_out=n_out, col_chunk=col_chunk),
        out_shape=[jax.ShapeDtypeStruct((t, w.shape[1]), dt) for w, dt in zip(weights, out_dtypes)],
        grid=(t // tm,),
        in_specs=[pl.BlockSpec((tm, d), lambda i: (i, 0)),
                  pl.BlockSpec((1, d), lambda i: (0, 0))]
                 + [pl.BlockSpec(w.shape, lambda i: (0, 0)) for w in weights],
        out_specs=[pl.BlockSpec((tm, w.shape[1]), lambda i: (i, 0)) for w in weights],
        compiler_params=_params("parallel"),
        name="norm_proj",
    )(x, g.reshape(1, d), *weights)


def _pool_kernel(a_ref, f_ref, bf_ref, w_ref, scale_ref, o_ref, cum_ref, *, rows):
    s, width = a_ref.shape
    group = lax.broadcasted_iota(jnp.int32, (1, width), 1) // (width // len(POOL_WINDOWS))
    window = jnp.where(group == 0, POOL_WINDOWS[0],
                       jnp.where(group == 1, POOL_WINDOWS[1],
                                 jnp.where(group == 2, POOL_WINDOWS[2], POOL_WINDOWS[3])))
    tri = (lax.broadcasted_iota(jnp.int32, (rows, rows), 0)
           >= lax.broadcasted_iota(jnp.int32, (rows, rows), 1)).astype(F32)
    carry = jnp.zeros((1, f_ref.shape[1]), F32)
    for r in range(s // rows):
        r0 = r * rows
        if r == 0:
            ext = jnp.concatenate([jnp.zeros((POOL_HALO, width), F32), a_ref[0:rows, :]], axis=0)
        else:
            ext = a_ref[r0 - POOL_HALO:r0 + rows, :]
        s2 = ext + pltpu.roll(ext, 1, 0)
        s4 = s2 + pltpu.roll(s2, 2, 0)
        s8 = s4 + pltpu.roll(s4, 4, 0)
        s16 = s8 + pltpu.roll(s8, 8, 0)
        wsum = jnp.where(group == 0, s2, jnp.where(group == 1, s4, jnp.where(group == 2, s8, s16)))
        wsum = wsum[POOL_HALO:, :]
        a = ext[POOL_HALO:, :]
        t1 = r0 + 1 + lax.broadcasted_iota(jnp.int32, (rows, 1), 0)
        count = jnp.minimum(t1, window).astype(F32)
        pooled = wsum / count - a
        mixed = jnp.dot(pooled.astype(BF16), w_ref[...], preferred_element_type=F32)
        o_ref[r0:r0 + rows, :] = (mixed * scale_ref[...]).astype(o_ref.dtype)
        lf = _log_sigmoid(f_ref[r0:r0 + rows, :] + bf_ref[...]) * LOG2E
        cum = jnp.dot(tri, lf, preferred_element_type=F32, precision=HIGHEST) + carry
        cum_ref[r0:r0 + rows, :] = cum
        carry = cum[rows - 1:rows, :]


def pool_and_cum(a_in, f_pre, b_forget, pool_w_bd, pool_scale, *, rows=256):
    b, s, width = a_in.shape
    fw = f_pre.shape[-1]
    return pl.pallas_call(
        functools.partial(_pool_kernel, rows=rows),
        out_shape=[jax.ShapeDtypeStruct((b, s, width), BF16),
                   jax.ShapeDtypeStruct((b, s, fw), F32)],
        grid=(b,),
        in_specs=[pl.BlockSpec((None, s, width), lambda i: (i, 0, 0)),
                  pl.BlockSpec((None, s, fw), lambda i: (i, 0, 0)),
                  pl.BlockSpec((1, fw), lambda i: (0, 0)),
                  pl.BlockSpec((width, width), lambda i: (0, 0)),
                  pl.BlockSpec((1, width), lambda i: (0, 0))],
        out_specs=[pl.BlockSpec((None, s, width), lambda i: (i, 0, 0)),
                   pl.BlockSpec((None, s, fw), lambda i: (i, 0, 0))],
        compiler_params=_params("parallel"),
        name="pool_and_cum",
    )(a_in, f_pre, b_forget, pool_w_bd, pool_scale)


def _fox_kernel(q_ref, k_ref, v_ref, cum_ref, o_ref, *, tq):
    i = pl.program_id(2)
    lane = lax.broadcasted_iota(jnp.int32, (1, LANES), 1)
    first = lane < HEAD_DIM
    q = q_ref[...]
    q_heads = (jnp.where(first, q, jnp.zeros_like(q)), jnp.where(first, jnp.zeros_like(q), q))
    row = lax.broadcasted_iota(jnp.int32, (tq, tq), 0)
    col = lax.broadcasted_iota(jnp.int32, (tq, tq), 1)

    def logits(j):
        start = pl.multiple_of(j * tq, tq)
        k = k_ref[pl.ds(start, tq), :]
        return tuple(lax.dot_general(q_heads[h], k, NT_DIMS, preferred_element_type=F32)
                     for h in range(2))

    def block(j, scs, carry, masked):
        ms, accs = carry
        start = pl.multiple_of(j * tq, tq)
        v = v_ref[pl.ds(start, tq), :]
        ones = jnp.ones_like(v)
        v_heads = (jnp.where(first, v, ones), jnp.where(first, ones, v))
        ck = cum_ref[j]
        new_ms, new_accs = [], []
        for h in range(2):
            sc = scs[h] - ck[h:h + 1, :]
            if masked:
                sc = jnp.where(col <= row, sc, NEG_INF)
            m_new = jnp.maximum(ms[h], jnp.max(sc, axis=-1, keepdims=True))
            alpha = jnp.exp2(ms[h] - m_new)
            p = jnp.exp2((sc - m_new).astype(BF16))
            new_ms.append(m_new)
            new_accs.append(alpha * accs[h] + jnp.dot(p, v_heads[h], preferred_element_type=F32))
        return tuple(new_ms), tuple(new_accs)

    init = ((jnp.full((tq, 1), NEG_INF, F32),) * 2, (jnp.zeros((tq, LANES), F32),) * 2)
    def pair(t, c):
        return block(2 * t + 1, logits(2 * t + 1), block(2 * t, logits(2 * t), c, False), False)

    carry = lax.fori_loop(0, i // 2, pair, init)
    carry = lax.fori_loop(2 * (i // 2), i, lambda j, c: block(j, logits(j), c, False), carry)
    _, accs = block(i, logits(i), carry, True)
    out0 = accs[0] / accs[0][:, HEAD_DIM:HEAD_DIM + 1]
    out1 = accs[1] / accs[1][:, 0:1]
    o_ref[...] = jnp.where(first, out0, out1).astype(o_ref.dtype)


def fox_attention(qkv, cum_blocks, *, tq=512):
    b, s, w3 = qkv.shape
    n_pairs = w3 // 3 // LANES
    return pl.pallas_call(
        functools.partial(_fox_kernel, tq=tq),
        out_shape=jax.ShapeDtypeStruct((b, s, w3 // 3), BF16),
        grid=(b, n_pairs, s // tq),
        in_specs=[pl.BlockSpec((None, tq, LANES), lambda bi, p, i: (bi, i, p)),
                  pl.BlockSpec((None, s, LANES), lambda bi, p, i: (bi, 0, n_pairs + p)),
                  pl.BlockSpec((None, s, LANES), lambda bi, p, i: (bi, 0, 2 * n_pairs + p)),
                  pl.BlockSpec((None, None, s // tq, 2, tq), lambda bi, p, i: (bi, p, 0, 0, 0))],
        out_specs=pl.BlockSpec((None, tq, LANES), lambda bi, p, i: (bi, i, p)),
        compiler_params=_params("parallel", "parallel", "arbitrary"),
        name="fox_attention",
    )(qkv, qkv, qkv, cum_blocks)


def _sb_kernel(q_ref, k_ref, v_ref, o_ref, *, tq, tk):
    i = pl.program_id(2)
    per = tq // tk
    lane = lax.broadcasted_iota(jnp.int32, (1, LANES), 1)
    first = lane < HEAD_DIM
    q = q_ref[...]
    q_heads = (jnp.where(first, q, jnp.zeros_like(q)), jnp.where(first, jnp.zeros_like(q), q))
    row = lax.broadcasted_iota(jnp.int32, (tq, tk), 0)
    col = lax.broadcasted_iota(jnp.int32, (tq, tk), 1)
    neg_suffix = jnp.where(lax.broadcasted_iota(jnp.int32, (tk, tk), 0)
                           > lax.broadcasted_iota(jnp.int32, (tk, tk), 1), -1.0, 0.0).astype(BF16)

    def logits(j):
        start = pl.multiple_of(j * tk, tk)
        k = k_ref[pl.ds(start, tk), :]
        return tuple(lax.dot_general(q_heads[h], k, NT_DIMS, preferred_element_type=F32)
                     for h in range(2))

    def block(j, zs, carry, diagonal):
        used, acc = carry
        start = pl.multiple_of(j * tk, tk)
        v = v_ref[pl.ds(start, tk), :]
        if diagonal:
            valid = col + (j * tk - i * tq) < row
        new_used, pvs = [], []
        for h in range(2):
            z = zs[h]
            sp = jnp.maximum(z, jnp.log2(1.0 + jnp.exp2(jnp.minimum(z, SOFTPLUS2_CLAMP))))
            if diagonal:
                sp = jnp.where(valid, sp, 0.0)
            later_in = jnp.dot(sp.astype(BF16), neg_suffix, preferred_element_type=F32)
            w = jnp.exp2((z - sp + (later_in - used[h])).astype(BF16))
            if diagonal:
                w = jnp.where(valid, w, jnp.zeros_like(w))
            pvs.append(jnp.dot(w, v, preferred_element_type=F32))
            new_used.append(used[h] + (sp[:, 0:1] - later_in[:, 0:1]))
        return tuple(new_used), acc + jnp.where(first, pvs[0], pvs[1])

    carry = ((jnp.zeros((tq, 1), F32),) * 2, jnp.zeros((tq, LANES), F32))
    for d in range(per):
        j = (i + 1) * per - 1 - d
        carry = block(j, logits(j), carry, True)

    def pair(t, carry):
        j = i * per - 1 - 2 * t
        return block(j - 1, logits(j - 1), block(j, logits(j), carry, False), False)

    assert per % 2 == 0
    _, acc = lax.fori_loop(0, i * per // 2, pair, carry)
    o_ref[...] = acc.astype(o_ref.dtype)


def sb_attention(qkv, *, tq=512, tk=256):
    b, s, w3 = qkv.shape
    n_pairs = w3 // 3 // LANES
    return pl.pallas_call(
        functools.partial(_sb_kernel, tq=tq, tk=tk),
        out_shape=jax.ShapeDtypeStruct((b, s, w3 // 3), BF16),
        grid=(b, n_pairs, s // tq),
        in_specs=[pl.BlockSpec((None, tq, LANES), lambda bi, p, i: (bi, i, p)),
                  pl.BlockSpec((None, s, LANES), lambda bi, p, i: (bi, 0, n_pairs + p)),
                  pl.BlockSpec((None, s, LANES), lambda bi, p, i: (bi, 0, 2 * n_pairs + p))],
        out_specs=pl.BlockSpec((None, tq, LANES), lambda bi, p, i: (bi, i, p)),
        compiler_params=_params("parallel", "parallel", "arbitrary"),
        name="sb_attention",
    )(qkv, qkv, qkv)


def _conv_kernel(x_ref, halo_ref, w_ref, colscale_ref, o_ref):
    i = pl.program_id(1)
    halo = jnp.where(i == 0, 0.0, halo_ref[...])
    ext = jnp.concatenate([halo, x_ref[...]], axis=0)
    y = ext * w_ref[CONV_WIDTH - 1:CONV_WIDTH, :]
    for back in range(1, CONV_WIDTH):
        y = y + pltpu.roll(ext, back, 0) * w_ref[CONV_WIDTH - 1 - back:CONV_WIDTH - back, :]
    y = y[CONV_HALO:, :]
    o_ref[...] = (y * _sigmoid(y) * colscale_ref[...]).astype(o_ref.dtype)


def conv_silu(x, conv_w, colscale, *, rows=512):
    b, s, c = x.shape
    per = rows // CONV_HALO
    return pl.pallas_call(
        _conv_kernel,
        out_shape=jax.ShapeDtypeStruct((b, s, c), BF16),
        grid=(b, s // rows),
        in_specs=[pl.BlockSpec((None, rows, c), lambda bi, i: (bi, i, 0)),
                  pl.BlockSpec((None, CONV_HALO, c),
                               lambda bi, i: (bi, jnp.maximum(i * per - 1, 0), 0)),
                  pl.BlockSpec((CONV_WIDTH, c), lambda bi, i: (0, 0)),
                  pl.BlockSpec((1, c), lambda bi, i: (0, 0))],
        out_specs=pl.BlockSpec((None, rows, c), lambda bi, i: (bi, i, 0)),
        compiler_params=_params("parallel", "parallel"),
        name="conv_silu",
    )(x, x, conv_w, colscale)


def _mlstm_kernel(bi_ref, bf_ref, q_ref, k_ref, v_ref, vt_ref, li_ref, lf_ref, o_ref, g_ref,
                  out_ref, c_scr, n_scr, m_scr):
    head = pl.program_id(1)
    step = pl.program_id(2)
    L = q_ref.shape[0]

    @pl.when(step == 0)
    def _():
        c_scr[...] = jnp.zeros_like(c_scr)
        n_scr[...] = jnp.zeros_like(n_scr)
        m_scr[...] = jnp.zeros_like(m_scr)

    r_idx = lax.broadcasted_iota(jnp.int32, (L, L), 0)
    c_idx = lax.broadcasted_iota(jnp.int32, (L, L), 1)
    lower = r_idx >= c_idx
    tri_u = (r_idx <= c_idx).astype(F32)

    q = q_ref[...]
    k = k_ref[...]
    c_st = c_scr[...]
    n_st = n_scr[...]
    m_st = m_scr[...]
    lf_row = _log_sigmoid(lf_ref[...] + bf_ref[head])
    li_row = li_ref[...] + bi_ref[head]
    bcum_row = jnp.dot(jnp.broadcast_to(lf_row, (8, L)), tri_u, preferred_element_type=F32,
                       precision=HIGHEST)[0:1, :]
    bcum_col = jnp.broadcast_to(bcum_row, (LANES, L)).T[:, 0:1]
    g = jnp.sum(lf_row, axis=-1, keepdims=True)

    log_d = jnp.where(lower, bcum_col - bcum_row + li_row, NEG_INF)
    inter = bcum_col + m_st
    m_out = jnp.maximum(inter, jnp.max(log_d, axis=-1, keepdims=True))
    d_mat = jnp.exp(log_d - m_out)
    inter_w = jnp.exp(inter - m_out)
    qk = lax.dot_general(q, k, NT_DIMS, preferred_element_type=F32) * d_mat
    num = (jnp.dot(qk.astype(BF16), v_ref[...], preferred_element_type=F32)
           + inter_w * lax.dot_general(q, c_st.astype(BF16), NT_DIMS, preferred_element_type=F32))
    den = (jnp.sum(qk, axis=-1, keepdims=True)
           + inter_w * jnp.sum(q.astype(F32) * n_st, axis=-1, keepdims=True))
    hh = num / jnp.maximum(jnp.abs(den), jnp.exp(-m_out))
    hn = _rms(hh, g_ref[...])
    out_ref[...] = (hn * _sigmoid(o_ref[...])).astype(out_ref.dtype)

    a_row = g - bcum_row + li_row
    m_loc = jnp.max(a_row, axis=-1, keepdims=True)
    w_row = jnp.exp(a_row - m_loc)
    c_loc = jnp.dot((vt_ref[...].astype(F32) * w_row).astype(BF16), k, preferred_element_type=F32)
    n_loc = jnp.dot(jnp.broadcast_to(w_row, (8, L)).astype(BF16), k,
                    preferred_element_type=F32)[0:1, :]
    m_new = jnp.maximum(g + m_st, m_loc)
    decay = jnp.exp(g + m_st - m_new)
    fresh = jnp.exp(m_loc - m_new)
    c_scr[...] = decay * c_st + fresh * c_loc
    n_scr[...] = decay * n_st + fresh * n_loc
    m_scr[...] = m_new


def mlstm(qk, v, v_t, li, lf, o_gate, b_i, b_f, head_g):
    b, s, w = v.shape
    n_heads = w // LANES
    rows = v_t.shape[-1]
    grid_spec = pltpu.PrefetchScalarGridSpec(
        num_scalar_prefetch=2,
        grid=(b, n_heads, s // rows),
        in_specs=[pl.BlockSpec((None, rows, LANES), lambda bi, h, c, *_: (bi, c, h)),
                  pl.BlockSpec((None, rows, LANES), lambda bi, h, c, *_: (bi, c, n_heads + h)),
                  pl.BlockSpec((None, rows, LANES), lambda bi, h, c, *_: (bi, c, h)),
                  pl.BlockSpec((None, None, None, LANES, rows),
                               lambda bi, h, c, *_: (bi, h, c, 0, 0)),
                  pl.BlockSpec((None, None, None, 1, rows), lambda bi, h, c, *_: (bi, h, c, 0, 0)),
                  pl.BlockSpec((None, None, None, 1, rows), lambda bi, h, c, *_: (bi, h, c, 0, 0)),
                  pl.BlockSpec((None, rows, LANES), lambda bi, h, c, *_: (bi, c, h)),
                  pl.BlockSpec((None, 1, LANES), lambda bi, h, c, *_: (h, 0, 0))],
        out_specs=pl.BlockSpec((None, rows, LANES), lambda bi, h, c, *_: (bi, c, h)),
        scratch_shapes=[pltpu.VMEM((LANES, LANES), F32), pltpu.VMEM((1, LANES), F32),
                        pltpu.VMEM((1, 1), F32)])
    return pl.pallas_call(
        _mlstm_kernel,
        out_shape=jax.ShapeDtypeStruct((b, s, w), BF16),
        grid_spec=grid_spec,
        compiler_params=_params("parallel", "parallel", "arbitrary"),
        name="mlstm",
    )(b_i, b_f, qk, qk, v, v_t, li, lf, o_gate, head_g.reshape(n_heads, 1, LANES))


def _out_proj_kernel(x_ref, a_ref, b_ref, w_ref, o_ref):
    na = a_ref.shape[1]
    acc = jnp.dot(a_ref[...], w_ref[0:na, :], preferred_element_type=F32)
    acc = acc + jnp.dot(b_ref[...], w_ref[na:, :], preferred_element_type=F32)
    o_ref[...] = x_ref[...] + acc


def out_proj_residual(x, a, b_part, w, *, tm=512):
    t, d = x.shape
    return pl.pallas_call(
        _out_proj_kernel,
        out_shape=jax.ShapeDtypeStruct((t, d), F32),
        grid=(t // tm,),
        in_specs=[pl.BlockSpec((tm, d), lambda i: (i, 0)),
                  pl.BlockSpec((tm, a.shape[1]), lambda i: (i, 0)),
                  pl.BlockSpec((tm, b_part.shape[1]), lambda i: (i, 0)),
                  pl.BlockSpec(w.shape, lambda i: (0, 0))],
        out_specs=pl.BlockSpec((tm, d), lambda i: (i, 0)),
        compiler_params=_params("parallel"),
        name="out_proj_residual",
    )(x, a, b_part, w)


def _ff_chunks(tf):
    return [(c0, min(c0 + FF_CHUNK, tf)) for c0 in range(0, tf, FF_CHUNK)]


def _swiglu_accumulate(h_scr, wg_ref, wu_ref, wd_ref, act_scr, o_ref, after_chunk=None):
    h = h_scr[...]
    for c, (c0, c1) in enumerate(_ff_chunks(wg_ref.shape[1])):
        gate = jnp.dot(h, wg_ref[:, c0:c1], preferred_element_type=F32)
        up = jnp.dot(h, wu_ref[:, c0:c1], preferred_element_type=F32)
        act_scr[:, c0:c1] = (gate * _sigmoid(gate) * up).astype(BF16)
        if after_chunk is not None:
            after_chunk(c)
    o_ref[...] += jnp.dot(act_scr[...], wd_ref[...], preferred_element_type=F32)


def _ffn_kernel(x_ref, g_ref, wg_ref, wu_ref, wd_ref, o_ref, h_scr, act_scr):
    @pl.when(pl.program_id(1) == 0)
    def _():
        x = x_ref[...]
        h_scr[...] = _rms(x, g_ref[...]).astype(BF16)
        o_ref[...] = x

    _swiglu_accumulate(h_scr, wg_ref, wu_ref, wd_ref, act_scr, o_ref)


def ffn_residual(x, g, w_gate, w_up, w_down, layer, *, tm=1024, tf=1792):
    t, d = x.shape
    ff = w_gate.shape[2]
    return pl.pallas_call(
        _ffn_kernel,
        out_shape=jax.ShapeDtypeStruct((t, d), F32),
        grid=(t // tm, ff // tf),
        in_specs=[pl.BlockSpec((tm, d), lambda i, j: (i, 0)),
                  pl.BlockSpec((1, d), lambda i, j: (0, 0)),
                  pl.BlockSpec((None, d, tf), lambda i, j: (layer, 0, j)),
                  pl.BlockSpec((None, d, tf), lambda i, j: (layer, 0, j)),
                  pl.BlockSpec((None, tf, d), lambda i, j: (layer, j, 0))],
        out_specs=pl.BlockSpec((tm, d), lambda i, j: (i, 0)),
        scratch_shapes=[pltpu.VMEM((tm, d), BF16), pltpu.VMEM((tm, tf), BF16)],
        compiler_params=_params("parallel", "arbitrary"),
        name="ffn_residual",
    )(x, g.reshape(1, d), w_gate, w_up, w_down)


def _store_row_tiles(ref, x):
    rows, d = x.shape
    per = d // LANES
    for s in range(per):
        ref[pl.ds(s, rows, stride=per), :] = x[:, s * LANES:(s + 1) * LANES]


def _load_row_tiles(ref, rows, per):
    return [ref[pl.ds(s, rows, stride=per), :] for s in range(per)]


def _router_kernel(x_ref, g_ref, wr_ref, h_ref, route_ref):
    h = _rms(x_ref[...], g_ref[...])
    _store_row_tiles(h_ref, h)
    logits = jnp.dot(h, wr_ref[...], preferred_element_type=F32, precision=HIGHEST)
    lane = lax.broadcasted_iota(jnp.int32, logits.shape, 1)
    logits = jnp.where(lane < N_EXPERTS, logits, NEG_INF)
    m1 = jnp.max(logits, axis=-1, keepdims=True)
    i1 = jnp.min(jnp.where(logits == m1, lane, LANES), axis=-1, keepdims=True)
    rest = jnp.where(lane == i1, NEG_INF, logits)
    m2 = jnp.max(rest, axis=-1, keepdims=True)
    i2 = jnp.min(jnp.where(rest == m2, lane, LANES), axis=-1, keepdims=True)
    e2 = jnp.exp(m2 - m1)
    g1 = 1.0 / (1.0 + e2)
    g2 = e2 / (1.0 + e2)
    route_ref[...] = jnp.where(lane == 0, i1.astype(F32),
                               jnp.where(lane == 1, i2.astype(F32),
                                         jnp.where(lane == 2, g1, jnp.where(lane == 3, g2, 0.0))))


def router(x, g, w_router_padded, *, tm=512):
    t, d = x.shape
    per = d // LANES
    return pl.pallas_call(
        _router_kernel,
        out_shape=[jax.ShapeDtypeStruct((t * per, LANES), F32),
                   jax.ShapeDtypeStruct((t, LANES), F32)],
        grid=(t // tm,),
        in_specs=[pl.BlockSpec((tm, d), lambda i: (i, 0)),
                  pl.BlockSpec((1, d), lambda i: (0, 0)),
                  pl.BlockSpec((d, LANES), lambda i: (0, 0))],
        out_specs=[pl.BlockSpec((tm * per, LANES), lambda i: (i, 0)),
                   pl.BlockSpec((tm, LANES), lambda i: (i, 0))],
        compiler_params=_params("parallel"),
        name="router",
    )(x, g.reshape(1, d), w_router_padded)


def _expert_kernel(be_ref, nused_ref, tok_cur_ref, tok_next_ref, h_hbm, wg_ref, wu_ref, wd_ref,
                   o_ref, xbuf, sems, h_scr, act_scr, acc_scr, *, tm):
    i = pl.program_id(0)
    j = pl.program_id(1)
    n_used = nused_ref[0]
    used = i < n_used
    per = h_scr.shape[1] // LANES

    def issue_gather(tok_ref, slot):
        def body(r, _):
            src = pl.multiple_of(tok_ref[0, r] * per, per)
            dst = pl.multiple_of(r * per, per)
            pltpu.make_async_copy(h_hbm.at[pl.ds(src, per)], xbuf.at[slot, pl.ds(dst, per)],
                                  sems.at[slot]).start()
            return 0
        lax.fori_loop(0, tm, body, 0, unroll=8)

    @pl.when(jnp.logical_and(i == 0, j == 0))
    def _():
        issue_gather(tok_cur_ref, 0)

    @pl.when(jnp.logical_and(i + 1 < n_used, j == 0))
    def _():
        issue_gather(tok_next_ref, (i + 1) % 2)

    @pl.when(jnp.logical_and(used, j == 0))
    def _():
        slot = i % 2
        pltpu.make_async_copy(xbuf.at[slot], xbuf.at[slot], sems.at[slot]).wait()
        for s, piece in enumerate(_load_row_tiles(xbuf.at[slot], tm, per)):
            h_scr[:, s * LANES:(s + 1) * LANES] = piece.astype(BF16)
        acc_scr[...] = jnp.zeros_like(acc_scr)

    @pl.when(used)
    def _():
        _swiglu_accumulate(h_scr, wg_ref, wu_ref, wd_ref, act_scr, acc_scr)

    @pl.when(jnp.logical_and(used, j == pl.num_programs(1) - 1))
    def _():
        _store_row_tiles(o_ref, acc_scr[...])

    @pl.when(jnp.logical_and(jnp.logical_not(used), j == 0))
    def _():
        o_ref[...] = jnp.zeros_like(o_ref)


def expert_ffn(h, row_tok, block_expert, n_used, w_gate, w_up, w_down, layer, *, tm, tf=1792):
    n_rows = row_tok.shape[0]
    d = w_gate.shape[2]
    per = d // LANES
    ff = w_gate.shape[3]
    n_blocks = n_rows // tm
    last_j = ff // tf - 1

    def up_block(i, j, be, nu):
        return (layer, be[jnp.minimum(i, nu[0] - 1)], 0, jnp.where(i < nu[0], j, last_j))

    def down_block(i, j, be, nu):
        return (layer, be[jnp.minimum(i, nu[0] - 1)], jnp.where(i < nu[0], j, last_j), 0)

    grid_spec = pltpu.PrefetchScalarGridSpec(
        num_scalar_prefetch=2,
        grid=(n_blocks, ff // tf),
        in_specs=[pl.BlockSpec((None, 1, tm), lambda i, j, be, nu: (i, 0, 0),
                               memory_space=pltpu.SMEM),
                  pl.BlockSpec((None, 1, tm),
                               lambda i, j, be, nu: (jnp.minimum(i + 1, n_blocks - 1), 0, 0),
                               memory_space=pltpu.SMEM),
                  pl.BlockSpec(memory_space=pl.ANY),
                  pl.BlockSpec((None, None, d, tf), up_block),
                  pl.BlockSpec((None, None, d, tf), up_block),
                  pl.BlockSpec((None, None, tf, d), down_block)],
        out_specs=pl.BlockSpec((tm * per, LANES), lambda i, j, be, nu: (i, 0)),
        scratch_shapes=[pltpu.VMEM((2, tm * per, LANES), F32), pltpu.SemaphoreType.DMA((2,)),
                        pltpu.VMEM((tm, d), BF16), pltpu.VMEM((tm, tf), BF16),
                        pltpu.VMEM((tm, d), F32)])
    tok_blocks = row_tok.reshape(n_blocks, 1, tm)
    return pl.pallas_call(
        functools.partial(_expert_kernel, tm=tm),
        out_shape=jax.ShapeDtypeStruct((n_rows * per, LANES), F32),
        grid_spec=grid_spec,
        compiler_params=_params("arbitrary", "arbitrary"),
        name="expert_ffn",
    )(block_expert, n_used, tok_blocks, tok_blocks, h, w_gate, w_up, w_down)


def _combine_kernel(pos_cur_ref, pos_next_ref, x_ref, route_ref, y_ref, *rest, rows, final):
    if final:
        g_ref, o_ref, buf, sems = rest
    else:
        o_ref, buf, sems = rest
    i = pl.program_id(0)
    n = pl.num_programs(0)
    per = x_ref.shape[1] // LANES

    def issue(pos_ref, slot):
        def body(r, _):
            dst = pl.multiple_of(r * per, per)
            for kk in range(TOP_K):
                src = pl.multiple_of(pos_ref[0, TOP_K * r + kk] * per, per)
                pltpu.make_async_copy(y_ref.at[pl.ds(src, per)],
                                      buf.at[slot, kk, pl.ds(dst, per)], sems.at[slot]).start()
            return 0
        lax.fori_loop(0, rows, body, 0, unroll=4)

    @pl.when(i == 0)
    def _():
        issue(pos_cur_ref, 0)

    @pl.when(i + 1 < n)
    def _():
        issue(pos_next_ref, (i + 1) % 2)

    slot = i % 2
    pltpu.make_async_copy(buf.at[slot], buf.at[slot], sems.at[slot]).wait()
    route = route_ref[...]
    g1 = route[:, 2:3]
    g2 = route[:, 3:4]
    y1 = jnp.concatenate(_load_row_tiles(buf.at[slot, 0], rows, per), axis=-1)
    y2 = jnp.concatenate(_load_row_tiles(buf.at[slot, 1], rows, per), axis=-1)
    out = x_ref[...] + g1 * y1 + g2 * y2
    o_ref[...] = _rms(out, g_ref[...]) if final else out


def combine_residual(x, route, y_rows, pos, final_g=None, *, rows=256):
    t, d = x.shape
    final = final_g is not None
    n_steps = t // rows
    grid_spec = pltpu.PrefetchScalarGridSpec(
        num_scalar_prefetch=0,
        grid=(n_steps,),
        in_specs=[pl.BlockSpec((None, 1, TOP_K * rows), lambda i: (i, 0, 0),
                               memory_space=pltpu.SMEM),
                  pl.BlockSpec((None, 1, TOP_K * rows),
                               lambda i: (jnp.minimum(i + 1, n_steps - 1), 0, 0),
                               memory_space=pltpu.SMEM),
                  pl.BlockSpec((rows, d), lambda i: (i, 0)),
                  pl.BlockSpec((rows, LANES), lambda i: (i, 0)),
                  pl.BlockSpec(memory_space=pl.ANY)]
                 + ([pl.BlockSpec((1, d), lambda i: (0, 0))] if final else []),
        out_specs=pl.BlockSpec((rows, d), lambda i: (i, 0)),
        scratch_shapes=[pltpu.VMEM((2, TOP_K, rows * (d // LANES), LANES), F32),
                        pltpu.SemaphoreType.DMA((2,))])
    pos_blocks = pos.reshape(n_steps, 1, TOP_K * rows)
    extra = (final_g.reshape(1, d),) if final else ()
    return pl.pallas_call(
        functools.partial(_combine_kernel, rows=rows, final=final),
        out_shape=jax.ShapeDtypeStruct((t, d), F32),
        grid_spec=grid_spec,
        compiler_params=_params("arbitrary"),
        name="combine_residual",
    )(pos_blocks, pos_blocks, x, route, y_rows, *extra)


def _pad_cols(w, n):
    return jnp.pad(w, ((0, 0), (0, n - w.shape[1])))


def _scale_q_cols(w_qkv, width):
    scale = jnp.concatenate([jnp.full((width,), LOG2E * HEAD_DIM ** -0.5, F32),
                             jnp.ones((2 * width,), F32)])
    return w_qkv * scale


def even_layer(x, b, s, g_mix, g_ffn, w_in, b_forget, pool_w, pool_scale, w_out,
               w_gate, w_up, w_down, e, *, fox_tq=512):
    t, d = x.shape
    pool_width = pool_w.shape[0] * pool_w.shape[1]
    n_heads = b_forget.shape[0]
    fox_width = n_heads * HEAD_DIM
    w_a = w_in[:, :pool_width].astype(BF16)
    w_qkv = _scale_q_cols(w_in[:, pool_width:pool_width + 3 * fox_width], fox_width).astype(BF16)
    w_f = _pad_cols(w_in[:, pool_width + 3 * fox_width:], LANES).astype(BF16)
    qkv, a_in, f_pre = norm_proj(x, g_mix, [w_qkv, w_a, w_f], [BF16, F32, F32])

    pool_bd = jax.scipy.linalg.block_diag(*[pool_w[i] for i in range(pool_w.shape[0])]).astype(BF16)
    a_out, cum = pool_and_cum(a_in.reshape(b, s, pool_width), f_pre.reshape(b, s, LANES),
                              _pad_cols(b_forget.reshape(1, n_heads), LANES), pool_bd,
                              pool_scale.reshape(1, pool_width))
    cum_blocks = cum[:, :, :n_heads].reshape(b, s // fox_tq, fox_tq, n_heads // 2, 2)
    cum_blocks = cum_blocks.transpose(0, 3, 1, 4, 2)
    attn = fox_attention(qkv.reshape(b, s, 3 * fox_width), cum_blocks, tq=fox_tq)
    x = out_proj_residual(x, a_out.reshape(t, pool_width), attn.reshape(t, fox_width),
                          w_out.astype(BF16))
    return ffn_residual(x, g_ffn, w_gate, w_up, w_down, e)


def odd_layer(x, b, s, g_mix, g_ffn, w_in, b_igate, b_fgate, conv_w, head_norm_g, w_out,
              w_router, w_gate, w_up, w_down, e, final_g=None):
    x = odd_mixer_block(x, b, s, g_mix, w_in, b_igate, b_fgate, conv_w, head_norm_g, w_out)
    return moe_block(x, g_ffn, w_router, w_gate, w_up, w_down, e, final_g)


def odd_mixer_block(x, b, s, g_mix, w_in, b_igate, b_fgate, conv_w, head_norm_g, w_out):
    t, d = x.shape
    n_ml = b_igate.shape[0]
    ml_width = n_ml * LANES
    sb_width = d - ml_width
    cols = [3 * sb_width, 2 * ml_width, ml_width, ml_width]
    edges = [0]
    for c in cols:
        edges.append(edges[-1] + c)
    w_sb, w_mqk, w_mv, w_mo = [w_in[:, edges[i]:edges[i + 1]] for i in range(4)]
    w_sb = _scale_q_cols(w_sb, sb_width)
    w_sb, w_mqk, w_mv, w_mo = [w.astype(BF16) for w in (w_sb, w_mqk, w_mv, w_mo)]
    w_gates = _pad_cols(w_in[:, edges[-1]:], LANES).astype(BF16)
    sqkv, ml_qk, ml_v, ml_o, gates = norm_proj(
        x, g_mix, [w_sb, w_mqk, w_mv, w_mo, w_gates], [BF16, F32, BF16, F32, F32])

    c_out = sb_attention(sqkv.reshape(b, s, 3 * sb_width))

    k_scale = jnp.concatenate([jnp.ones((ml_width,), F32),
                               jnp.full((ml_width,), LANES ** -0.5, F32)]).reshape(1, 2 * ml_width)
    qk = conv_silu(ml_qk.reshape(b, s, 2 * ml_width), conv_w, k_scale)
    nb = s // MLSTM_BLOCK
    v3 = ml_v.reshape(b, s, ml_width)
    v_t = v3.reshape(b, nb, MLSTM_BLOCK, n_ml, LANES).transpose(0, 3, 1, 4, 2)
    gate_rows = gates[:, :2 * n_ml].reshape(b, nb, MLSTM_BLOCK, 2 * n_ml).transpose(0, 3, 1, 2)
    gate_rows = gate_rows.reshape(b, 2 * n_ml, nb, 1, MLSTM_BLOCK)
    d_out = mlstm(qk, v3, v_t, gate_rows[:, :n_ml], gate_rows[:, n_ml:],
                  ml_o.reshape(b, s, ml_width), b_igate, b_fgate, head_norm_g)
    return out_proj_residual(x, c_out.reshape(t, sb_width), d_out.reshape(t, ml_width),
                             w_out.astype(BF16))


def moe_block(x, g_ffn, w_router, w_gate, w_up, w_down, e, final_g=None, *, expert_rows=1024):
    t, d = x.shape
    h, route = router(x, g_ffn, _pad_cols(w_router, LANES))
    flat_e = route[:, :TOP_K].astype(jnp.int32).reshape(-1)
    n_assign = t * TOP_K
    onehot = jax.nn.one_hot(flat_e, N_EXPERTS, dtype=jnp.int32)
    rank = jnp.sum((jnp.cumsum(onehot, axis=0) - onehot) * onehot, axis=-1)
    counts = jnp.sum(onehot, axis=0)
    padded = (counts + expert_rows - 1) // expert_rows * expert_rows
    group_end = jnp.cumsum(padded)
    pos = (group_end[flat_e] - padded[flat_e] + rank).astype(jnp.int32)
    n_rows = n_assign + N_EXPERTS * expert_rows
    n_blocks = n_rows // expert_rows
    flat_tok = jnp.arange(n_assign, dtype=jnp.int32) // TOP_K
    row_tok = jnp.zeros((n_rows,), jnp.int32).at[pos].set(flat_tok)
    block_expert = jnp.minimum(
        jnp.searchsorted(group_end, jnp.arange(n_blocks) * expert_rows, side='right'),
        N_EXPERTS - 1).astype(jnp.int32)
    n_used = (group_end[-1:] // expert_rows).astype(jnp.int32)

    y_rows = expert_ffn(h, row_tok, block_expert, n_used, w_gate, w_up, w_down, e, tm=expert_rows)
    return combine_residual(x, route, y_rows, pos, final_g)


def kernel(x, norm_mix_g, norm_ffn_g, norm_final_g, ev_w_in, ev_b_forget, ev_pool_w, ev_pool_scale, ev_w_out, ffn_w_gate, ffn_w_up, ffn_w_down, od_w_in, od_b_igate, od_b_fgate, od_conv_w, od_head_norm_g, od_w_out, moe_w_router, moe_w_gate, moe_w_up, moe_w_down):
    b, s, d = x.shape
    depth = norm_mix_g.shape[0]
    xt = x.reshape(b * s, d)
    ffn_w = [w.astype(BF16) for w in (ffn_w_gate, ffn_w_up, ffn_w_down)]
    moe_w = [w.astype(BF16) for w in (moe_w_gate, moe_w_up, moe_w_down)]
    for layer in range(depth):
        e = layer // 2
        if layer % 2 == 0:
            xt = even_layer(xt, b, s, norm_mix_g[layer], norm_ffn_g[layer], ev_w_in[e],
                            ev_b_forget[e], ev_pool_w[e], ev_pool_scale[e], ev_w_out[e],
                            *ffn_w, e)
        else:
            xt = odd_layer(xt, b, s, norm_mix_g[layer], norm_ffn_g[layer], od_w_in[e],
                           od_b_igate[e], od_b_fgate[e], od_conv_w[e], od_head_norm_g[e],
                           od_w_out[e], moe_w_router[e], *moe_w, e,
                           norm_final_g if layer == depth - 1 else None)
    assert depth % 2 == 0
    return xt.reshape(b, s, d)
```

```python
import functools

import jax
import jax.numpy as jnp
from jax import lax
from jax.experimental import pallas as pl
from jax.experimental.pallas import tpu as pltpu

F32 = jnp.float32
BF16 = jnp.bfloat16
EPS = 1e-6
NEG_INF = float("-inf")

LANES = 128
HEAD_DIM = 64
POOL_WINDOWS = (2, 4, 8, 16)
POOL_HALO = 16
CONV_WIDTH = 4
CONV_HALO = 8
FF_CHUNK = 512
MLSTM_BLOCK = 512
DMA_PRIORITIES = 2
N_EXPERTS = 8
TOP_K = 2
VMEM_LIMIT = 56 * 1024 * 1024

LOG2E = 1.4426950408889634
SOFTPLUS2_CLAMP = 64.0
HIGHEST = lax.Precision.HIGHEST
NT_DIMS = (((1,), (1,)), ((), ()))


def _params(*semantics):
    return pltpu.CompilerParams(dimension_semantics=semantics, vmem_limit_bytes=VMEM_LIMIT)


def _rms(x, g):
    return x * lax.rsqrt(jnp.mean(x * x, axis=-1, keepdims=True) + EPS) * g


def _log_sigmoid(x):
    return jnp.minimum(x, 0.0) - jnp.log(1.0 + jnp.exp(-jnp.abs(x)))


def _sigmoid(x):
    return 1.0 / (1.0 + jnp.exp(-x))


def _norm_proj_kernel(x_ref, g_ref, *refs, n_out, col_chunk):
    w_refs, o_refs = refs[:n_out], refs[n_out:]
    h = _rms(x_ref[...], g_ref[...]).astype(BF16)
    for w_ref, o_ref in zip(w_refs, o_refs):
        n = w_ref.shape[1]
        for c0 in range(0, n, col_chunk):
            c1 = min(c0 + col_chunk, n)
            o_ref[:, c0:c1] = jnp.dot(
                h, w_ref[:, c0:c1], preferred_element_type=F32).astype(o_ref.dtype)


def norm_proj(x, g, weights, out_dtypes, *, tm=512, col_chunk=512):
    t, d = x.shape
    n_out = len(weights)
    return pl.pallas_call(
        functools.partial(_norm_proj_kernel, n_out=n_out, col_chunk=col_chunk),
        out_shape=[jax.ShapeDtypeStruct((t, w.shape[1]), dt) for w, dt in zip(weights, out_dtypes)],
        grid=(t // tm,),
        in_specs=[pl.BlockSpec((tm, d), lambda i: (i, 0)),
                  pl.BlockSpec((1, d), lambda i: (0, 0))]
                 + [pl.BlockSpec(w.shape, lambda i: (0, 0)) for w in weights],
        out_specs=[pl.BlockSpec((tm, w.shape[1]), lambda i: (i, 0)) for w in weights],
        compiler_params=_params("parallel"),
        name="norm_proj",
    )(x, g.reshape(1, d), *weights)


def _pool_kernel(a_ref, f_ref, bf_ref, w_ref, scale_ref, o_ref, cum_ref, *, rows):
    s, width = a_ref.shape
    group = lax.broadcasted_iota(jnp.int32, (1, width), 1) // (width // len(POOL_WINDOWS))
    window = jnp.where(group == 0, POOL_WINDOWS[0],
                       jnp.where(group == 1, POOL_WINDOWS[1],
                                 jnp.where(group == 2, POOL_WINDOWS[2], POOL_WINDOWS[3])))
    tri = (lax.broadcasted_iota(jnp.int32, (rows, rows), 0)
           >= lax.broadcasted_iota(jnp.int32, (rows, rows), 1)).astype(F32)
    carry = jnp.zeros((1, f_ref.shape[1]), F32)
    for r in range(s // rows):
        r0 = r * rows
        if r == 0:
            ext = jnp.concatenate([jnp.zeros((POOL_HALO, width), F32), a_ref[0:rows, :]], axis=0)
        else:
            ext = a_ref[r0 - POOL_HALO:r0 + rows, :]
        s2 = ext + pltpu.roll(ext, 1, 0)
        s4 = s2 + pltpu.roll(s2, 2, 0)
        s8 = s4 + pltpu.roll(s4, 4, 0)
        s16 = s8 + pltpu.roll(s8, 8, 0)
        wsum = jnp.where(group == 0, s2, jnp.where(group == 1, s4, jnp.where(group == 2, s8, s16)))
        wsum = wsum[POOL_HALO:, :]
        a = ext[POOL_HALO:, :]
        t1 = r0 + 1 + lax.broadcasted_iota(jnp.int32, (rows, 1), 0)
        count = jnp.minimum(t1, window).astype(F32)
        pooled = wsum / count - a
        mixed = jnp.dot(pooled.astype(BF16), w_ref[...], preferred_element_type=F32)
        o_ref[r0:r0 + rows, :] = (mixed * scale_ref[...]).astype(o_ref.dtype)
        lf = _log_sigmoid(f_ref[r0:r0 + rows, :] + bf_ref[...]) * LOG2E
        cum = jnp.dot(tri, lf, preferred_element_type=F32, precision=HIGHEST) + carry
        cum_ref[r0:r0 + rows, :] = cum
        carry = cum[rows - 1:rows, :]


def pool_and_cum(a_in, f_pre, b_forget, pool_w_bd, pool_scale, *, rows=256):
    b, s, width = a_in.shape
    fw = f_pre.shape[-1]
    return pl.pallas_call(
        functools.partial(_pool_kernel, rows=rows),
        out_shape=[jax.ShapeDtypeStruct((b, s, width), BF16),
                   jax.ShapeDtypeStruct((b, s, fw), F32)],
        grid=(b,),
        in_specs=[pl.BlockSpec((None, s, width), lambda i: (i, 0, 0)),
                  pl.BlockSpec((None, s, fw), lambda i: (i, 0, 0)),
                  pl.BlockSpec((1, fw), lambda i: (0, 0)),
                  pl.BlockSpec((width, width), lambda i: (0, 0)),
                  pl.BlockSpec((1, width), lambda i: (0, 0))],
        out_specs=[pl.BlockSpec((None, s, width), lambda i: (i, 0, 0)),
                   pl.BlockSpec((None, s, fw), lambda i: (i, 0, 0))],
        compiler_params=_params("parallel"),
        name="pool_and_cum",
    )(a_in, f_pre, b_forget, pool_w_bd, pool_scale)


def _fox_kernel(q_ref, k_ref, v_ref, cum_ref, o_ref, *, tq):
    i = pl.program_id(2)
    lane = lax.broadcasted_iota(jnp.int32, (1, LANES), 1)
    first = lane < HEAD_DIM
    q = q_ref[...]
    q_heads = (jnp.where(first, q, jnp.zeros_like(q)), jnp.where(first, jnp.zeros_like(q), q))
    row = lax.broadcasted_iota(jnp.int32, (tq, tq), 0)
    col = lax.broadcasted_iota(jnp.int32, (tq, tq), 1)

    def logits(j):
        start = pl.multiple_of(j * tq, tq)
        k = k_ref[pl.ds(start, tq), :]
        return tuple(lax.dot_general(q_heads[h], k, NT_DIMS, preferred_element_type=F32)
                     for h in range(2))

    def block(j, scs, carry, masked):
        ms, accs = carry
        start = pl.multiple_of(j * tq, tq)
        v = v_ref[pl.ds(start, tq), :]
        ones = jnp.ones_like(v)
        v_heads = (jnp.where(first, v, ones), jnp.where(first, ones, v))
        ck = cum_ref[j]
        new_ms, new_accs = [], []
        for h in range(2):
            sc = scs[h] - ck[h:h + 1, :]
            if masked:
                sc = jnp.where(col <= row, sc, NEG_INF)
            m_new = jnp.maximum(ms[h], jnp.max(sc, axis=-1, keepdims=True))
            alpha = jnp.exp2(ms[h] - m_new)
            p = jnp.exp2((sc - m_new).astype(BF16))
            new_ms.append(m_new)
            new_accs.append(alpha * accs[h] + jnp.dot(p, v_heads[h], preferred_element_type=F32))
        return tuple(new_ms), tuple(new_accs)

    init = ((jnp.full((tq, 1), NEG_INF, F32),) * 2, (jnp.zeros((tq, LANES), F32),) * 2)
    def pair(t, c):
        return block(2 * t + 1, logits(2 * t + 1), block(2 * t, logits(2 * t), c, False), False)

    carry = lax.fori_loop(0, i // 2, pair, init)
    carry = lax.fori_loop(2 * (i // 2), i, lambda j, c: block(j, logits(j), c, False), carry)
    _, accs = block(i, logits(i), carry, True)
    out0 = accs[0] / accs[0][:, HEAD_DIM:HEAD_DIM + 1]
    out1 = accs[1] / accs[1][:, 0:1]
    o_ref[...] = jnp.where(first, out0, out1).astype(o_ref.dtype)


def fox_attention(qkv, cum_blocks, *, tq=512):
    b, s, w3 = qkv.shape
    n_pairs = w3 // 3 // LANES
    return pl.pallas_call(
        functools.partial(_fox_kernel, tq=tq),
        out_shape=jax.ShapeDtypeStruct((b, s, w3 // 3), BF16),
        grid=(b, n_pairs, s // tq),
        in_specs=[pl.BlockSpec((None, tq, LANES), lambda bi, p, i: (bi, i, p)),
                  pl.BlockSpec((None, s, LANES), lambda bi, p, i: (bi, 0, n_pairs + p)),
                  pl.BlockSpec((None, s, LANES), lambda bi, p, i: (bi, 0, 2 * n_pairs + p)),
                  pl.BlockSpec((None, None, s // tq, 2, tq), lambda bi, p, i: (bi, p, 0, 0, 0))],
        out_specs=pl.BlockSpec((None, tq, LANES), lambda bi, p, i: (bi, i, p)),
        compiler_params=_params("parallel", "parallel", "arbitrary"),
        name="fox_attention",
    )(qkv, qkv, qkv, cum_blocks)


def _sb_kernel(q_ref, k_ref, v_ref, o_ref, *, tq, tk):
    i = pl.program_id(2)
    per = tq // tk
    lane = lax.broadcasted_iota(jnp.int32, (1, LANES), 1)
    first = lane < HEAD_DIM
    q = q_ref[...]
    q_heads = (jnp.where(first, q, jnp.zeros_like(q)), jnp.where(first, jnp.zeros_like(q), q))
    row = lax.broadcasted_iota(jnp.int32, (tq, tk), 0)
    col = lax.broadcasted_iota(jnp.int32, (tq, tk), 1)
    neg_suffix = jnp.where(lax.broadcasted_iota(jnp.int32, (tk, tk), 0)
                           > lax.broadcasted_iota(jnp.int32, (tk, tk), 1), -1.0, 0.0).astype(BF16)

    def logits(j):
        start = pl.multiple_of(j * tk, tk)
        k = k_ref[pl.ds(start, tk), :]
        return tuple(lax.dot_general(q_heads[h], k, NT_DIMS, preferred_element_type=F32)
                     for h in range(2))

    def block(j, zs, carry, diagonal):
        used, acc = carry
        start = pl.multiple_of(j * tk, tk)
        v = v_ref[pl.ds(start, tk), :]
        if diagonal:
            valid = col + (j * tk - i * tq) < row
        new_used, pvs = [], []
        for h in range(2):
            z = zs[h]
            sp = jnp.maximum(z, jnp.log2(1.0 + jnp.exp2(jnp.minimum(z, SOFTPLUS2_CLAMP))))
            if diagonal:
                sp = jnp.where(valid, sp, 0.0)
            later_in = jnp.dot(sp.astype(BF16), neg_suffix, preferred_element_type=F32)
            w = jnp.exp2((z - sp + (later_in - used[h])).astype(BF16))
            if diagonal:
                w = jnp.where(valid, w, jnp.zeros_like(w))
            pvs.append(jnp.dot(w, v, preferred_element_type=F32))
            new_used.append(used[h] + (sp[:, 0:1] - later_in[:, 0:1]))
        return tuple(new_used), acc + jnp.where(first, pvs[0], pvs[1])

    carry = ((jnp.zeros((tq, 1), F32),) * 2, jnp.zeros((tq, LANES), F32))
    for d in range(per):
        j = (i + 1) * per - 1 - d
        carry = block(j, logits(j), carry, True)

    def pair(t, carry):
        j = i * per - 1 - 2 * t
        return block(j - 1, logits(j - 1), block(j, logits(j), carry, False), False)

    assert per % 2 == 0
    _, acc = lax.fori_loop(0, i * per // 2, pair, carry)
    o_ref[...] = acc.astype(o_ref.dtype)


def sb_attention(qkv, *, tq=512, tk=256):
    b, s, w3 = qkv.shape
    n_pairs = w3 // 3 // LANES
    return pl.pallas_call(
        functools.partial(_sb_kernel, tq=tq, tk=tk),
        out_shape=jax.ShapeDtypeStruct((b, s, w3 // 3), BF16),
        grid=(b, n_pairs, s // tq),
        in_specs=[pl.BlockSpec((None, tq, LANES), lambda bi, p, i: (bi, i, p)),
                  pl.BlockSpec((None, s, LANES), lambda bi, p, i: (bi, 0, n_pairs + p)),
                  pl.BlockSpec((None, s, LANES), lambda bi, p, i: (bi, 0, 2 * n_pairs + p))],
        out_specs=pl.BlockSpec((None, tq, LANES), lambda bi, p, i: (bi, i, p)),
        compiler_params=_params("parallel", "parallel", "arbitrary"),
        name="sb_attention",
    )(qkv, qkv, qkv)


def _conv_kernel(x_ref, halo_ref, w_ref, colscale_ref, o_ref):
    i = pl.program_id(1)
    halo = jnp.where(i == 0, 0.0, halo_ref[...])
    ext = jnp.concatenate([halo, x_ref[...]], axis=0)
    y = ext * w_ref[CONV_WIDTH - 1:CONV_WIDTH, :]
    for back in range(1, CONV_WIDTH):
        y = y + pltpu.roll(ext, back, 0) * w_ref[CONV_WIDTH - 1 - back:CONV_WIDTH - back, :]
    y = y[CONV_HALO:, :]
    o_ref[...] = (y * _sigmoid(y) * colscale_ref[...]).astype(o_ref.dtype)


def conv_silu(x, conv_w, colscale, *, rows=512):
    b, s, c = x.shape
    per = rows // CONV_HALO
    return pl.pallas_call(
        _conv_kernel,
        out_shape=jax.ShapeDtypeStruct((b, s, c), BF16),
        grid=(b, s // rows),
        in_specs=[pl.BlockSpec((None, rows, c), lambda bi, i: (bi, i, 0)),
                  pl.BlockSpec((None, CONV_HALO, c),
                               lambda bi, i: (bi, jnp.maximum(i * per - 1, 0), 0)),
                  pl.BlockSpec((CONV_WIDTH, c), lambda bi, i: (0, 0)),
                  pl.BlockSpec((1, c), lambda bi, i: (0, 0))],
        out_specs=pl.BlockSpec((None, rows, c), lambda bi, i: (bi, i, 0)),
        compiler_params=_params("parallel", "parallel"),
        name="conv_silu",
    )(x, x, conv_w, colscale)


def _mlstm_kernel(bi_ref, bf_ref, q_ref, k_ref, v_ref, vt_ref, li_ref, lf_ref, o_ref, g_ref,
                  out_ref, c_scr, n_scr, m_scr):
    head = pl.program_id(1)
    step = pl.program_id(2)
    L = q_ref.shape[0]

    @pl.when(step == 0)
    def _():
        c_scr[...] = jnp.zeros_like(c_scr)
        n_scr[...] = jnp.zeros_like(n_scr)
        m_scr[...] = jnp.zeros_like(m_scr)

    r_idx = lax.broadcasted_iota(jnp.int32, (L, L), 0)
    c_idx = lax.broadcasted_iota(jnp.int32, (L, L), 1)
    lower = r_idx >= c_idx
    tri_u = (r_idx <= c_idx).astype(F32)

    q = q_ref[...]
    k = k_ref[...]
    c_st = c_scr[...]
    n_st = n_scr[...]
    m_st = m_scr[...]
    lf_row = _log_sigmoid(lf_ref[...] + bf_ref[head])
    li_row = li_ref[...] + bi_ref[head]
    bcum_row = jnp.dot(jnp.broadcast_to(lf_row, (8, L)), tri_u, preferred_element_type=F32,
                       precision=HIGHEST)[0:1, :]
    bcum_col = jnp.broadcast_to(bcum_row, (LANES, L)).T[:, 0:1]
    g = jnp.sum(lf_row, axis=-1, keepdims=True)

    log_d = jnp.where(lower, bcum_col - bcum_row + li_row, NEG_INF)
    inter = bcum_col + m_st
    m_out = jnp.maximum(inter, jnp.max(log_d, axis=-1, keepdims=True))
    d_mat = jnp.exp(log_d - m_out)
    inter_w = jnp.exp(inter - m_out)
    qk = lax.dot_general(q, k, NT_DIMS, preferred_element_type=F32) * d_mat
    num = (jnp.dot(qk.astype(BF16), v_ref[...], preferred_element_type=F32)
           + inter_w * lax.dot_general(q, c_st.astype(BF16), NT_DIMS, preferred_element_type=F32))
    den = (jnp.sum(qk, axis=-1, keepdims=True)
           + inter_w * jnp.sum(q.astype(F32) * n_st, axis=-1, keepdims=True))
    hh = num / jnp.maximum(jnp.abs(den), jnp.exp(-m_out))
    hn = _rms(hh, g_ref[...])
    out_ref[...] = (hn * _sigmoid(o_ref[...])).astype(out_ref.dtype)

    a_row = g - bcum_row + li_row
    m_loc = jnp.max(a_row, axis=-1, keepdims=True)
    w_row = jnp.exp(a_row - m_loc)
    c_loc = jnp.dot((vt_ref[...].astype(F32) * w_row).astype(BF16), k, preferred_element_type=F32)
    n_loc = jnp.dot(jnp.broadcast_to(w_row, (8, L)).astype(BF16), k,
                    preferred_element_type=F32)[0:1, :]
    m_new = jnp.maximum(g + m_st, m_loc)
    decay = jnp.exp(g + m_st - m_new)
    fresh = jnp.exp(m_loc - m_new)
    c_scr[...] = decay * c_st + fresh * c_loc
    n_scr[...] = decay * n_st + fresh * n_loc
    m_scr[...] = m_new


def mlstm(qk, v, v_t, li, lf, o_gate, b_i, b_f, head_g):
    b, s, w = v.shape
    n_heads = w // LANES
    rows = v_t.shape[-1]
    grid_spec = pltpu.PrefetchScalarGridSpec(
        num_scalar_prefetch=2,
        grid=(b, n_heads, s // rows),
        in_specs=[pl.BlockSpec((None, rows, LANES), lambda bi, h, c, *_: (bi, c, h)),
                  pl.BlockSpec((None, rows, LANES), lambda bi, h, c, *_: (bi, c, n_heads + h)),
                  pl.BlockSpec((None, rows, LANES), lambda bi, h, c, *_: (bi, c, h)),
                  pl.BlockSpec((None, None, None, LANES, rows),
                               lambda bi, h, c, *_: (bi, h, c, 0, 0)),
                  pl.BlockSpec((None, None, None, 1, rows), lambda bi, h, c, *_: (bi, h, c, 0, 0)),
                  pl.BlockSpec((None, None, None, 1, rows), lambda bi, h, c, *_: (bi, h, c, 0, 0)),
                  pl.BlockSpec((None, rows, LANES), lambda bi, h, c, *_: (bi, c, h)),
                  pl.BlockSpec((None, 1, LANES), lambda bi, h, c, *_: (h, 0, 0))],
        out_specs=pl.BlockSpec((None, rows, LANES), lambda bi, h, c, *_: (bi, c, h)),
        scratch_shapes=[pltpu.VMEM((LANES, LANES), F32), pltpu.VMEM((1, LANES), F32),
                        pltpu.VMEM((1, 1), F32)])
    return pl.pallas_call(
        _mlstm_kernel,
        out_shape=jax.ShapeDtypeStruct((b, s, w), BF16),
        grid_spec=grid_spec,
        compiler_params=_params("parallel", "parallel", "arbitrary"),
        name="mlstm",
    )(b_i, b_f, qk, qk, v, v_t, li, lf, o_gate, head_g.reshape(n_heads, 1, LANES))


def _out_proj_kernel(x_ref, a_ref, b_ref, w_ref, o_ref):
    na = a_ref.shape[1]
    acc = jnp.dot(a_ref[...], w_ref[0:na, :], preferred_element_type=F32)
    acc = acc + jnp.dot(b_ref[...], w_ref[na:, :], preferred_element_type=F32)
    o_ref[...] = x_ref[...] + acc


def out_proj_residual(x, a, b_part, w, *, tm=512):
    t, d = x.shape
    return pl.pallas_call(
        _out_proj_kernel,
        out_shape=jax.ShapeDtypeStruct((t, d), F32),
        grid=(t // tm,),
        in_specs=[pl.BlockSpec((tm, d), lambda i: (i, 0)),
                  pl.BlockSpec((tm, a.shape[1]), lambda i: (i, 0)),
                  pl.BlockSpec((tm, b_part.shape[1]), lambda i: (i, 0)),
                  pl.BlockSpec(w.shape, lambda i: (0, 0))],
        out_specs=pl.BlockSpec((tm, d), lambda i: (i, 0)),
        compiler_params=_params("parallel"),
        name="out_proj_residual",
    )(x, a, b_part, w)


def _ff_chunks(tf):
    return [(c0, min(c0 + FF_CHUNK, tf)) for c0 in range(0, tf, FF_CHUNK)]


def _swiglu_accumulate(h_scr, wg_ref, wu_ref, wd_ref, act_scr, o_ref, after_chunk=None):
    h = h_scr[...]
    for c, (c0, c1) in enumerate(_ff_chunks(wg_ref.shape[1])):
        gate = jnp.dot(h, wg_ref[:, c0:c1], preferred_element_type=F32)
        up = jnp.dot(h, wu_ref[:, c0:c1], preferred_element_type=F32)
        act_scr[:, c0:c1] = (gate * _sigmoid(gate) * up).astype(BF16)
        if after_chunk is not None:
            after_chunk(c)
    o_ref[...] += jnp.dot(act_scr[...], wd_ref[...], preferred_element_type=F32)


def _ffn_kernel(x_ref, g_ref, wg_ref, wu_ref, wd_ref, o_ref, h_scr, act_scr):
    @pl.when(pl.program_id(1) == 0)
    def _():
        x = x_ref[...]
        h_scr[...] = _rms(x, g_ref[...]).astype(BF16)
        o_ref[...] = x

    _swiglu_accumulate(h_scr, wg_ref, wu_ref, wd_ref, act_scr, o_ref)


def ffn_residual(x, g, w_gate, w_up, w_down, layer, *, tm=1024, tf=1792):
    t, d = x.shape
    ff = w_gate.shape[2]
    return pl.pallas_call(
        _ffn_kernel,
        out_shape=jax.ShapeDtypeStruct((t, d), F32),
        grid=(t // tm, ff // tf),
        in_specs=[pl.BlockSpec((tm, d), lambda i, j: (i, 0)),
                  pl.BlockSpec((1, d), lambda i, j: (0, 0)),
                  pl.BlockSpec((None, d, tf), lambda i, j: (layer, 0, j)),
                  pl.BlockSpec((None, d, tf), lambda i, j: (layer, 0, j)),
                  pl.BlockSpec((None, tf, d), lambda i, j: (layer, j, 0))],
        out_specs=pl.BlockSpec((tm, d), lambda i, j: (i, 0)),
        scratch_shapes=[pltpu.VMEM((tm, d), BF16), pltpu.VMEM((tm, tf), BF16)],
        compiler_params=_params("parallel", "arbitrary"),
        name="ffn_residual",
    )(x, g.reshape(1, d), w_gate, w_up, w_down)


def _store_row_tiles(ref, x):
    rows, d = x.shape
    per = d // LANES
    for s in range(per):
        ref[pl.ds(s, rows, stride=per), :] = x[:, s * LANES:(s + 1) * LANES]


def _load_row_tiles(ref, rows, per):
    return [ref[pl.ds(s, rows, stride=per), :] for s in range(per)]


def _router_kernel(x_ref, g_ref, wr_ref, h_ref, route_ref):
    h = _rms(x_ref[...], g_ref[...])
    _store_row_tiles(h_ref, h)
    logits = jnp.dot(h, wr_ref[...], preferred_element_type=F32, precision=HIGHEST)
    lane = lax.broadcasted_iota(jnp.int32, logits.shape, 1)
    logits = jnp.where(lane < N_EXPERTS, logits, NEG_INF)
    m1 = jnp.max(logits, axis=-1, keepdims=True)
    i1 = jnp.min(jnp.where(logits == m1, lane, LANES), axis=-1, keepdims=True)
    rest = jnp.where(lane == i1, NEG_INF, logits)
    m2 = jnp.max(rest, axis=-1, keepdims=True)
    i2 = jnp.min(jnp.where(rest == m2, lane, LANES), axis=-1, keepdims=True)
    e2 = jnp.exp(m2 - m1)
    g1 = 1.0 / (1.0 + e2)
    g2 = e2 / (1.0 + e2)
    route_ref[...] = jnp.where(lane == 0, i1.astype(F32),
                               jnp.where(lane == 1, i2.astype(F32),
                                         jnp.where(lane == 2, g1, jnp.where(lane == 3, g2, 0.0))))


def router(x, g, w_router_padded, *, tm=512):
    t, d = x.shape
    per = d // LANES
    return pl.pallas_call(
        _router_kernel,
        out_shape=[jax.ShapeDtypeStruct((t * per, LANES), F32),
                   jax.ShapeDtypeStruct((t, LANES), F32)],
        grid=(t // tm,),
        in_specs=[pl.BlockSpec((tm, d), lambda i: (i, 0)),
                  pl.BlockSpec((1, d), lambda i: (0, 0)),
                  pl.BlockSpec((d, LANES), lambda i: (0, 0))],
        out_specs=[pl.BlockSpec((tm * per, LANES), lambda i: (i, 0)),
                   pl.BlockSpec((tm, LANES), lambda i: (i, 0))],
        compiler_params=_params("parallel"),
        name="router",
    )(x, g.reshape(1, d), w_router_padded)


def _expert_kernel(be_ref, nused_ref, tok_cur_ref, tok_next_ref, h_hbm, wg_ref, wu_ref, wd_ref,
                   o_ref, xbuf, sems, h_scr, act_scr, acc_scr, *, tm):
    i = pl.program_id(0)
    j = pl.program_id(1)
    n_used = nused_ref[0]
    used = i < n_used
    per = h_scr.shape[1] // LANES

    def issue_gather(tok_ref, slot):
        def body(rr, _):
            for prio in range(DMA_PRIORITIES):
                r = rr * DMA_PRIORITIES + prio
                src = pl.multiple_of(tok_ref[0, r] * per, per)
                dst = pl.multiple_of(r * per, per)
                pltpu.make_async_copy(h_hbm.at[pl.ds(src, per)], xbuf.at[slot, pl.ds(dst, per)],
                                      sems.at[slot]).start(priority=prio)
            return 0
        lax.fori_loop(0, tm // DMA_PRIORITIES, body, 0, unroll=4)

    @pl.when(jnp.logical_and(i == 0, j == 0))
    def _():
        issue_gather(tok_cur_ref, 0)

    @pl.when(jnp.logical_and(used, j == 0))
    def _():
        slot = i % 2
        pltpu.make_async_copy(xbuf.at[slot], xbuf.at[slot], sems.at[slot]).wait()
        for s, piece in enumerate(_load_row_tiles(xbuf.at[slot], tm, per)):
            h_scr[:, s * LANES:(s + 1) * LANES] = piece.astype(BF16)
        acc_scr[...] = jnp.zeros_like(acc_scr)

    @pl.when(used)
    def _():
        _swiglu_accumulate(h_scr, wg_ref, wu_ref, wd_ref, act_scr, acc_scr)

    @pl.when(jnp.logical_and(i + 1 < n_used, j == 0))
    def _():
        issue_gather(tok_next_ref, (i + 1) % 2)

    @pl.when(jnp.logical_and(used, j == pl.num_programs(1) - 1))
    def _():
        _store_row_tiles(o_ref, acc_scr[...])

    @pl.when(jnp.logical_and(jnp.logical_not(used), j == 0))
    def _():
        o_ref[...] = jnp.zeros_like(o_ref)


def expert_ffn(h, row_tok, block_expert, n_used, w_gate, w_up, w_down, layer, *, tm, tf=1792):
    n_rows = row_tok.shape[0]
    d = w_gate.shape[2]
    per = d // LANES
    ff = w_gate.shape[3]
    n_blocks = n_rows // tm
    last_j = ff // tf - 1

    def up_block(i, j, be, nu):
        return (layer, be[jnp.minimum(i, nu[0] - 1)], 0, jnp.where(i < nu[0], j, last_j))

    def down_block(i, j, be, nu):
        return (layer, be[jnp.minimum(i, nu[0] - 1)], jnp.where(i < nu[0], j, last_j), 0)

    grid_spec = pltpu.PrefetchScalarGridSpec(
        num_scalar_prefetch=2,
        grid=(n_blocks, ff // tf),
        in_specs=[pl.BlockSpec((None, 1, tm), lambda i, j, be, nu: (i, 0, 0),
                               memory_space=pltpu.SMEM),
                  pl.BlockSpec((None, 1, tm),
                               lambda i, j, be, nu: (jnp.minimum(i + 1, n_blocks - 1), 0, 0),
                               memory_space=pltpu.SMEM),
                  pl.BlockSpec(memory_space=pl.ANY),
                  pl.BlockSpec((None, None, d, tf), up_block),
                  pl.BlockSpec((None, None, d, tf), up_block),
                  pl.BlockSpec((None, None, tf, d), down_block)],
        out_specs=pl.BlockSpec((tm * per, LANES), lambda i, j, be, nu: (i, 0)),
        scratch_shapes=[pltpu.VMEM((2, tm * per, LANES), F32), pltpu.SemaphoreType.DMA((2,)),
                        pltpu.VMEM((tm, d), BF16), pltpu.VMEM((tm, tf), BF16),
                        pltpu.VMEM((tm, d), F32)])
    tok_blocks = row_tok.reshape(n_blocks, 1, tm)
    return pl.pallas_call(
        functools.partial(_expert_kernel, tm=tm),
        out_shape=jax.ShapeDtypeStruct((n_rows * per, LANES), F32),
        grid_spec=grid_spec,
        compiler_params=_params("arbitrary", "arbitrary"),
        name="expert_ffn",
    )(block_expert, n_used, tok_blocks, tok_blocks, h, w_gate, w_up, w_down)


def _combine_kernel(pos_cur_ref, pos_next_ref, x_ref, route_ref, y_ref, *rest, rows, final):
    if final:
        g_ref, o_ref, buf, sems = rest
    else:
        o_ref, buf, sems = rest
    i = pl.program_id(0)
    n = pl.num_programs(0)
    per = x_ref.shape[1] // LANES

    def issue(pos_ref, slot):
        def body(r, _):
            dst = pl.multiple_of(r * per, per)
            for kk in range(TOP_K):
                src = pl.multiple_of(pos_ref[0, TOP_K * r + kk] * per, per)
                pltpu.make_async_copy(y_ref.at[pl.ds(src, per)], buf.at[slot, kk, pl.ds(dst, per)],
                                      sems.at[slot]).start(priority=kk % DMA_PRIORITIES)
            return 0
        lax.fori_loop(0, rows, body, 0, unroll=4)

    @pl.when(i == 0)
    def _():
        issue(pos_cur_ref, 0)

    @pl.when(i + 1 < n)
    def _():
        issue(pos_next_ref, (i + 1) % 2)

    slot = i % 2
    pltpu.make_async_copy(buf.at[slot], buf.at[slot], sems.at[slot]).wait()
    route = route_ref[...]
    g1 = route[:, 2:3]
    g2 = route[:, 3:4]
    y1 = jnp.concatenate(_load_row_tiles(buf.at[slot, 0], rows, per), axis=-1)
    y2 = jnp.concatenate(_load_row_tiles(buf.at[slot, 1], rows, per), axis=-1)
    out = x_ref[...] + g1 * y1 + g2 * y2
    o_ref[...] = _rms(out, g_ref[...]) if final else out


def combine_residual(x, route, y_rows, pos, final_g=None, *, rows=256):
    t, d = x.shape
    final = final_g is not None
    n_steps = t // rows
    grid_spec = pltpu.PrefetchScalarGridSpec(
        num_scalar_prefetch=0,
        grid=(n_steps,),
        in_specs=[pl.BlockSpec((None, 1, TOP_K * rows), lambda i: (i, 0, 0),
                               memory_space=pltpu.SMEM),
                  pl.BlockSpec((None, 1, TOP_K * rows),
                               lambda i: (jnp.minimum(i + 1, n_steps - 1), 0, 0),
                               memory_space=pltpu.SMEM),
                  pl.BlockSpec((rows, d), lambda i: (i, 0)),
                  pl.BlockSpec((rows, LANES), lambda i: (i, 0)),
                  pl.BlockSpec(memory_space=pl.ANY)]
                 + ([pl.BlockSpec((1, d), lambda i: (0, 0))] if final else []),
        out_specs=pl.BlockSpec((rows, d), lambda i: (i, 0)),
        scratch_shapes=[pltpu.VMEM((2, TOP_K, rows * (d // LANES), LANES), F32),
                        pltpu.SemaphoreType.DMA((2,))])
    pos_blocks = pos.reshape(n_steps, 1, TOP_K * rows)
    extra = (final_g.reshape(1, d),) if final else ()
    return pl.pallas_call(
        functools.partial(_combine_kernel, rows=rows, final=final),
        out_shape=jax.ShapeDtypeStruct((t, d), F32),
        grid_spec=grid_spec,
        compiler_params=_params("arbitrary"),
        name="combine_residual",
    )(pos_blocks, pos_blocks, x, route, y_rows, *extra)


def _pad_cols(w, n):
    return jnp.pad(w, ((0, 0), (0, n - w.shape[1])))


def _scale_q_cols(w_qkv, width):
    scale = jnp.concatenate([jnp.full((width,), LOG2E * HEAD_DIM ** -0.5, F32),
                             jnp.ones((2 * width,), F32)])
    return w_qkv * scale


def even_layer(x, b, s, g_mix, g_ffn, w_in, b_forget, pool_w, pool_scale, w_out,
               w_gate, w_up, w_down, e, *, fox_tq=512):
    t, d = x.shape
    pool_width = pool_w.shape[0] * pool_w.shape[1]
    n_heads = b_forget.shape[0]
    fox_width = n_heads * HEAD_DIM
    w_a = w_in[:, :pool_width].astype(BF16)
    w_qkv = _scale_q_cols(w_in[:, pool_width:pool_width + 3 * fox_width], fox_width).astype(BF16)
    w_f = _pad_cols(w_in[:, pool_width + 3 * fox_width:], LANES).astype(BF16)
    qkv, a_in, f_pre = norm_proj(x, g_mix, [w_qkv, w_a, w_f], [BF16, F32, F32])

    pool_bd = jax.scipy.linalg.block_diag(*[pool_w[i] for i in range(pool_w.shape[0])]).astype(BF16)
    a_out, cum = pool_and_cum(a_in.reshape(b, s, pool_width), f_pre.reshape(b, s, LANES),
                              _pad_cols(b_forget.reshape(1, n_heads), LANES), pool_bd,
                              pool_scale.reshape(1, pool_width))
    cum_blocks = cum[:, :, :n_heads].reshape(b, s // fox_tq, fox_tq, n_heads // 2, 2)
    cum_blocks = cum_blocks.transpose(0, 3, 1, 4, 2)
    attn = fox_attention(qkv.reshape(b, s, 3 * fox_width), cum_blocks, tq=fox_tq)
    x = out_proj_residual(x, a_out.reshape(t, pool_width), attn.reshape(t, fox_width),
                          w_out.astype(BF16))
    return ffn_residual(x, g_ffn, w_gate, w_up, w_down, e)


def odd_layer(x, b, s, g_mix, g_ffn, w_in, b_igate, b_fgate, conv_w, head_norm_g, w_out,
              w_router, w_gate, w_up, w_down, e, final_g=None):
    x = odd_mixer_block(x, b, s, g_mix, w_in, b_igate, b_fgate, conv_w, head_norm_g, w_out)
    return moe_block(x, g_ffn, w_router, w_gate, w_up, w_down, e, final_g)


def odd_mixer_block(x, b, s, g_mix, w_in, b_igate, b_fgate, conv_w, head_norm_g, w_out):
    t, d = x.shape
    n_ml = b_igate.shape[0]
    ml_width = n_ml * LANES
    sb_width = d - ml_width
    cols = [3 * sb_width, 2 * ml_width, ml_width, ml_width]
    edges = [0]
    for c in cols:
        edges.append(edges[-1] + c)
    w_sb, w_mqk, w_mv, w_mo = [w_in[:, edges[i]:edges[i + 1]] for i in range(4)]
    w_sb = _scale_q_cols(w_sb, sb_width)
    w_sb, w_mqk, w_mv, w_mo = [w.astype(BF16) for w in (w_sb, w_mqk, w_mv, w_mo)]
    w_gates = _pad_cols(w_in[:, edges[-1]:], LANES).astype(BF16)
    sqkv, ml_qk, ml_v, ml_o, gates = norm_proj(
        x, g_mix, [w_sb, w_mqk, w_mv, w_mo, w_gates], [BF16, F32, BF16, F32, F32])

    c_out = sb_attention(sqkv.reshape(b, s, 3 * sb_width))

    k_scale = jnp.concatenate([jnp.ones((ml_width,), F32),
                               jnp.full((ml_width,), LANES ** -0.5, F32)]).reshape(1, 2 * ml_width)
    qk = conv_silu(ml_qk.reshape(b, s, 2 * ml_width), conv_w, k_scale)
    nb = s // MLSTM_BLOCK
    v3 = ml_v.reshape(b, s, ml_width)
    v_t = v3.reshape(b, nb, MLSTM_BLOCK, n_ml, LANES).transpose(0, 3, 1, 4, 2)
    gate_rows = gates[:, :2 * n_ml].reshape(b, nb, MLSTM_BLOCK, 2 * n_ml).transpose(0, 3, 1, 2)
    gate_rows = gate_rows.reshape(b, 2 * n_ml, nb, 1, MLSTM_BLOCK)
    d_out = mlstm(qk, v3, v_t, gate_rows[:, :n_ml], gate_rows[:, n_ml:],
                  ml_o.reshape(b, s, ml_width), b_igate, b_fgate, head_norm_g)
    return out_proj_residual(x, c_out.reshape(t, sb_width), d_out.reshape(t, ml_width),
                             w_out.astype(BF16))


def moe_block(x, g_ffn, w_router, w_gate, w_up, w_down, e, final_g=None, *, expert_rows=1024):
    t, d = x.shape
    h, route = router(x, g_ffn, _pad_cols(w_router, LANES))
    flat_e = route[:, :TOP_K].astype(jnp.int32).reshape(-1)
    n_assign = t * TOP_K
    onehot = jax.nn.one_hot(flat_e, N_EXPERTS, dtype=jnp.int32)
    rank = jnp.sum((jnp.cumsum(onehot, axis=0) - onehot) * onehot, axis=-1)
    counts = jnp.sum(onehot, axis=0)
    padded = (counts + expert_rows - 1) // expert_rows * expert_rows
    group_end = jnp.cumsum(padded)
    pos = (group_end[flat_e] - padded[flat_e] + rank).astype(jnp.int32)
    n_rows = n_assign + N_EXPERTS * expert_rows
    n_blocks = n_rows // expert_rows
    flat_tok = jnp.arange(n_assign, dtype=jnp.int32) // TOP_K
    row_tok = jnp.zeros((n_rows,), jnp.int32).at[pos].set(flat_tok)
    block_expert = jnp.minimum(
        jnp.searchsorted(group_end, jnp.arange(n_blocks) * expert_rows, side='right'),
        N_EXPERTS - 1).astype(jnp.int32)
    n_used = (group_end[-1:] // expert_rows).astype(jnp.int32)

    y_rows = expert_ffn(h, row_tok, block_expert, n_used, w_gate, w_up, w_down, e, tm=expert_rows)
    return combine_residual(x, route, y_rows, pos, final_g)


def kernel(x, norm_mix_g, norm_ffn_g, norm_final_g, ev_w_in, ev_b_forget, ev_pool_w, ev_pool_scale, ev_w_out, ffn_w_gate, ffn_w_up, ffn_w_down, od_w_in, od_b_igate, od_b_fgate, od_conv_w, od_head_norm_g, od_w_out, moe_w_router, moe_w_gate, moe_w_up, moe_w_down):
    b, s, d = x.shape
    depth = norm_mix_g.shape[0]
    xt = x.reshape(b * s, d)
    ffn_w = [w.astype(BF16) for w in (ffn_w_gate, ffn_w_up, ffn_w_down)]
    moe_w = [w.astype(BF16) for w in (moe_w_gate, moe_w_up, moe_w_down)]
    for layer in range(depth):
        e = layer // 2
        if layer % 2 == 0:
            xt = even_layer(xt, b, s, norm_mix_g[layer], norm_ffn_g[layer], ev_w_in[e],
                            ev_b_forget[e], ev_pool_w[e], ev_pool_scale[e], ev_w_out[e],
                            *ffn_w, e)
        else:
            xt = odd_layer(xt, b, s, norm_mix_g[layer], norm_ffn_g[layer], od_w_in[e],
                           od_b_igate[e], od_b_fgate[e], od_conv_w[e], od_head_norm_g[e],
                           od_w_out[e], moe_w_router[e], *moe_w, e,
                           norm_final_g if layer == depth - 1 else None)
    assert depth % 2 == 0
    return xt.reshape(b, s, d)
```

```python
import functools

import jax
import jax.numpy as jnp
from jax import lax
from jax.experimental import pallas as pl
from jax.experimental.pallas import tpu as pltpu

F32 = jnp.float32
BF16 = jnp.bfloat16
EPS = 1e-6
NEG_INF = float("-inf")

LANES = 128
HEAD_DIM = 64
POOL_WINDOWS = (2, 4, 8, 16)
POOL_HALO = 16
CONV_WIDTH = 4
CONV_HALO = 8
FF_CHUNK = 512
MLSTM_BLOCK = 512
FOX_BLOCKS_PER_TRIP = 3
SB_BLOCKS_PER_TRIP = 4
DMA_PRIORITIES = 2
N_EXPERTS = 8
TOP_K = 2
VMEM_LIMIT = 56 * 1024 * 1024

LOG2E = 1.4426950408889634
SOFTPLUS2_CLAMP = 64.0
HIGHEST = lax.Precision.HIGHEST
NT_DIMS = (((1,), (1,)), ((), ()))


def _params(*semantics):
    return pltpu.CompilerParams(dimension_semantics=semantics, vmem_limit_bytes=VMEM_LIMIT)


def _rms(x, g):
    return x * lax.rsqrt(jnp.mean(x * x, axis=-1, keepdims=True) + EPS) * g


def _log_sigmoid(x):
    return jnp.minimum(x, 0.0) - jnp.log(1.0 + jnp.exp(-jnp.abs(x)))


def _sigmoid(x):
    return 1.0 / (1.0 + jnp.exp(-x))


def _norm_proj_kernel(x_ref, g_ref, *refs, n_out, col_chunk):
    w_refs, o_refs = refs[:n_out], refs[n_out:]
    h = _rms(x_ref[...], g_ref[...]).astype(BF16)
    for w_ref, o_ref in zip(w_refs, o_refs):
        n = w_ref.shape[1]
        for c0 in range(0, n, col_chunk):
            c1 = min(c0 + col_chunk, n)
            o_ref[:, c0:c1] = jnp.dot(
                h, w_ref[:, c0:c1], preferred_element_type=F32).astype(o_ref.dtype)


def norm_proj(x, g, weights, out_dtypes, *, tm=512, col_chunk=512):
    t, d = x.shape
    n_out = len(weights)
    return pl.pallas_call(
        functools.partial(_norm_proj_kernel, n_out=n_out, col_chunk=col_chunk),
        out_shape=[jax.ShapeDtypeStruct((t, w.shape[1]), dt) for w, dt in zip(weights, out_dtypes)],
        grid=(t // tm,),
        in_specs=[pl.BlockSpec((tm, d), lambda i: (i, 0)),
                  pl.BlockSpec((1, d), lambda i: (0, 0))]
                 + [pl.BlockSpec(w.shape, lambda i: (0, 0)) for w in weights],
        out_specs=[pl.BlockSpec((tm, w.shape[1]), lambda i: (i, 0)) for w in weights],
        compiler_params=_params("parallel"),
        name="norm_proj",
    )(x, g.reshape(1, d), *weights)


def _pool_kernel(a_ref, f_ref, bf_ref, w_ref, scale_ref, o_ref, cum_ref, *, rows):
    s, width = a_ref.shape
    group = lax.broadcasted_iota(jnp.int32, (1, width), 1) // (width // len(POOL_WINDOWS))
    window = jnp.where(group == 0, POOL_WINDOWS[0],
                       jnp.where(group == 1, POOL_WINDOWS[1],
                                 jnp.where(group == 2, POOL_WINDOWS[2], POOL_WINDOWS[3])))
    tri = (lax.broadcasted_iota(jnp.int32, (rows, rows), 0)
           >= lax.broadcasted_iota(jnp.int32, (rows, rows), 1)).astype(F32)
    carry = jnp.zeros((1, f_ref.shape[1]), F32)
    for r in range(s // rows):
        r0 = r * rows
        if r == 0:
            ext = jnp.concatenate([jnp.zeros((POOL_HALO, width), F32), a_ref[0:rows, :]], axis=0)
        else:
            ext = a_ref[r0 - POOL_HALO:r0 + rows, :]
        s2 = ext + pltpu.roll(ext, 1, 0)
        s4 = s2 + pltpu.roll(s2, 2, 0)
        s8 = s4 + pltpu.roll(s4, 4, 0)
        s16 = s8 + pltpu.roll(s8, 8, 0)
        wsum = jnp.where(group == 0, s2, jnp.where(group == 1, s4, jnp.where(group == 2, s8, s16)))
        wsum = wsum[POOL_HALO:, :]
        a = ext[POOL_HALO:, :]
        t1 = r0 + 1 + lax.broadcasted_iota(jnp.int32, (rows, 1), 0)
        count = jnp.minimum(t1, window).astype(F32)
        pooled = wsum / count - a
        mixed = jnp.dot(pooled.astype(BF16), w_ref[...], preferred_element_type=F32)
        o_ref[r0:r0 + rows, :] = (mixed * scale_ref[...]).astype(o_ref.dtype)
        lf = _log_sigmoid(f_ref[r0:r0 + rows, :] + bf_ref[...]) * LOG2E
        cum = jnp.dot(tri, lf, preferred_element_type=F32, precision=HIGHEST) + carry
        cum_ref[r0:r0 + rows, :] = cum
        carry = cum[rows - 1:rows, :]


def pool_and_cum(a_in, f_pre, b_forget, pool_w_bd, pool_scale, *, rows=256):
    b, s, width = a_in.shape
    fw = f_pre.shape[-1]
    return pl.pallas_call(
        functools.partial(_pool_kernel, rows=rows),
        out_shape=[jax.ShapeDtypeStruct((b, s, width), BF16),
                   jax.ShapeDtypeStruct((b, s, fw), F32)],
        grid=(b,),
        in_specs=[pl.BlockSpec((None, s, width), lambda i: (i, 0, 0)),
                  pl.BlockSpec((None, s, fw), lambda i: (i, 0, 0)),
                  pl.BlockSpec((1, fw), lambda i: (0, 0)),
                  pl.BlockSpec((width, width), lambda i: (0, 0)),
                  pl.BlockSpec((1, width), lambda i: (0, 0))],
        out_specs=[pl.BlockSpec((None, s, width), lambda i: (i, 0, 0)),
                   pl.BlockSpec((None, s, fw), lambda i: (i, 0, 0))],
        compiler_params=_params("parallel"),
        name="pool_and_cum",
    )(a_in, f_pre, b_forget, pool_w_bd, pool_scale)


def _fox_kernel(q_ref, k_ref, v_ref, cum_ref, o_ref, *, tq):
    i = pl.program_id(2)
    lane = lax.broadcasted_iota(jnp.int32, (1, LANES), 1)
    first = lane < HEAD_DIM
    q = q_ref[...]
    q_heads = (jnp.where(first, q, jnp.zeros_like(q)), jnp.where(first, jnp.zeros_like(q), q))
    row = lax.broadcasted_iota(jnp.int32, (tq, tq), 0)
    col = lax.broadcasted_iota(jnp.int32, (tq, tq), 1)

    def block(j, carry, masked):
        ms, accs = carry
        start = pl.multiple_of(j * tq, tq)
        k = k_ref[pl.ds(start, tq), :]
        v = v_ref[pl.ds(start, tq), :]
        ones = jnp.ones_like(v)
        v_heads = (jnp.where(first, v, ones), jnp.where(first, ones, v))
        ck = cum_ref[j]
        new_ms, new_accs = [], []
        scs = [lax.dot_general(q_heads[h], k, NT_DIMS, preferred_element_type=F32)
               for h in range(2)]
        for h in range(2):
            sc = scs[h] - ck[h:h + 1, :]
            if masked:
                sc = jnp.where(col <= row, sc, NEG_INF)
            m_new = jnp.maximum(ms[h], jnp.max(sc, axis=-1, keepdims=True))
            alpha = jnp.exp2(ms[h] - m_new)
            p = jnp.exp2((sc - m_new).astype(BF16))
            new_ms.append(m_new)
            new_accs.append(alpha * accs[h] + jnp.dot(p, v_heads[h], preferred_element_type=F32))
        return tuple(new_ms), tuple(new_accs)

    init = ((jnp.full((tq, 1), NEG_INF, F32),) * 2, (jnp.zeros((tq, LANES), F32),) * 2)
    def run(first_block, count, c):
        for d in range(count):
            c = block(first_block + d, c, False)
        return c

    n_trips = i // FOX_BLOCKS_PER_TRIP
    carry = lax.fori_loop(
        0, n_trips, lambda t, c: run(FOX_BLOCKS_PER_TRIP * t, FOX_BLOCKS_PER_TRIP, c), init)
    def tail(n_left):
        return lambda c: block(i, run(i - n_left, n_left, c), True)

    left = i - FOX_BLOCKS_PER_TRIP * n_trips
    _, accs = lax.switch(left, [tail(n) for n in range(FOX_BLOCKS_PER_TRIP)], carry)
    out0 = accs[0] / accs[0][:, HEAD_DIM:HEAD_DIM + 1]
    out1 = accs[1] / accs[1][:, 0:1]
    o_ref[...] = jnp.where(first, out0, out1).astype(o_ref.dtype)


def fox_attention(qkv, cum_blocks, *, tq=512):
    b, s, w3 = qkv.shape
    n_pairs = w3 // 3 // LANES
    return pl.pallas_call(
        functools.partial(_fox_kernel, tq=tq),
        out_shape=jax.ShapeDtypeStruct((b, s, w3 // 3), BF16),
        grid=(b, n_pairs, s // tq),
        in_specs=[pl.BlockSpec((None, tq, LANES), lambda bi, p, i: (bi, i, p)),
                  pl.BlockSpec((None, s, LANES), lambda bi, p, i: (bi, 0, n_pairs + p)),
                  pl.BlockSpec((None, s, LANES), lambda bi, p, i: (bi, 0, 2 * n_pairs + p)),
                  pl.BlockSpec((None, None, s // tq, 2, tq), lambda bi, p, i: (bi, p, 0, 0, 0))],
        out_specs=pl.BlockSpec((None, tq, LANES), lambda bi, p, i: (bi, i, p)),
        compiler_params=_params("parallel", "parallel", "arbitrary"),
        name="fox_attention",
    )(qkv, qkv, qkv, cum_blocks)


def _sb_kernel(q_ref, k_ref, v_ref, o_ref, *, tq, tk):
    i = pl.program_id(2)
    per = tq // tk
    lane = lax.broadcasted_iota(jnp.int32, (1, LANES), 1)
    first = lane < HEAD_DIM
    q = q_ref[...]
    q_heads = (jnp.where(first, q, jnp.zeros_like(q)), jnp.where(first, jnp.zeros_like(q), q))
    row = lax.broadcasted_iota(jnp.int32, (tq, tk), 0)
    col = lax.broadcasted_iota(jnp.int32, (tq, tk), 1)
    neg_suffix = jnp.where(lax.broadcasted_iota(jnp.int32, (tk, tk), 0)
                           > lax.broadcasted_iota(jnp.int32, (tk, tk), 1), -1.0, 0.0).astype(BF16)

    def logits(j, r0=0):
        start = pl.multiple_of(j * tk, tk)
        k = k_ref[pl.ds(start, tk), :]
        return tuple(lax.dot_general(q_heads[h][r0:], k, NT_DIMS, preferred_element_type=F32)
                     for h in range(2))

    def block(j, zs, carry, diagonal, r0=0):
        used, acc = carry
        start = pl.multiple_of(j * tk, tk)
        v = v_ref[pl.ds(start, tk), :]
        if diagonal:
            valid = (col < row)[:tq - r0]
        new_used, pvs = [], []
        for h in range(2):
            z = zs[h]
            sp = jnp.maximum(z, jnp.log2(1.0 + jnp.exp2(jnp.minimum(z, SOFTPLUS2_CLAMP))))
            if diagonal:
                sp = jnp.where(valid, sp, 0.0)
            later_in = jnp.dot(sp.astype(BF16), neg_suffix, preferred_element_type=F32)
            w = jnp.exp2((z - sp + (later_in - used[h][r0:])).astype(BF16))
            if diagonal:
                w = jnp.where(valid, w, jnp.zeros_like(w))
            pvs.append(jnp.dot(w, v, preferred_element_type=F32))
            grown = used[h][r0:] + (sp[:, 0:1] - later_in[:, 0:1])
            new_used.append(jnp.concatenate([used[h][:r0], grown], axis=0) if r0 else grown)
        pv = acc[r0:] + jnp.where(first, pvs[0], pvs[1])
        return tuple(new_used), (jnp.concatenate([acc[:r0], pv], axis=0) if r0 else pv)

    def run(first_block, count, carry):
        for d in range(count):
            carry = block(first_block - d, logits(first_block - d), carry, False)
        return carry

    assert per == 2 and SB_BLOCKS_PER_TRIP == 4
    n_full = i * per
    left = n_full % SB_BLOCKS_PER_TRIP

    def head(n_left):
        def f(carry):
            for d in range(per):
                j, r0 = (i + 1) * per - 1 - d, (per - 1 - d) * tk
                carry = block(j, logits(j, r0), carry, True, r0)
            return run(n_full - 1, n_left, carry)
        return f

    carry = ((jnp.zeros((tq, 1), F32),) * 2, jnp.zeros((tq, LANES), F32))
    carry = lax.switch(left // 2, [head(0), head(2)], carry)
    rest = n_full - left
    _, acc = lax.fori_loop(
        0, rest // SB_BLOCKS_PER_TRIP,
        lambda t, c: run(rest - 1 - SB_BLOCKS_PER_TRIP * t, SB_BLOCKS_PER_TRIP, c), carry)
    o_ref[...] = acc.astype(o_ref.dtype)


def sb_attention(qkv, *, tq=512, tk=256):
    b, s, w3 = qkv.shape
    n_pairs = w3 // 3 // LANES
    return pl.pallas_call(
        functools.partial(_sb_kernel, tq=tq, tk=tk),
        out_shape=jax.ShapeDtypeStruct((b, s, w3 // 3), BF16),
        grid=(b, n_pairs, s // tq),
        in_specs=[pl.BlockSpec((None, tq, LANES), lambda bi, p, i: (bi, i, p)),
                  pl.BlockSpec((None, s, LANES), lambda bi, p, i: (bi, 0, n_pairs + p)),
                  pl.BlockSpec((None, s, LANES), lambda bi, p, i: (bi, 0, 2 * n_pairs + p))],
        out_specs=pl.BlockSpec((None, tq, LANES), lambda bi, p, i: (bi, i, p)),
        compiler_params=_params("parallel", "parallel", "arbitrary"),
        name="sb_attention",
    )(qkv, qkv, qkv)


def _conv_kernel(x_ref, halo_ref, w_ref, colscale_ref, o_ref):
    i = pl.program_id(1)
    halo = jnp.where(i == 0, 0.0, halo_ref[...])
    ext = jnp.concatenate([halo, x_ref[...]], axis=0)
    y = ext * w_ref[CONV_WIDTH - 1:CONV_WIDTH, :]
    for back in range(1, CONV_WIDTH):
        y = y + pltpu.roll(ext, back, 0) * w_ref[CONV_WIDTH - 1 - back:CONV_WIDTH - back, :]
    y = y[CONV_HALO:, :]
    o_ref[...] = (y * _sigmoid(y) * colscale_ref[...]).astype(o_ref.dtype)


def conv_silu(x, conv_w, colscale, *, rows=512):
    b, s, c = x.shape
    per = rows // CONV_HALO
    return pl.pallas_call(
        _conv_kernel,
        out_shape=jax.ShapeDtypeStruct((b, s, c), BF16),
        grid=(b, s // rows),
        in_specs=[pl.BlockSpec((None, rows, c), lambda bi, i: (bi, i, 0)),
                  pl.BlockSpec((None, CONV_HALO, c),
                               lambda bi, i: (bi, jnp.maximum(i * per - 1, 0), 0)),
                  pl.BlockSpec((CONV_WIDTH, c), lambda bi, i: (0, 0)),
                  pl.BlockSpec((1, c), lambda bi, i: (0, 0))],
        out_specs=pl.BlockSpec((None, rows, c), lambda bi, i: (bi, i, 0)),
        compiler_params=_params("parallel", "parallel"),
        name="conv_silu",
    )(x, x, conv_w, colscale)


def _mlstm_kernel(bi_ref, bf_ref, q_ref, k_ref, v_ref, vt_ref, li_ref, lf_ref, o_ref, g_ref,
                  out_ref, c_scr, n_scr, m_scr):
    head = pl.program_id(1)
    step = pl.program_id(2)
    L = q_ref.shape[0]

    @pl.when(step == 0)
    def _():
        c_scr[...] = jnp.zeros_like(c_scr)
        n_scr[...] = jnp.zeros_like(n_scr)
        m_scr[...] = jnp.zeros_like(m_scr)

    r_idx = lax.broadcasted_iota(jnp.int32, (L, L), 0)
    c_idx = lax.broadcasted_iota(jnp.int32, (L, L), 1)
    lower = r_idx >= c_idx
    tri_u = (r_idx <= c_idx).astype(F32)

    q = q_ref[...]
    k = k_ref[...]
    c_st = c_scr[...]
    n_st = n_scr[...]
    m_st = m_scr[...]
    lf_row = _log_sigmoid(lf_ref[...] + bf_ref[head])
    li_row = li_ref[...] + bi_ref[head]
    bcum_row = jnp.dot(jnp.broadcast_to(lf_row, (8, L)), tri_u, preferred_element_type=F32,
                       precision=HIGHEST)[0:1, :]
    bcum_col = jnp.broadcast_to(bcum_row, (LANES, L)).T[:, 0:1]
    g = jnp.sum(lf_row, axis=-1, keepdims=True)

    log_d = jnp.where(lower, bcum_col - bcum_row + li_row, NEG_INF)
    inter = bcum_col + m_st
    m_out = jnp.maximum(inter, jnp.max(log_d, axis=-1, keepdims=True))
    d_mat = jnp.exp(log_d - m_out)
    inter_w = jnp.exp(inter - m_out)
    qk = lax.dot_general(q, k, NT_DIMS, preferred_element_type=F32) * d_mat
    num = (jnp.dot(qk.astype(BF16), v_ref[...], preferred_element_type=F32)
           + inter_w * lax.dot_general(q, c_st.astype(BF16), NT_DIMS, preferred_element_type=F32))
    den = (jnp.sum(qk, axis=-1, keepdims=True)
           + inter_w * jnp.sum(q.astype(F32) * n_st, axis=-1, keepdims=True))
    hh = num / jnp.maximum(jnp.abs(den), jnp.exp(-m_out))
    hn = _rms(hh, g_ref[...])
    out_ref[...] = (hn * _sigmoid(o_ref[...])).astype(out_ref.dtype)

    a_row = g - bcum_row + li_row
    m_loc = jnp.max(a_row, axis=-1, keepdims=True)
    w_row = jnp.exp(a_row - m_loc)
    c_loc = jnp.dot((vt_ref[...].astype(F32) * w_row).astype(BF16), k, preferred_element_type=F32)
    n_loc = jnp.dot(jnp.broadcast_to(w_row, (8, L)).astype(BF16), k,
                    preferred_element_type=F32)[0:1, :]
    m_new = jnp.maximum(g + m_st, m_loc)
    decay = jnp.exp(g + m_st - m_new)
    fresh = jnp.exp(m_loc - m_new)
    c_scr[...] = decay * c_st + fresh * c_loc
    n_scr[...] = decay * n_st + fresh * n_loc
    m_scr[...] = m_new


def mlstm(qk, v, v_t, li, lf, o_gate, b_i, b_f, head_g):
    b, s, w = v.shape
    n_heads = w // LANES
    rows = v_t.shape[-1]
    grid_spec = pltpu.PrefetchScalarGridSpec(
        num_scalar_prefetch=2,
        grid=(b, n_heads, s // rows),
        in_specs=[pl.BlockSpec((None, rows, LANES), lambda bi, h, c, *_: (bi, c, h)),
                  pl.BlockSpec((None, rows, LANES), lambda bi, h, c, *_: (bi, c, n_heads + h)),
                  pl.BlockSpec((None, rows, LANES), lambda bi, h, c, *_: (bi, c, h)),
                  pl.BlockSpec((None, None, None, LANES, rows),
                               lambda bi, h, c, *_: (bi, h, c, 0, 0)),
                  pl.BlockSpec((None, None, None, 1, rows), lambda bi, h, c, *_: (bi, h, c, 0, 0)),
                  pl.BlockSpec((None, None, None, 1, rows), lambda bi, h, c, *_: (bi, h, c, 0, 0)),
                  pl.BlockSpec((None, rows, LANES), lambda bi, h, c, *_: (bi, c, h)),
                  pl.BlockSpec((None, 1, LANES), lambda bi, h, c, *_: (h, 0, 0))],
        out_specs=pl.BlockSpec((None, rows, LANES), lambda bi, h, c, *_: (bi, c, h)),
        scratch_shapes=[pltpu.VMEM((LANES, LANES), F32), pltpu.VMEM((1, LANES), F32),
                        pltpu.VMEM((1, 1), F32)])
    return pl.pallas_call(
        _mlstm_kernel,
        out_shape=jax.ShapeDtypeStruct((b, s, w), BF16),
        grid_spec=grid_spec,
        compiler_params=_params("parallel", "parallel", "arbitrary"),
        name="mlstm",
    )(b_i, b_f, qk, qk, v, v_t, li, lf, o_gate, head_g.reshape(n_heads, 1, LANES))


def _out_proj_kernel(x_ref, a_ref, b_ref, w_ref, o_ref):
    na = a_ref.shape[1]
    acc = jnp.dot(a_ref[...], w_ref[0:na, :], preferred_element_type=F32)
    acc = acc + jnp.dot(b_ref[...], w_ref[na:, :], preferred_element_type=F32)
    o_ref[...] = x_ref[...] + acc


def out_proj_residual(x, a, b_part, w, *, tm=512):
    t, d = x.shape
    return pl.pallas_call(
        _out_proj_kernel,
        out_shape=jax.ShapeDtypeStruct((t, d), F32),
        grid=(t // tm,),
        in_specs=[pl.BlockSpec((tm, d), lambda i: (i, 0)),
                  pl.BlockSpec((tm, a.shape[1]), lambda i: (i, 0)),
                  pl.BlockSpec((tm, b_part.shape[1]), lambda i: (i, 0)),
                  pl.BlockSpec(w.shape, lambda i: (0, 0))],
        out_specs=pl.BlockSpec((tm, d), lambda i: (i, 0)),
        compiler_params=_params("parallel"),
        name="out_proj_residual",
    )(x, a, b_part, w)


def _ff_chunks(tf):
    return [(c0, min(c0 + FF_CHUNK, tf)) for c0 in range(0, tf, FF_CHUNK)]


def _swiglu_accumulate(h_scr, wg_ref, wu_ref, wd_ref, act_scr, o_ref, after_chunk=None):
    h = h_scr[...]
    for c, (c0, c1) in enumerate(_ff_chunks(wg_ref.shape[1])):
        gate = jnp.dot(h, wg_ref[:, c0:c1], preferred_element_type=F32)
        up = jnp.dot(h, wu_ref[:, c0:c1], preferred_element_type=F32)
        act_scr[:, c0:c1] = (gate * _sigmoid(gate) * up).astype(BF16)
        if after_chunk is not None:
            after_chunk(c)
    o_ref[...] += jnp.dot(act_scr[...], wd_ref[...], preferred_element_type=F32)


def _ffn_kernel(x_ref, g_ref, wg_ref, wu_ref, wd_ref, o_ref, h_scr, act_scr):
    @pl.when(pl.program_id(1) == 0)
    def _():
        x = x_ref[...]
        h_scr[...] = _rms(x, g_ref[...]).astype(BF16)
        o_ref[...] = x

    _swiglu_accumulate(h_scr, wg_ref, wu_ref, wd_ref, act_scr, o_ref)


def ffn_residual(x, g, w_gate, w_up, w_down, layer, *, tm=1024, tf=1792):
    t, d = x.shape
    ff = w_gate.shape[2]
    return pl.pallas_call(
        _ffn_kernel,
        out_shape=jax.ShapeDtypeStruct((t, d), F32),
        grid=(t // tm, ff // tf),
        in_specs=[pl.BlockSpec((tm, d), lambda i, j: (i, 0)),
                  pl.BlockSpec((1, d), lambda i, j: (0, 0)),
                  pl.BlockSpec((None, d, tf), lambda i, j: (layer, 0, j)),
                  pl.BlockSpec((None, d, tf), lambda i, j: (layer, 0, j)),
                  pl.BlockSpec((None, tf, d), lambda i, j: (layer, j, 0))],
        out_specs=pl.BlockSpec((tm, d), lambda i, j: (i, 0)),
        scratch_shapes=[pltpu.VMEM((tm, d), BF16), pltpu.VMEM((tm, tf), BF16)],
        compiler_params=_params("parallel", "arbitrary"),
        name="ffn_residual",
    )(x, g.reshape(1, d), w_gate, w_up, w_down)


def _store_row_tiles(ref, x):
    rows, d = x.shape
    per = d // LANES
    for s in range(per):
        ref[pl.ds(s, rows, stride=per), :] = x[:, s * LANES:(s + 1) * LANES]


def _load_row_tiles(ref, rows, per):
    return [ref[pl.ds(s, rows, stride=per), :] for s in range(per)]


def _router_kernel(x_ref, g_ref, wr_ref, h_ref, route_ref):
    h = _rms(x_ref[...], g_ref[...])
    _store_row_tiles(h_ref, h)
    logits = jnp.dot(h, wr_ref[...], preferred_element_type=F32, precision=HIGHEST)
    lane = lax.broadcasted_iota(jnp.int32, logits.shape, 1)
    logits = jnp.where(lane < N_EXPERTS, logits, NEG_INF)
    m1 = jnp.max(logits, axis=-1, keepdims=True)
    i1 = jnp.min(jnp.where(logits == m1, lane, LANES), axis=-1, keepdims=True)
    rest = jnp.where(lane == i1, NEG_INF, logits)
    m2 = jnp.max(rest, axis=-1, keepdims=True)
    i2 = jnp.min(jnp.where(rest == m2, lane, LANES), axis=-1, keepdims=True)
    e2 = jnp.exp(m2 - m1)
    g1 = 1.0 / (1.0 + e2)
    g2 = e2 / (1.0 + e2)
    route_ref[...] = jnp.where(lane == 0, i1.astype(F32),
                               jnp.where(lane == 1, i2.astype(F32),
                                         jnp.where(lane == 2, g1, jnp.where(lane == 3, g2, 0.0))))


def router(x, g, w_router_padded, *, tm=512):
    t, d = x.shape
    per = d // LANES
    return pl.pallas_call(
        _router_kernel,
        out_shape=[jax.ShapeDtypeStruct((t * per, LANES), F32),
                   jax.ShapeDtypeStruct((t, LANES), F32)],
        grid=(t // tm,),
        in_specs=[pl.BlockSpec((tm, d), lambda i: (i, 0)),
                  pl.BlockSpec((1, d), lambda i: (0, 0)),
                  pl.BlockSpec((d, LANES), lambda i: (0, 0))],
        out_specs=[pl.BlockSpec((tm * per, LANES), lambda i: (i, 0)),
                   pl.BlockSpec((tm, LANES), lambda i: (i, 0))],
        compiler_params=_params("parallel"),
        name="router",
    )(x, g.reshape(1, d), w_router_padded)


def _expert_kernel(be_ref, nused_ref, tok_cur_ref, tok_next_ref, h_hbm, wg_ref, wu_ref, wd_ref,
                   o_ref, xbuf, sems, h_scr, act_scr, acc_scr, *, tm):
    i = pl.program_id(0)
    j = pl.program_id(1)
    n_used = nused_ref[0]
    used = i < n_used
    per = h_scr.shape[1] // LANES

    def issue_gather(tok_ref, slot):
        def body(rr, _):
            for prio in range(DMA_PRIORITIES):
                r = rr * DMA_PRIORITIES + prio
                src = pl.multiple_of(tok_ref[0, r] * per, per)
                dst = pl.multiple_of(r * per, per)
                pltpu.make_async_copy(h_hbm.at[pl.ds(src, per)], xbuf.at[slot, pl.ds(dst, per)],
                                      sems.at[slot]).start(priority=prio)
            return 0
        lax.fori_loop(0, tm // DMA_PRIORITIES, body, 0, unroll=4)

    @pl.when(jnp.logical_and(i == 0, j == 0))
    def _():
        issue_gather(tok_cur_ref, 0)

    @pl.when(jnp.logical_and(used, j == 0))
    def _():
        slot = i % 2
        pltpu.make_async_copy(xbuf.at[slot], xbuf.at[slot], sems.at[slot]).wait()
        for s, piece in enumerate(_load_row_tiles(xbuf.at[slot], tm, per)):
            h_scr[:, s * LANES:(s + 1) * LANES] = piece.astype(BF16)
        acc_scr[...] = jnp.zeros_like(acc_scr)

    @pl.when(used)
    def _():
        _swiglu_accumulate(h_scr, wg_ref, wu_ref, wd_ref, act_scr, acc_scr)

    @pl.when(jnp.logical_and(i + 1 < n_used, j == 0))
    def _():
        issue_gather(tok_next_ref, (i + 1) % 2)

    @pl.when(jnp.logical_and(used, j == pl.num_programs(1) - 1))
    def _():
        _store_row_tiles(o_ref, acc_scr[...])

    @pl.when(jnp.logical_and(jnp.logical_not(used), j == 0))
    def _():
        o_ref[...] = jnp.zeros_like(o_ref)


def expert_ffn(h, row_tok, block_expert, n_used, w_gate, w_up, w_down, layer, *, tm, tf=1792):
    n_rows = row_tok.shape[0]
    d = w_gate.shape[2]
    per = d // LANES
    ff = w_gate.shape[3]
    n_blocks = n_rows // tm
    last_j = ff // tf - 1

    def up_block(i, j, be, nu):
        return (layer, be[jnp.minimum(i, nu[0] - 1)], 0, jnp.where(i < nu[0], j, last_j))

    def down_block(i, j, be, nu):
        return (layer, be[jnp.minimum(i, nu[0] - 1)], jnp.where(i < nu[0], j, last_j), 0)

    grid_spec = pltpu.PrefetchScalarGridSpec(
        num_scalar_prefetch=2,
        grid=(n_blocks, ff // tf),
        in_specs=[pl.BlockSpec((None, 1, tm), lambda i, j, be, nu: (i, 0, 0),
                               memory_space=pltpu.SMEM),
                  pl.BlockSpec((None, 1, tm),
                               lambda i, j, be, nu: (jnp.minimum(i + 1, n_blocks - 1), 0, 0),
                               memory_space=pltpu.SMEM),
                  pl.BlockSpec(memory_space=pl.ANY),
                  pl.BlockSpec((None, None, d, tf), up_block),
                  pl.BlockSpec((None, None, d, tf), up_block),
                  pl.BlockSpec((None, None, tf, d), down_block)],
        out_specs=pl.BlockSpec((tm * per, LANES), lambda i, j, be, nu: (i, 0)),
        scratch_shapes=[pltpu.VMEM((2, tm * per, LANES), F32), pltpu.SemaphoreType.DMA((2,)),
                        pltpu.VMEM((tm, d), BF16), pltpu.VMEM((tm, tf), BF16),
                        pltpu.VMEM((tm, d), F32)])
    tok_blocks = row_tok.reshape(n_blocks, 1, tm)
    return pl.pallas_call(
        functools.partial(_expert_kernel, tm=tm),
        out_shape=jax.ShapeDtypeStruct((n_rows * per, LANES), F32),
        grid_spec=grid_spec,
        compiler_params=_params("arbitrary", "arbitrary"),
        name="expert_ffn",
    )(block_expert, n_used, tok_blocks, tok_blocks, h, w_gate, w_up, w_down)


def _combine_kernel(pos_cur_ref, pos_next_ref, x_ref, route_ref, y_ref, *rest, rows, final):
    if final:
        g_ref, o_ref, buf, sems = rest
    else:
        o_ref, buf, sems = rest
    i = pl.program_id(0)
    n = pl.num_programs(0)
    per = x_ref.shape[1] // LANES

    def issue(pos_ref, slot):
        def body(r, _):
            dst = pl.multiple_of(r * per, per)
            for kk in range(TOP_K):
                src = pl.multiple_of(pos_ref[0, TOP_K * r + kk] * per, per)
                pltpu.make_async_copy(y_ref.at[pl.ds(src, per)], buf.at[slot, kk, pl.ds(dst, per)],
                                      sems.at[slot]).start(priority=kk % DMA_PRIORITIES)
            return 0
        lax.fori_loop(0, rows, body, 0, unroll=4)

    @pl.when(i == 0)
    def _():
        issue(pos_cur_ref, 0)

    @pl.when(i + 1 < n)
    def _():
        issue(pos_next_ref, (i + 1) % 2)

    slot = i % 2
    pltpu.make_async_copy(buf.at[slot], buf.at[slot], sems.at[slot]).wait()
    route = route_ref[...]
    g1 = route[:, 2:3]
    g2 = route[:, 3:4]
    y1 = jnp.concatenate(_load_row_tiles(buf.at[slot, 0], rows, per), axis=-1)
    y2 = jnp.concatenate(_load_row_tiles(buf.at[slot, 1], rows, per), axis=-1)
    out = x_ref[...] + g1 * y1 + g2 * y2
    o_ref[...] = _rms(out, g_ref[...]) if final else out


def combine_residual(x, route, y_rows, pos, final_g=None, *, rows=256):
    t, d = x.shape
    final = final_g is not None
    n_steps = t // rows
    grid_spec = pltpu.PrefetchScalarGridSpec(
        num_scalar_prefetch=0,
        grid=(n_steps,),
        in_specs=[pl.BlockSpec((None, 1, TOP_K * rows), lambda i: (i, 0, 0),
                               memory_space=pltpu.SMEM),
                  pl.BlockSpec((None, 1, TOP_K * rows),
                               lambda i: (jnp.minimum(i + 1, n_steps - 1), 0, 0),
                               memory_space=pltpu.SMEM),
                  pl.BlockSpec((rows, d), lambda i: (i, 0)),
                  pl.BlockSpec((rows, LANES), lambda i: (i, 0)),
                  pl.BlockSpec(memory_space=pl.ANY)]
                 + ([pl.BlockSpec((1, d), lambda i: (0, 0))] if final else []),
        out_specs=pl.BlockSpec((rows, d), lambda i: (i, 0)),
        scratch_shapes=[pltpu.VMEM((2, TOP_K, rows * (d // LANES), LANES), F32),
                        pltpu.SemaphoreType.DMA((2,))])
    pos_blocks = pos.reshape(n_steps, 1, TOP_K * rows)
    extra = (final_g.reshape(1, d),) if final else ()
    return pl.pallas_call(
        functools.partial(_combine_kernel, rows=rows, final=final),
        out_shape=jax.ShapeDtypeStruct((t, d), F32),
        grid_spec=grid_spec,
        compiler_params=_params("arbitrary"),
        name="combine_residual",
    )(pos_blocks, pos_blocks, x, route, y_rows, *extra)


def _pad_cols(w, n):
    return jnp.pad(w, ((0, 0), (0, n - w.shape[1])))


def _scale_q_cols(w_qkv, width):
    scale = jnp.concatenate([jnp.full((width,), LOG2E * HEAD_DIM ** -0.5, F32),
                             jnp.ones((2 * width,), F32)])
    return w_qkv * scale


def even_layer(x, b, s, g_mix, g_ffn, w_in, b_forget, pool_w, pool_scale, w_out,
               w_gate, w_up, w_down, e, *, fox_tq=512):
    t, d = x.shape
    pool_width = pool_w.shape[0] * pool_w.shape[1]
    n_heads = b_forget.shape[0]
    fox_width = n_heads * HEAD_DIM
    w_a = w_in[:, :pool_width].astype(BF16)
    w_qkv = _scale_q_cols(w_in[:, pool_width:pool_width + 3 * fox_width], fox_width).astype(BF16)
    w_f = _pad_cols(w_in[:, pool_width + 3 * fox_width:], LANES).astype(BF16)
    qkv, a_in, f_pre = norm_proj(x, g_mix, [w_qkv, w_a, w_f], [BF16, F32, F32])

    pool_bd = jax.scipy.linalg.block_diag(*[pool_w[i] for i in range(pool_w.shape[0])]).astype(BF16)
    a_out, cum = pool_and_cum(a_in.reshape(b, s, pool_width), f_pre.reshape(b, s, LANES),
                              _pad_cols(b_forget.reshape(1, n_heads), LANES), pool_bd,
                              pool_scale.reshape(1, pool_width))
    cum_blocks = cum[:, :, :n_heads].reshape(b, s // fox_tq, fox_tq, n_heads // 2, 2)
    cum_blocks = cum_blocks.transpose(0, 3, 1, 4, 2)
    attn = fox_attention(qkv.reshape(b, s, 3 * fox_width), cum_blocks, tq=fox_tq)
    x = out_proj_residual(x, a_out.reshape(t, pool_width), attn.reshape(t, fox_width),
                          w_out.astype(BF16))
    return ffn_residual(x, g_ffn, w_gate, w_up, w_down, e)


def odd_layer(x, b, s, g_mix, g_ffn, w_in, b_igate, b_fgate, conv_w, head_norm_g, w_out,
              w_router, w_gate, w_up, w_down, e, final_g=None):
    x = odd_mixer_block(x, b, s, g_mix, w_in, b_igate, b_fgate, conv_w, head_norm_g, w_out)
    return moe_block(x, g_ffn, w_router, w_gate, w_up, w_down, e, final_g)


def odd_mixer_block(x, b, s, g_mix, w_in, b_igate, b_fgate, conv_w, head_norm_g, w_out):
    t, d = x.shape
    n_ml = b_igate.shape[0]
    ml_width = n_ml * LANES
    sb_width = d - ml_width
    cols = [3 * sb_width, 2 * ml_width, ml_width, ml_width]
    edges = [0]
    for c in cols:
        edges.append(edges[-1] + c)
    w_sb, w_mqk, w_mv, w_mo = [w_in[:, edges[i]:edges[i + 1]] for i in range(4)]
    w_sb = _scale_q_cols(w_sb, sb_width)
    w_sb, w_mqk, w_mv, w_mo = [w.astype(BF16) for w in (w_sb, w_mqk, w_mv, w_mo)]
    w_gates = _pad_cols(w_in[:, edges[-1]:], LANES).astype(BF16)
    sqkv, ml_qk, ml_v, ml_o, gates = norm_proj(
        x, g_mix, [w_sb, w_mqk, w_mv, w_mo, w_gates], [BF16, F32, BF16, F32, F32])

    c_out = sb_attention(sqkv.reshape(b, s, 3 * sb_width))

    k_scale = jnp.concatenate([jnp.ones((ml_width,), F32),
                               jnp.full((ml_width,), LANES ** -0.5, F32)]).reshape(1, 2 * ml_width)
    qk = conv_silu(ml_qk.reshape(b, s, 2 * ml_width), conv_w, k_scale)
    nb = s // MLSTM_BLOCK
    v3 = ml_v.reshape(b, s, ml_width)
    v_t = v3.reshape(b, nb, MLSTM_BLOCK, n_ml, LANES).transpose(0, 3, 1, 4, 2)
    gate_rows = gates[:, :2 * n_ml].reshape(b, nb, MLSTM_BLOCK, 2 * n_ml).transpose(0, 3, 1, 2)
    gate_rows = gate_rows.reshape(b, 2 * n_ml, nb, 1, MLSTM_BLOCK)
    d_out = mlstm(qk, v3, v_t, gate_rows[:, :n_ml], gate_rows[:, n_ml:],
                  ml_o.reshape(b, s, ml_width), b_igate, b_fgate, head_norm_g)
    return out_proj_residual(x, c_out.reshape(t, sb_width), d_out.reshape(t, ml_width),
                             w_out.astype(BF16))


def moe_block(x, g_ffn, w_router, w_gate, w_up, w_down, e, final_g=None, *, expert_rows=1024):
    t, d = x.shape
    h, route = router(x, g_ffn, _pad_cols(w_router, LANES))
    flat_e = route[:, :TOP_K].astype(jnp.int32).reshape(-1)
    n_assign = t * TOP_K
    onehot = jax.nn.one_hot(flat_e, N_EXPERTS, dtype=jnp.int32)
    rank = jnp.sum((jnp.cumsum(onehot, axis=0) - onehot) * onehot, axis=-1)
    counts = jnp.sum(onehot, axis=0)
    padded = (counts + expert_rows - 1) // expert_rows * expert_rows
    group_end = jnp.cumsum(padded)
    pos = (group_end[flat_e] - padded[flat_e] + rank).astype(jnp.int32)
    n_rows = n_assign + N_EXPERTS * expert_rows
    n_blocks = n_rows // expert_rows
    flat_tok = jnp.arange(n_assign, dtype=jnp.int32) // TOP_K
    row_tok = jnp.zeros((n_rows,), jnp.int32).at[pos].set(flat_tok)
    block_expert = jnp.minimum(
        jnp.searchsorted(group_end, jnp.arange(n_blocks) * expert_rows, side='right'),
        N_EXPERTS - 1).astype(jnp.int32)
    n_used = (group_end[-1:] // expert_rows).astype(jnp.int32)

    y_rows = expert_ffn(h, row_tok, block_expert, n_used, w_gate, w_up, w_down, e, tm=expert_rows)
    return combine_residual(x, route, y_rows, pos, final_g)


def kernel(x, norm_mix_g, norm_ffn_g, norm_final_g, ev_w_in, ev_b_forget, ev_pool_w, ev_pool_scale, ev_w_out, ffn_w_gate, ffn_w_up, ffn_w_down, od_w_in, od_b_igate, od_b_fgate, od_conv_w, od_head_norm_g, od_w_out, moe_w_router, moe_w_gate, moe_w_up, moe_w_down):
    b, s, d = x.shape
    depth = norm_mix_g.shape[0]
    xt = x.reshape(b * s, d)
    ffn_w = [w.astype(BF16) for w in (ffn_w_gate, ffn_w_up, ffn_w_down)]
    moe_w = [w.astype(BF16) for w in (moe_w_gate, moe_w_up, moe_w_down)]
    for layer in range(depth):
        e = layer // 2
        if layer % 2 == 0:
            xt = even_layer(xt, b, s, norm_mix_g[layer], norm_ffn_g[layer], ev_w_in[e],
                            ev_b_forget[e], ev_pool_w[e], ev_pool_scale[e], ev_w_out[e],
                            *ffn_w, e)
        else:
            xt = odd_layer(xt, b, s, norm_mix_g[layer], norm_ffn_g[layer], od_w_in[e],
                           od_b_igate[e], od_b_fgate[e], od_conv_w[e], od_head_norm_g[e],
                           od_w_out[e], moe_w_router[e], *moe_w, e,
                           norm_final_g if layer == depth - 1 else None)
    assert depth % 2 == 0
    return xt.reshape(b, s, d)
```

```python
import functools

import jax
import jax.numpy as jnp
from jax import lax
from jax.experimental import pallas as pl
from jax.experimental.pallas import tpu as pltpu

F32 = jnp.float32
BF16 = jnp.bfloat16
EPS = 1e-6
NEG_INF = float("-inf")

LANES = 128
HEAD_DIM = 64
POOL_WINDOWS = (2, 4, 8, 16)
POOL_HALO = 16
CONV_WIDTH = 4
CONV_HALO = 8
FF_CHUNK = 512
MLSTM_BLOCK = 512
FOX_BLOCKS_PER_TRIP = 3
SB_BLOCKS_PER_TRIP = 4
DMA_PRIORITIES = 2
N_EXPERTS = 8
TOP_K = 2
VMEM_LIMIT = 56 * 1024 * 1024

LOG2E = 1.4426950408889634
SOFTPLUS2_CLAMP = 64.0
HIGHEST = lax.Precision.HIGHEST
NT_DIMS = (((1,), (1,)), ((), ()))


def _params(*semantics):
    return pltpu.CompilerParams(dimension_semantics=semantics, vmem_limit_bytes=VMEM_LIMIT)


def _rms(x, g):
    return x * lax.rsqrt(jnp.mean(x * x, axis=-1, keepdims=True) + EPS) * g


def _log_sigmoid(x):
    return jnp.minimum(x, 0.0) - jnp.log(1.0 + jnp.exp(-jnp.abs(x)))


def _sigmoid(x):
    return 1.0 / (1.0 + jnp.exp(-x))


def _norm_proj_kernel(x_ref, g_ref, *refs, n_out, col_chunk):
    w_refs, o_refs = refs[:n_out], refs[n_out:]
    h = _rms(x_ref[...], g_ref[...]).astype(BF16)
    for w_ref, o_ref in zip(w_refs, o_refs):
        n = w_ref.shape[1]
        for c0 in range(0, n, col_chunk):
            c1 = min(c0 + col_chunk, n)
            o_ref[:, c0:c1] = jnp.dot(
                h, w_ref[:, c0:c1], preferred_element_type=F32).astype(o_ref.dtype)


def norm_proj(x, g, weights, out_dtypes, *, tm=512, col_chunk=512):
    t, d = x.shape
    n_out = len(weights)
    return pl.pallas_call(
        functools.partial(_norm_proj_kernel, n_out=n_out, col_chunk=col_chunk),
        out_shape=[jax.ShapeDtypeStruct((t, w.shape[1]), dt) for w, dt in zip(weights, out_dtypes)],
        grid=(t // tm,),
        in_specs=[pl.BlockSpec((tm, d), lambda i: (i, 0)),
                  pl.BlockSpec((1, d), lambda i: (0, 0))]
                 + [pl.BlockSpec(w.shape, lambda i: (0, 0)) for w in weights],
        out_specs=[pl.BlockSpec((tm, w.shape[1]), lambda i: (i, 0)) for w in weights],
        compiler_params=_params("parallel"),
        name="norm_proj",
    )(x, g.reshape(1, d), *weights)


def _pool_kernel(a_ref, f_ref, bf_ref, w_ref, scale_ref, o_ref, cum_ref, *, rows):
    s, width = a_ref.shape
    group = lax.broadcasted_iota(jnp.int32, (1, width), 1) // (width // len(POOL_WINDOWS))
    window = jnp.where(group == 0, POOL_WINDOWS[0],
                       jnp.where(group == 1, POOL_WINDOWS[1],
                                 jnp.where(group == 2, POOL_WINDOWS[2], POOL_WINDOWS[3])))
    tri = (lax.broadcasted_iota(jnp.int32, (rows, rows), 0)
           >= lax.broadcasted_iota(jnp.int32, (rows, rows), 1)).astype(F32)
    carry = jnp.zeros((1, f_ref.shape[1]), F32)
    for r in range(s // rows):
        r0 = r * rows
        if r == 0:
            ext = jnp.concatenate([jnp.zeros((POOL_HALO, width), F32), a_ref[0:rows, :]], axis=0)
        else:
            ext = a_ref[r0 - POOL_HALO:r0 + rows, :]
        s2 = ext + pltpu.roll(ext, 1, 0)
        s4 = s2 + pltpu.roll(s2, 2, 0)
        s8 = s4 + pltpu.roll(s4, 4, 0)
        s16 = s8 + pltpu.roll(s8, 8, 0)
        wsum = jnp.where(group == 0, s2, jnp.where(group == 1, s4, jnp.where(group == 2, s8, s16)))
        wsum = wsum[POOL_HALO:, :]
        a = ext[POOL_HALO:, :]
        t1 = r0 + 1 + lax.broadcasted_iota(jnp.int32, (rows, 1), 0)
        count = jnp.minimum(t1, window).astype(F32)
        pooled = wsum / count - a
        mixed = jnp.dot(pooled.astype(BF16), w_ref[...], preferred_element_type=F32)
        o_ref[r0:r0 + rows, :] = (mixed * scale_ref[...]).astype(o_ref.dtype)
        lf = _log_sigmoid(f_ref[r0:r0 + rows, :] + bf_ref[...]) * LOG2E
        cum = jnp.dot(tri, lf, preferred_element_type=F32, precision=HIGHEST) + carry
        cum_ref[r0:r0 + rows, :] = cum
        carry = cum[rows - 1:rows, :]


def pool_and_cum(a_in, f_pre, b_forget, pool_w_bd, pool_scale, *, rows=256):
    b, s, width = a_in.shape
    fw = f_pre.shape[-1]
    return pl.pallas_call(
        functools.partial(_pool_kernel, rows=rows),
        out_shape=[jax.ShapeDtypeStruct((b, s, width), BF16),
                   jax.ShapeDtypeStruct((b, s, fw), F32)],
        grid=(b,),
        in_specs=[pl.BlockSpec((None, s, width), lambda i: (i, 0, 0)),
                  pl.BlockSpec((None, s, fw), lambda i: (i, 0, 0)),
                  pl.BlockSpec((1, fw), lambda i: (0, 0)),
                  pl.BlockSpec((width, width), lambda i: (0, 0)),
                  pl.BlockSpec((1, width), lambda i: (0, 0))],
        out_specs=[pl.BlockSpec((None, s, width), lambda i: (i, 0, 0)),
                   pl.BlockSpec((None, s, fw), lambda i: (i, 0, 0))],
        compiler_params=_params("parallel"),
        name="pool_and_cum",
    )(a_in, f_pre, b_forget, pool_w_bd, pool_scale)


def _fox_kernel(q_ref, k_ref, v_ref, cum_ref, o_ref, *, tq):
    i = pl.program_id(2)
    lane = lax.broadcasted_iota(jnp.int32, (1, LANES), 1)
    first = lane < HEAD_DIM
    q = q_ref[...]
    q_heads = (jnp.where(first, q, jnp.zeros_like(q)), jnp.where(first, jnp.zeros_like(q), q))
    row = lax.broadcasted_iota(jnp.int32, (tq, tq), 0)
    col = lax.broadcasted_iota(jnp.int32, (tq, tq), 1)

    def block(j, carry, masked):
        ms, accs = carry
        start = pl.multiple_of(j * tq, tq)
        k = k_ref[pl.ds(start, tq), :]
        v = v_ref[pl.ds(start, tq), :]
        ones = jnp.ones_like(v)
        v_heads = (jnp.where(first, v, ones), jnp.where(first, ones, v))
        ck = cum_ref[j]
        new_ms, new_accs = [], []
        scs = [lax.dot_general(q_heads[h], k, NT_DIMS, preferred_element_type=F32)
               for h in range(2)]
        for h in range(2):
            sc = scs[h] - ck[h:h + 1, :]
            if masked:
                sc = jnp.where(col <= row, sc, NEG_INF)
            m_new = jnp.maximum(ms[h], jnp.max(sc, axis=-1, keepdims=True))
            alpha = jnp.exp2(ms[h] - m_new)
            p = jnp.exp2((sc - m_new).astype(BF16))
            new_ms.append(m_new)
            new_accs.append(alpha * accs[h] + jnp.dot(p, v_heads[h], preferred_element_type=F32))
        return tuple(new_ms), tuple(new_accs)

    init = ((jnp.full((tq, 1), NEG_INF, F32),) * 2, (jnp.zeros((tq, LANES), F32),) * 2)
    def run(first_block, count, c):
        for d in range(count):
            c = block(first_block + d, c, False)
        return c

    n_trips = i // FOX_BLOCKS_PER_TRIP
    carry = lax.fori_loop(
        0, n_trips, lambda t, c: run(FOX_BLOCKS_PER_TRIP * t, FOX_BLOCKS_PER_TRIP, c), init)
    def tail(n_left):
        return lambda c: block(i, run(i - n_left, n_left, c), True)

    left = i - FOX_BLOCKS_PER_TRIP * n_trips
    _, accs = lax.switch(left, [tail(n) for n in range(FOX_BLOCKS_PER_TRIP)], carry)
    out0 = accs[0] / accs[0][:, HEAD_DIM:HEAD_DIM + 1]
    out1 = accs[1] / accs[1][:, 0:1]
    o_ref[...] = jnp.where(first, out0, out1).astype(o_ref.dtype)


def fox_attention(qkv, cum_blocks, *, tq=512):
    b, s, w3 = qkv.shape
    n_pairs = w3 // 3 // LANES
    return pl.pallas_call(
        functools.partial(_fox_kernel, tq=tq),
        out_shape=jax.ShapeDtypeStruct((b, s, w3 // 3), BF16),
        grid=(b, n_pairs, s // tq),
        in_specs=[pl.BlockSpec((None, tq, LANES), lambda bi, p, i: (bi, i, p)),
                  pl.BlockSpec((None, s, LANES), lambda bi, p, i: (bi, 0, n_pairs + p)),
                  pl.BlockSpec((None, s, LANES), lambda bi, p, i: (bi, 0, 2 * n_pairs + p)),
                  pl.BlockSpec((None, None, s // tq, 2, tq), lambda bi, p, i: (bi, p, 0, 0, 0))],
        out_specs=pl.BlockSpec((None, tq, LANES), lambda bi, p, i: (bi, i, p)),
        compiler_params=_params("parallel", "parallel", "arbitrary"),
        name="fox_attention",
    )(qkv, qkv, qkv, cum_blocks)


def _sb_kernel(q_ref, k_ref, v_ref, o_ref, *, tq, tk):
    i = pl.program_id(2)
    per = tq // tk
    lane = lax.broadcasted_iota(jnp.int32, (1, LANES), 1)
    first = lane < HEAD_DIM
    q = q_ref[...]
    q_heads = (jnp.where(first, q, jnp.zeros_like(q)), jnp.where(first, jnp.zeros_like(q), q))
    row = lax.broadcasted_iota(jnp.int32, (tq, tk), 0)
    col = lax.broadcasted_iota(jnp.int32, (tq, tk), 1)
    neg_suffix = jnp.where(lax.broadcasted_iota(jnp.int32, (tk, tk), 0)
                           > lax.broadcasted_iota(jnp.int32, (tk, tk), 1), -1.0, 0.0).astype(BF16)

    def logits(j, r0=0):
        start = pl.multiple_of(j * tk, tk)
        k = k_ref[pl.ds(start, tk), :]
        return tuple(lax.dot_general(q_heads[h][r0:], k, NT_DIMS, preferred_element_type=F32)
                     for h in range(2))

    def block(j, zs, carry, diagonal, r0=0):
        used, acc = carry
        start = pl.multiple_of(j * tk, tk)
        v = v_ref[pl.ds(start, tk), :]
        if diagonal:
            valid = (col < row)[:tq - r0]
        new_used, pvs = [], []
        for h in range(2):
            z = zs[h]
            sp = jnp.maximum(z, jnp.log2(1.0 + jnp.exp2(jnp.minimum(z, SOFTPLUS2_CLAMP))))
            if diagonal:
                sp = jnp.where(valid, sp, 0.0)
            later_in = jnp.dot(sp.astype(BF16), neg_suffix, preferred_element_type=F32)
            w = jnp.exp2((z - sp + (later_in - used[h][r0:])).astype(BF16))
            if diagonal:
                w = jnp.where(valid, w, jnp.zeros_like(w))
            pvs.append(jnp.dot(w, v, preferred_element_type=F32))
            grown = used[h][r0:] + (sp[:, 0:1] - later_in[:, 0:1])
            new_used.append(jnp.concatenate([used[h][:r0], grown], axis=0) if r0 else grown)
        pv = acc[r0:] + jnp.where(first, pvs[0], pvs[1])
        return tuple(new_used), (jnp.concatenate([acc[:r0], pv], axis=0) if r0 else pv)

    def run(first_block, count, carry):
        for d in range(count):
            carry = block(first_block - d, logits(first_block - d), carry, False)
        return carry

    assert per == 2 and SB_BLOCKS_PER_TRIP == 4
    n_full = i * per
    left = n_full % SB_BLOCKS_PER_TRIP

    def head(n_left):
        def f(carry):
            for d in range(per):
                j, r0 = (i + 1) * per - 1 - d, (per - 1 - d) * tk
                carry = block(j, logits(j, r0), carry, True, r0)
            return run(n_full - 1, n_left, carry)
        return f

    carry = ((jnp.zeros((tq, 1), F32),) * 2, jnp.zeros((tq, LANES), F32))
    carry = lax.switch(left // 2, [head(0), head(2)], carry)
    rest = n_full - left
    _, acc = lax.fori_loop(
        0, rest // SB_BLOCKS_PER_TRIP,
        lambda t, c: run(rest - 1 - SB_BLOCKS_PER_TRIP * t, SB_BLOCKS_PER_TRIP, c), carry)
    o_ref[...] = acc.astype(o_ref.dtype)


def sb_attention(qkv, *, tq=512, tk=256):
    b, s, w3 = qkv.shape
    n_pairs = w3 // 3 // LANES
    return pl.pallas_call(
        functools.partial(_sb_kernel, tq=tq, tk=tk),
        out_shape=jax.ShapeDtypeStruct((b, s, w3 // 3), BF16),
        grid=(b, n_pairs, s // tq),
        in_specs=[pl.BlockSpec((None, tq, LANES), lambda bi, p, i: (bi, i, p)),
                  pl.BlockSpec((None, s, LANES), lambda bi, p, i: (bi, 0, n_pairs + p)),
                  pl.BlockSpec((None, s, LANES), lambda bi, p, i: (bi, 0, 2 * n_pairs + p))],
        out_specs=pl.BlockSpec((None, tq, LANES), lambda bi, p, i: (bi, i, p)),
        compiler_params=_params("parallel", "parallel", "arbitrary"),
        name="sb_attention",
    )(qkv, qkv, qkv)


def _conv_kernel(x_ref, halo_ref, w_ref, colscale_ref, o_ref):
    i = pl.program_id(1)
    halo = jnp.where(i == 0, 0.0, halo_ref[...])
    ext = jnp.concatenate([halo, x_ref[...]], axis=0)
    y = ext * w_ref[CONV_WIDTH - 1:CONV_WIDTH, :]
    for back in range(1, CONV_WIDTH):
        y = y + pltpu.roll(ext, back, 0) * w_ref[CONV_WIDTH - 1 - back:CONV_WIDTH - back, :]
    y = y[CONV_HALO:, :]
    o_ref[...] = (y * _sigmoid(y) * colscale_ref[...]).astype(o_ref.dtype)


def conv_silu(x, conv_w, colscale, *, rows=512):
    b, s, c = x.shape
    per = rows // CONV_HALO
    return pl.pallas_call(
        _conv_kernel,
        out_shape=jax.ShapeDtypeStruct((b, s, c), BF16),
        grid=(b, s // rows),
        in_specs=[pl.BlockSpec((None, rows, c), lambda bi, i: (bi, i, 0)),
                  pl.BlockSpec((None, CONV_HALO, c),
                               lambda bi, i: (bi, jnp.maximum(i * per - 1, 0), 0)),
                  pl.BlockSpec((CONV_WIDTH, c), lambda bi, i: (0, 0)),
                  pl.BlockSpec((1, c), lambda bi, i: (0, 0))],
        out_specs=pl.BlockSpec((None, rows, c), lambda bi, i: (bi, i, 0)),
        compiler_params=_params("parallel", "parallel"),
        name="conv_silu",
    )(x, x, conv_w, colscale)


def _mlstm_kernel(bi_ref, bf_ref, q_ref, k_ref, v_ref, vt_ref, li_ref, lf_ref, o_ref, g_ref,
                  out_ref, c_scr, n_scr, m_scr):
    head = pl.program_id(1)
    step = pl.program_id(2)
    L = q_ref.shape[0]

    @pl.when(step == 0)
    def _():
        c_scr[...] = jnp.zeros_like(c_scr)
        n_scr[...] = jnp.zeros_like(n_scr)
        m_scr[...] = jnp.zeros_like(m_scr)

    r_idx = lax.broadcasted_iota(jnp.int32, (L, L), 0)
    c_idx = lax.broadcasted_iota(jnp.int32, (L, L), 1)
    lower = r_idx >= c_idx
    tri_u = (r_idx <= c_idx).astype(F32)

    q = q_ref[...]
    k = k_ref[...]
    c_st = c_scr[...]
    n_st = n_scr[...]
    m_st = m_scr[...]
    lf_row = _log_sigmoid(lf_ref[...] + bf_ref[head])
    li_row = li_ref[...] + bi_ref[head]
    bcum_row = jnp.dot(jnp.broadcast_to(lf_row, (8, L)), tri_u, preferred_element_type=F32,
                       precision=HIGHEST)[0:1, :]
    bcum_col = jnp.broadcast_to(bcum_row, (LANES, L)).T[:, 0:1]
    g = jnp.sum(lf_row, axis=-1, keepdims=True)

    log_d = jnp.where(lower, bcum_col - bcum_row + li_row, NEG_INF)
    inter = bcum_col + m_st
    m_out = jnp.maximum(inter, jnp.max(log_d, axis=-1, keepdims=True))
    d_mat = jnp.exp(log_d - m_out)
    inter_w = jnp.exp(inter - m_out)
    qk = lax.dot_general(q, k, NT_DIMS, preferred_element_type=F32) * d_mat
    num = (jnp.dot(qk.astype(BF16), v_ref[...], preferred_element_type=F32)
           + inter_w * lax.dot_general(q, c_st.astype(BF16), NT_DIMS, preferred_element_type=F32))
    den = (jnp.sum(qk, axis=-1, keepdims=True)
           + inter_w * jnp.sum(q.astype(F32) * n_st, axis=-1, keepdims=True))
    hh = num / jnp.maximum(jnp.abs(den), jnp.exp(-m_out))
    hn = _rms(hh, g_ref[...])
    out_ref[...] = (hn * _sigmoid(o_ref[...])).astype(out_ref.dtype)

    a_row = g - bcum_row + li_row
    m_loc = jnp.max(a_row, axis=-1, keepdims=True)
    w_row = jnp.exp(a_row - m_loc)
    c_loc = jnp.dot((vt_ref[...].astype(F32) * w_row).astype(BF16), k, preferred_element_type=F32)
    n_loc = jnp.dot(jnp.broadcast_to(w_row, (8, L)).astype(BF16), k,
                    preferred_element_type=F32)[0:1, :]
    m_new = jnp.maximum(g + m_st, m_loc)
    decay = jnp.exp(g + m_st - m_new)
    fresh = jnp.exp(m_loc - m_new)
    c_scr[...] = decay * c_st + fresh * c_loc
    n_scr[...] = decay * n_st + fresh * n_loc
    m_scr[...] = m_new


def mlstm(qk, v, v_t, li, lf, o_gate, b_i, b_f, head_g):
    b, s, w = v.shape
    n_heads = w // LANES
    rows = v_t.shape[-1]
    grid_spec = pltpu.PrefetchScalarGridSpec(
        num_scalar_prefetch=2,
        grid=(b, n_heads, s // rows),
        in_specs=[pl.BlockSpec((None, rows, LANES), lambda bi, h, c, *_: (bi, c, h)),
                  pl.BlockSpec((None, rows, LANES), lambda bi, h, c, *_: (bi, c, n_heads + h)),
                  pl.BlockSpec((None, rows, LANES), lambda bi, h, c, *_: (bi, c, h)),
                  pl.BlockSpec((None, None, None, LANES, rows),
                               lambda bi, h, c, *_: (bi, h, c, 0, 0)),
                  pl.BlockSpec((None, None, None, 1, rows), lambda bi, h, c, *_: (bi, h, c, 0, 0)),
                  pl.BlockSpec((None, None, None, 1, rows), lambda bi, h, c, *_: (bi, h, c, 0, 0)),
                  pl.BlockSpec((None, rows, LANES), lambda bi, h, c, *_: (bi, c, h)),
                  pl.BlockSpec((None, 1, LANES), lambda bi, h, c, *_: (h, 0, 0))],
        out_specs=pl.BlockSpec((None, rows, LANES), lambda bi, h, c, *_: (bi, c, h)),
        scratch_shapes=[pltpu.VMEM((LANES, LANES), F32), pltpu.VMEM((1, LANES), F32),
                        pltpu.VMEM((1, 1), F32)])
    return pl.pallas_call(
        _mlstm_kernel,
        out_shape=jax.ShapeDtypeStruct((b, s, w), BF16),
        grid_spec=grid_spec,
        compiler_params=_params("parallel", "parallel", "arbitrary"),
        name="mlstm",
    )(b_i, b_f, qk, qk, v, v_t, li, lf, o_gate, head_g.reshape(n_heads, 1, LANES))


def _ff_chunks(tf):
    return [(c0, min(c0 + FF_CHUNK, tf)) for c0 in range(0, tf, FF_CHUNK)]


def _swiglu_accumulate(h_scr, wg_ref, wu_ref, wd_ref, act_scr, o_ref, after_chunk=None):
    h = h_scr[...]
    for c, (c0, c1) in enumerate(_ff_chunks(wg_ref.shape[1])):
        gate = jnp.dot(h, wg_ref[:, c0:c1], preferred_element_type=F32)
        up = jnp.dot(h, wu_ref[:, c0:c1], preferred_element_type=F32)
        act_scr[:, c0:c1] = (gate * _sigmoid(gate) * up).astype(BF16)
        if after_chunk is not None:
            after_chunk(c)
    o_ref[...] += jnp.dot(act_scr[...], wd_ref[...], preferred_element_type=F32)


def _ffn_kernel(x_ref, a_ref, b_ref, wo_ref, g_ref, wg_ref, wu_ref, wd_ref, o_ref, h_scr, act_scr):
    @pl.when(pl.program_id(1) == 0)
    def _():
        na = a_ref.shape[1]
        x = x_ref[...] + jnp.dot(a_ref[...], wo_ref[0:na, :], preferred_element_type=F32)
        x = x + jnp.dot(b_ref[...], wo_ref[na:, :], preferred_element_type=F32)
        h_scr[...] = _rms(x, g_ref[...]).astype(BF16)
        o_ref[...] = x

    _swiglu_accumulate(h_scr, wg_ref, wu_ref, wd_ref, act_scr, o_ref)


def out_proj_ffn(x, a, b_part, w_out, g, w_gate, w_up, w_down, layer, *, tm=1024, tf=1792):
    t, d = x.shape
    ff = w_gate.shape[2]
    return pl.pallas_call(
        _ffn_kernel,
        out_shape=jax.ShapeDtypeStruct((t, d), F32),
        grid=(t // tm, ff // tf),
        in_specs=[pl.BlockSpec((tm, d), lambda i, j: (i, 0)),
                  pl.BlockSpec((tm, a.shape[1]), lambda i, j: (i, 0)),
                  pl.BlockSpec((tm, b_part.shape[1]), lambda i, j: (i, 0)),
                  pl.BlockSpec(w_out.shape, lambda i, j: (0, 0)),
                  pl.BlockSpec((1, d), lambda i, j: (0, 0)),
                  pl.BlockSpec((None, d, tf), lambda i, j: (layer, 0, j)),
                  pl.BlockSpec((None, d, tf), lambda i, j: (layer, 0, j)),
                  pl.BlockSpec((None, tf, d), lambda i, j: (layer, j, 0))],
        out_specs=pl.BlockSpec((tm, d), lambda i, j: (i, 0)),
        scratch_shapes=[pltpu.VMEM((tm, d), BF16), pltpu.VMEM((tm, tf), BF16)],
        compiler_params=_params("parallel", "arbitrary"),
        name="out_proj_ffn",
    )(x, a, b_part, w_out, g.reshape(1, d), w_gate, w_up, w_down)


def _store_row_tiles(ref, x):
    rows, d = x.shape
    per = d // LANES
    for s in range(per):
        ref[pl.ds(s, rows, stride=per), :] = x[:, s * LANES:(s + 1) * LANES]


def _load_row_tiles(ref, rows, per):
    return [ref[pl.ds(s, rows, stride=per), :] for s in range(per)]


def _router_kernel(x_ref, a_ref, b_ref, wo_ref, g_ref, wr_ref, x1_ref, h_ref, route_ref):
    na = a_ref.shape[1]
    x1 = x_ref[...] + jnp.dot(a_ref[...], wo_ref[0:na, :], preferred_element_type=F32)
    x1 = x1 + jnp.dot(b_ref[...], wo_ref[na:, :], preferred_element_type=F32)
    x1_ref[...] = x1
    h = _rms(x1, g_ref[...])
    _store_row_tiles(h_ref, h)
    h_hi = h.astype(BF16)
    h_lo = (h - h_hi.astype(F32)).astype(BF16)
    hi_terms = jnp.dot(h_hi, wr_ref[...], preferred_element_type=F32)
    logits = (hi_terms[:, :LANES] + hi_terms[:, LANES:]
              + jnp.dot(h_lo, wr_ref[:, :LANES], preferred_element_type=F32))
    lane = lax.broadcasted_iota(jnp.int32, logits.shape, 1)
    logits = jnp.where(lane < N_EXPERTS, logits, NEG_INF)
    m1 = jnp.max(logits, axis=-1, keepdims=True)
    i1 = jnp.min(jnp.where(logits == m1, lane, LANES), axis=-1, keepdims=True)
    rest = jnp.where(lane == i1, NEG_INF, logits)
    m2 = jnp.max(rest, axis=-1, keepdims=True)
    i2 = jnp.min(jnp.where(rest == m2, lane, LANES), axis=-1, keepdims=True)
    e2 = jnp.exp(m2 - m1)
    g1 = 1.0 / (1.0 + e2)
    g2 = e2 / (1.0 + e2)
    route_ref[...] = jnp.where(lane == 0, i1.astype(F32),
                               jnp.where(lane == 1, i2.astype(F32),
                                         jnp.where(lane == 2, g1, jnp.where(lane == 3, g2, 0.0))))


def out_proj_router(x, a, b_part, w_out, g, w_router_padded, *, tm=512):
    t, d = x.shape
    per = d // LANES
    w_hi = w_router_padded.astype(BF16)
    w_lo = (w_router_padded - w_hi.astype(F32)).astype(BF16)
    w_pieces = jnp.concatenate([w_hi, w_lo], axis=1)
    return pl.pallas_call(
        _router_kernel,
        out_shape=[jax.ShapeDtypeStruct((t, d), F32),
                   jax.ShapeDtypeStruct((t * per, LANES), F32),
                   jax.ShapeDtypeStruct((t, LANES), F32)],
        grid=(t // tm,),
        in_specs=[pl.BlockSpec((tm, d), lambda i: (i, 0)),
                  pl.BlockSpec((tm, a.shape[1]), lambda i: (i, 0)),
                  pl.BlockSpec((tm, b_part.shape[1]), lambda i: (i, 0)),
                  pl.BlockSpec(w_out.shape, lambda i: (0, 0)),
                  pl.BlockSpec((1, d), lambda i: (0, 0)),
                  pl.BlockSpec((d, 2 * LANES), lambda i: (0, 0))],
        out_specs=[pl.BlockSpec((tm, d), lambda i: (i, 0)),
                   pl.BlockSpec((tm * per, LANES), lambda i: (i, 0)),
                   pl.BlockSpec((tm, LANES), lambda i: (i, 0))],
        compiler_params=_params("parallel"),
        name="out_proj_router",
    )(x, a, b_part, w_out, g.reshape(1, d), w_pieces)


def _expert_kernel(be_ref, nused_ref, tok_cur_ref, tok_next_ref, h_hbm, wg_ref, wu_ref, wd_ref,
                   o_ref, xbuf, sems, h_scr, act_scr, acc_scr, *, tm):
    i = pl.program_id(0)
    j = pl.program_id(1)
    n_used = nused_ref[0]
    used = i < n_used
    per = h_scr.shape[1] // LANES

    def issue_gather(tok_ref, slot):
        def body(rr, _):
            for prio in range(DMA_PRIORITIES):
                r = rr * DMA_PRIORITIES + prio
                src = pl.multiple_of(tok_ref[0, r] * per, per)
                dst = pl.multiple_of(r * per, per)
                pltpu.make_async_copy(h_hbm.at[pl.ds(src, per)], xbuf.at[slot, pl.ds(dst, per)],
                                      sems.at[slot]).start(priority=prio)
            return 0
        lax.fori_loop(0, tm // DMA_PRIORITIES, body, 0, unroll=4)

    @pl.when(jnp.logical_and(i == 0, j == 0))
    def _():
        issue_gather(tok_cur_ref, 0)

    @pl.when(jnp.logical_and(used, j == 0))
    def _():
        slot = i % 2
        pltpu.make_async_copy(xbuf.at[slot], xbuf.at[slot], sems.at[slot]).wait()
        for s, piece in enumerate(_load_row_tiles(xbuf.at[slot], tm, per)):
            h_scr[:, s * LANES:(s + 1) * LANES] = piece.astype(BF16)
        acc_scr[...] = jnp.zeros_like(acc_scr)

    @pl.when(used)
    def _():
        _swiglu_accumulate(h_scr, wg_ref, wu_ref, wd_ref, act_scr, acc_scr)

    @pl.when(jnp.logical_and(i + 1 < n_used, j == 0))
    def _():
        issue_gather(tok_next_ref, (i + 1) % 2)

    @pl.when(jnp.logical_and(used, j == pl.num_programs(1) - 1))
    def _():
        _store_row_tiles(o_ref, acc_scr[...])

    @pl.when(jnp.logical_and(jnp.logical_not(used), j == 0))
    def _():
        o_ref[...] = jnp.zeros_like(o_ref)


def expert_ffn(h, row_tok, block_expert, n_used, w_gate, w_up, w_down, layer, *, tm, tf=1792):
    n_rows = row_tok.shape[0]
    d = w_gate.shape[2]
    per = d // LANES
    ff = w_gate.shape[3]
    n_blocks = n_rows // tm
    last_j = ff // tf - 1

    def up_block(i, j, be, nu):
        return (layer, be[jnp.minimum(i, nu[0] - 1)], 0, jnp.where(i < nu[0], j, last_j))

    def down_block(i, j, be, nu):
        return (layer, be[jnp.minimum(i, nu[0] - 1)], jnp.where(i < nu[0], j, last_j), 0)

    grid_spec = pltpu.PrefetchScalarGridSpec(
        num_scalar_prefetch=2,
        grid=(n_blocks, ff // tf),
        in_specs=[pl.BlockSpec((None, 1, tm), lambda i, j, be, nu: (i, 0, 0),
                               memory_space=pltpu.SMEM),
                  pl.BlockSpec((None, 1, tm),
                               lambda i, j, be, nu: (jnp.minimum(i + 1, n_blocks - 1), 0, 0),
                               memory_space=pltpu.SMEM),
                  pl.BlockSpec(memory_space=pl.ANY),
                  pl.BlockSpec((None, None, d, tf), up_block),
                  pl.BlockSpec((None, None, d, tf), up_block),
                  pl.BlockSpec((None, None, tf, d), down_block)],
        out_specs=pl.BlockSpec((tm * per, LANES), lambda i, j, be, nu: (i, 0)),
        scratch_shapes=[pltpu.VMEM((2, tm * per, LANES), F32), pltpu.SemaphoreType.DMA((2,)),
                        pltpu.VMEM((tm, d), BF16), pltpu.VMEM((tm, tf), BF16),
                        pltpu.VMEM((tm, d), F32)])
    tok_blocks = row_tok.reshape(n_blocks, 1, tm)
    return pl.pallas_call(
        functools.partial(_expert_kernel, tm=tm),
        out_shape=jax.ShapeDtypeStruct((n_rows * per, LANES), F32),
        grid_spec=grid_spec,
        compiler_params=_params("arbitrary", "arbitrary"),
        name="expert_ffn",
    )(block_expert, n_used, tok_blocks, tok_blocks, h, w_gate, w_up, w_down)


def _combine_kernel(pos_cur_ref, pos_next_ref, x_ref, route_ref, y_ref, *rest, rows, final):
    if final:
        g_ref, o_ref, buf, sems = rest
    else:
        o_ref, buf, sems = rest
    i = pl.program_id(0)
    n = pl.num_programs(0)
    per = x_ref.shape[1] // LANES

    def issue(pos_ref, slot):
        def body(r, _):
            dst = pl.multiple_of(r * per, per)
            for kk in range(TOP_K):
                src = pl.multiple_of(pos_ref[0, TOP_K * r + kk] * per, per)
                pltpu.make_async_copy(y_ref.at[pl.ds(src, per)], buf.at[slot, kk, pl.ds(dst, per)],
                                      sems.at[slot]).start(priority=kk % DMA_PRIORITIES)
            return 0
        lax.fori_loop(0, rows, body, 0, unroll=4)

    @pl.when(i == 0)
    def _():
        issue(pos_cur_ref, 0)

    @pl.when(i + 1 < n)
    def _():
        issue(pos_next_ref, (i + 1) % 2)

    slot = i % 2
    pltpu.make_async_copy(buf.at[slot], buf.at[slot], sems.at[slot]).wait()
    route = route_ref[...]
    g1 = route[:, 2:3]
    g2 = route[:, 3:4]
    y1 = jnp.concatenate(_load_row_tiles(buf.at[slot, 0], rows, per), axis=-1)
    y2 = jnp.concatenate(_load_row_tiles(buf.at[slot, 1], rows, per), axis=-1)
    out = x_ref[...] + g1 * y1 + g2 * y2
    o_ref[...] = _rms(out, g_ref[...]) if final else out


def combine_residual(x, route, y_rows, pos, final_g=None, *, rows=256):
    t, d = x.shape
    final = final_g is not None
    n_steps = t // rows
    grid_spec = pltpu.PrefetchScalarGridSpec(
        num_scalar_prefetch=0,
        grid=(n_steps,),
        in_specs=[pl.BlockSpec((None, 1, TOP_K * rows), lambda i: (i, 0, 0),
                               memory_space=pltpu.SMEM),
                  pl.BlockSpec((None, 1, TOP_K * rows),
                               lambda i: (jnp.minimum(i + 1, n_steps - 1), 0, 0),
                               memory_space=pltpu.SMEM),
                  pl.BlockSpec((rows, d), lambda i: (i, 0)),
                  pl.BlockSpec((rows, LANES), lambda i: (i, 0)),
                  pl.BlockSpec(memory_space=pl.ANY)]
                 + ([pl.BlockSpec((1, d), lambda i: (0, 0))] if final else []),
        out_specs=pl.BlockSpec((rows, d), lambda i: (i, 0)),
        scratch_shapes=[pltpu.VMEM((2, TOP_K, rows * (d // LANES), LANES), F32),
                        pltpu.SemaphoreType.DMA((2,))])
    pos_blocks = pos.reshape(n_steps, 1, TOP_K * rows)
    extra = (final_g.reshape(1, d),) if final else ()
    return pl.pallas_call(
        functools.partial(_combine_kernel, rows=rows, final=final),
        out_shape=jax.ShapeDtypeStruct((t, d), F32),
        grid_spec=grid_spec,
        compiler_params=_params("arbitrary"),
        name="combine_residual",
    )(pos_blocks, pos_blocks, x, route, y_rows, *extra)


def _pad_cols(w, n):
    return jnp.pad(w, ((0, 0), (0, n - w.shape[1])))


def _scale_q_cols(w_qkv, width):
    scale = jnp.concatenate([jnp.full((width,), LOG2E * HEAD_DIM ** -0.5, F32),
                             jnp.ones((2 * width,), F32)])
    return w_qkv * scale


def even_layer(x, b, s, g_mix, g_ffn, w_in, b_forget, pool_w, pool_scale, w_out,
               w_gate, w_up, w_down, e, *, fox_tq=512):
    t, d = x.shape
    pool_width = pool_w.shape[0] * pool_w.shape[1]
    n_heads = b_forget.shape[0]
    fox_width = n_heads * HEAD_DIM
    w_a = w_in[:, :pool_width].astype(BF16)
    w_qkv = _scale_q_cols(w_in[:, pool_width:pool_width + 3 * fox_width], fox_width).astype(BF16)
    w_f = _pad_cols(w_in[:, pool_width + 3 * fox_width:], LANES).astype(BF16)
    qkv, a_in, f_pre = norm_proj(x, g_mix, [w_qkv, w_a, w_f], [BF16, F32, F32])

    pool_bd = jax.scipy.linalg.block_diag(*[pool_w[i] for i in range(pool_w.shape[0])]).astype(BF16)
    a_out, cum = pool_and_cum(a_in.reshape(b, s, pool_width), f_pre.reshape(b, s, LANES),
                              _pad_cols(b_forget.reshape(1, n_heads), LANES), pool_bd,
                              pool_scale.reshape(1, pool_width))
    cum_blocks = cum[:, :, :n_heads].reshape(b, s // fox_tq, fox_tq, n_heads // 2, 2)
    cum_blocks = cum_blocks.transpose(0, 3, 1, 4, 2)
    attn = fox_attention(qkv.reshape(b, s, 3 * fox_width), cum_blocks, tq=fox_tq)
    return out_proj_ffn(x, a_out.reshape(t, pool_width), attn.reshape(t, fox_width),
                        w_out.astype(BF16), g_ffn, w_gate, w_up, w_down, e)


def odd_layer(x, b, s, g_mix, g_ffn, w_in, b_igate, b_fgate, conv_w, head_norm_g, w_out,
              w_router, w_gate, w_up, w_down, e, final_g=None):
    c_out, d_out = odd_mixers(x, b, s, g_mix, w_in, b_igate, b_fgate, conv_w, head_norm_g)
    return moe_block(x, c_out, d_out, w_out.astype(BF16), g_ffn, w_router, w_gate, w_up, w_down,
                     e, final_g)


def odd_mixers(x, b, s, g_mix, w_in, b_igate, b_fgate, conv_w, head_norm_g):
    t, d = x.shape
    n_ml = b_igate.shape[0]
    ml_width = n_ml * LANES
    sb_width = d - ml_width
    cols = [3 * sb_width, 2 * ml_width, ml_width, ml_width]
    edges = [0]
    for c in cols:
        edges.append(edges[-1] + c)
    w_sb, w_mqk, w_mv, w_mo = [w_in[:, edges[i]:edges[i + 1]] for i in range(4)]
    w_sb = _scale_q_cols(w_sb, sb_width)
    w_sb, w_mqk, w_mv, w_mo = [w.astype(BF16) for w in (w_sb, w_mqk, w_mv, w_mo)]
    w_gates = _pad_cols(w_in[:, edges[-1]:], LANES).astype(BF16)
    sqkv, ml_qk, ml_v, ml_o, gates = norm_proj(
        x, g_mix, [w_sb, w_mqk, w_mv, w_mo, w_gates], [BF16, F32, BF16, F32, F32])

    c_out = sb_attention(sqkv.reshape(b, s, 3 * sb_width))

    k_scale = jnp.concatenate([jnp.ones((ml_width,), F32),
                               jnp.full((ml_width,), LANES ** -0.5, F32)]).reshape(1, 2 * ml_width)
    qk = conv_silu(ml_qk.reshape(b, s, 2 * ml_width), conv_w, k_scale)
    nb = s // MLSTM_BLOCK
    v3 = ml_v.reshape(b, s, ml_width)
    v_t = v3.reshape(b, nb, MLSTM_BLOCK, n_ml, LANES).transpose(0, 3, 1, 4, 2)
    gate_rows = gates[:, :2 * n_ml].reshape(b, nb, MLSTM_BLOCK, 2 * n_ml).transpose(0, 3, 1, 2)
    gate_rows = gate_rows.reshape(b, 2 * n_ml, nb, 1, MLSTM_BLOCK)
    d_out = mlstm(qk, v3, v_t, gate_rows[:, :n_ml], gate_rows[:, n_ml:],
                  ml_o.reshape(b, s, ml_width), b_igate, b_fgate, head_norm_g)
    return c_out.reshape(t, sb_width), d_out.reshape(t, ml_width)


def moe_block(x, mix_a, mix_b, w_out, g_ffn, w_router, w_gate, w_up, w_down, e, final_g=None, *,
              expert_rows=1024):
    t, d = x.shape
    x, h, route = out_proj_router(x, mix_a, mix_b, w_out, g_ffn, _pad_cols(w_router, LANES))
    flat_e = route[:, :TOP_K].astype(jnp.int32).reshape(-1)
    n_assign = t * TOP_K
    onehot = jax.nn.one_hot(flat_e, N_EXPERTS, dtype=jnp.int32)
    rank = jnp.sum((jnp.cumsum(onehot, axis=0) - onehot) * onehot, axis=-1)
    counts = jnp.sum(onehot, axis=0)
    padded = (counts + expert_rows - 1) // expert_rows * expert_rows
    group_end = jnp.cumsum(padded)
    pos = (group_end[flat_e] - padded[flat_e] + rank).astype(jnp.int32)
    n_rows = n_assign + N_EXPERTS * expert_rows
    n_blocks = n_rows // expert_rows
    flat_tok = jnp.arange(n_assign, dtype=jnp.int32) // TOP_K
    row_tok = jnp.zeros((n_rows,), jnp.int32).at[pos].set(flat_tok)
    block_expert = jnp.minimum(
        jnp.searchsorted(group_end, jnp.arange(n_blocks) * expert_rows, side='right'),
        N_EXPERTS - 1).astype(jnp.int32)
    n_used = (group_end[-1:] // expert_rows).astype(jnp.int32)

    y_rows = expert_ffn(h, row_tok, block_expert, n_used, w_gate, w_up, w_down, e, tm=expert_rows)
    return combine_residual(x, route, y_rows, pos, final_g)


def kernel(x, norm_mix_g, norm_ffn_g, norm_final_g, ev_w_in, ev_b_forget, ev_pool_w, ev_pool_scale, ev_w_out, ffn_w_gate, ffn_w_up, ffn_w_down, od_w_in, od_b_igate, od_b_fgate, od_conv_w, od_head_norm_g, od_w_out, moe_w_router, moe_w_gate, moe_w_up, moe_w_down):
    b, s, d = x.shape
    depth = norm_mix_g.shape[0]
    xt = x.reshape(b * s, d)
    ffn_w = [w.astype(BF16) for w in (ffn_w_gate, ffn_w_up, ffn_w_down)]
    moe_w = [w.astype(BF16) for w in (moe_w_gate, moe_w_up, moe_w_down)]
    for layer in range(depth):
        e = layer // 2
        if layer % 2 == 0:
            xt = even_layer(xt, b, s, norm_mix_g[layer], norm_ffn_g[layer], ev_w_in[e],
                            ev_b_forget[e], ev_pool_w[e], ev_pool_scale[e], ev_w_out[e],
                            *ffn_w, e)
        else:
            xt = odd_layer(xt, b, s, norm_mix_g[layer], norm_ffn_g[layer], od_w_in[e],
                           od_b_igate[e], od_b_fgate[e], od_conv_w[e], od_head_norm_g[e],
                           od_w_out[e], moe_w_router[e], *moe_w, e,
                           norm_final_g if layer == depth - 1 else None)
    assert depth % 2 == 0
    return xt.reshape(b, s, d)
```

```python
import functools

import jax
import jax.numpy as jnp
from jax import lax
from jax.experimental import pallas as pl
from jax.experimental.pallas import tpu as pltpu

F32 = jnp.float32
BF16 = jnp.bfloat16
EPS = 1e-6
NEG_INF = float("-inf")

LANES = 128
HEAD_DIM = 64
POOL_WINDOWS = (2, 4, 8, 16)
POOL_HALO = 16
CONV_WIDTH = 4
CONV_HALO = 8
FF_CHUNK = 512
MLSTM_BLOCK = 512
FOX_BLOCKS_PER_TRIP = 3
SB_BLOCKS_PER_TRIP = 4
DMA_PRIORITIES = 2
N_EXPERTS = 8
TOP_K = 2
VMEM_LIMIT = 56 * 1024 * 1024

LOG2E = 1.4426950408889634
SOFTPLUS2_CLAMP = 64.0
HIGHEST = lax.Precision.HIGHEST
NT_DIMS = (((1,), (1,)), ((), ()))


def _params(*semantics):
    return pltpu.CompilerParams(dimension_semantics=semantics, vmem_limit_bytes=VMEM_LIMIT)


def _rms(x, g):
    return x * lax.rsqrt(jnp.mean(x * x, axis=-1, keepdims=True) + EPS) * g


def _log_sigmoid(x):
    return jnp.minimum(x, 0.0) - jnp.log(1.0 + jnp.exp(-jnp.abs(x)))


def _sigmoid(x):
    return 1.0 / (1.0 + jnp.exp(-x))


def _norm_proj_kernel(x_ref, g_ref, *refs, n_out, col_chunk):
    w_refs, o_refs = refs[:n_out], refs[n_out:]
    h = _rms(x_ref[...], g_ref[...]).astype(BF16)
    for w_ref, o_ref in zip(w_refs, o_refs):
        n = w_ref.shape[1]
        for c0 in range(0, n, col_chunk):
            c1 = min(c0 + col_chunk, n)
            o_ref[:, c0:c1] = jnp.dot(
                h, w_ref[:, c0:c1], preferred_element_type=F32).astype(o_ref.dtype)


def norm_proj(x, g, weights, out_dtypes, *, tm=512, col_chunk=512):
    t, d = x.shape
    n_out = len(weights)
    return pl.pallas_call(
        functools.partial(_norm_proj_kernel, n_out=n_out, col_chunk=col_chunk),
        out_shape=[jax.ShapeDtypeStruct((t, w.shape[1]), dt) for w, dt in zip(weights, out_dtypes)],
        grid=(t // tm,),
        in_specs=[pl.BlockSpec((tm, d), lambda i: (i, 0)),
                  pl.BlockSpec((1, d), lambda i: (0, 0))]
                 + [pl.BlockSpec(w.shape, lambda i: (0, 0)) for w in weights],
        out_specs=[pl.BlockSpec((tm, w.shape[1]), lambda i: (i, 0)) for w in weights],
        compiler_params=_params("parallel"),
        name="norm_proj",
    )(x, g.reshape(1, d), *weights)


def _pool_kernel(a_ref, f_ref, bf_ref, w_ref, scale_ref, o_ref, cum_ref, *, rows):
    s, width = a_ref.shape
    group = lax.broadcasted_iota(jnp.int32, (1, width), 1) // (width // len(POOL_WINDOWS))
    window = jnp.where(group == 0, POOL_WINDOWS[0],
                       jnp.where(group == 1, POOL_WINDOWS[1],
                                 jnp.where(group == 2, POOL_WINDOWS[2], POOL_WINDOWS[3])))
    tri = (lax.broadcasted_iota(jnp.int32, (rows, rows), 0)
           >= lax.broadcasted_iota(jnp.int32, (rows, rows), 1)).astype(F32)
    carry = jnp.zeros((1, f_ref.shape[1]), F32)
    for r in range(s // rows):
        r0 = r * rows
        if r == 0:
            ext = jnp.concatenate([jnp.zeros((POOL_HALO, width), F32), a_ref[0:rows, :]], axis=0)
        else:
            ext = a_ref[r0 - POOL_HALO:r0 + rows, :]
        s2 = ext + pltpu.roll(ext, 1, 0)
        s4 = s2 + pltpu.roll(s2, 2, 0)
        s8 = s4 + pltpu.roll(s4, 4, 0)
        s16 = s8 + pltpu.roll(s8, 8, 0)
        wsum = jnp.where(group == 0, s2, jnp.where(group == 1, s4, jnp.where(group == 2, s8, s16)))
        wsum = wsum[POOL_HALO:, :]
        a = ext[POOL_HALO:, :]
        t1 = r0 + 1 + lax.broadcasted_iota(jnp.int32, (rows, 1), 0)
        count = jnp.minimum(t1, window).astype(F32)
        pooled = wsum / count - a
        mixed = jnp.dot(pooled.astype(BF16), w_ref[...], preferred_element_type=F32)
        o_ref[r0:r0 + rows, :] = (mixed * scale_ref[...]).astype(o_ref.dtype)
        lf = _log_sigmoid(f_ref[r0:r0 + rows, :] + bf_ref[...]) * LOG2E
        cum = jnp.dot(tri, lf, preferred_element_type=F32, precision=HIGHEST) + carry
        cum_ref[r0:r0 + rows, :] = cum
        carry = cum[rows - 1:rows, :]


def pool_and_cum(a_in, f_pre, b_forget, pool_w_bd, pool_scale, *, rows=256):
    b, s, width = a_in.shape
    fw = f_pre.shape[-1]
    return pl.pallas_call(
        functools.partial(_pool_kernel, rows=rows),
        out_shape=[jax.ShapeDtypeStruct((b, s, width), BF16),
                   jax.ShapeDtypeStruct((b, s, fw), F32)],
        grid=(b,),
        in_specs=[pl.BlockSpec((None, s, width), lambda i: (i, 0, 0)),
                  pl.BlockSpec((None, s, fw), lambda i: (i, 0, 0)),
                  pl.BlockSpec((1, fw), lambda i: (0, 0)),
                  pl.BlockSpec((width, width), lambda i: (0, 0)),
                  pl.BlockSpec((1, width), lambda i: (0, 0))],
        out_specs=[pl.BlockSpec((None, s, width), lambda i: (i, 0, 0)),
                   pl.BlockSpec((None, s, fw), lambda i: (i, 0, 0))],
        compiler_params=_params("parallel"),
        name="pool_and_cum",
    )(a_in, f_pre, b_forget, pool_w_bd, pool_scale)


def _fox_kernel(q_ref, k_ref, v_ref, cum_ref, o_ref, *, tq):
    i = pl.program_id(2)
    lane = lax.broadcasted_iota(jnp.int32, (1, LANES), 1)
    first = lane < HEAD_DIM
    q = q_ref[...]
    q_heads = (jnp.where(first, q, jnp.zeros_like(q)), jnp.where(first, jnp.zeros_like(q), q))
    row = lax.broadcasted_iota(jnp.int32, (tq, tq), 0)
    col = lax.broadcasted_iota(jnp.int32, (tq, tq), 1)

    def block(j, carry, masked):
        ms, accs = carry
        start = pl.multiple_of(j * tq, tq)
        k = k_ref[pl.ds(start, tq), :]
        v = v_ref[pl.ds(start, tq), :]
        ones = jnp.ones_like(v)
        v_heads = (jnp.where(first, v, ones), jnp.where(first, ones, v))
        ck = cum_ref[j]
        new_ms, new_accs = [], []
        scs = [lax.dot_general(q_heads[h], k, NT_DIMS, preferred_element_type=F32)
               for h in range(2)]
        for h in range(2):
            sc = scs[h] - ck[h:h + 1, :]
            if masked:
                sc = jnp.where(col <= row, sc, NEG_INF)
            m_new = jnp.maximum(ms[h], jnp.max(sc, axis=-1, keepdims=True))
            alpha = jnp.exp2(ms[h] - m_new)
            p = jnp.exp2((sc - m_new).astype(BF16))
            new_ms.append(m_new)
            new_accs.append(alpha * accs[h] + jnp.dot(p, v_heads[h], preferred_element_type=F32))
        return tuple(new_ms), tuple(new_accs)

    init = ((jnp.full((tq, 1), NEG_INF, F32),) * 2, (jnp.zeros((tq, LANES), F32),) * 2)
    def run(first_block, count, c):
        for d in range(count):
            c = block(first_block + d, c, False)
        return c

    n_trips = i // FOX_BLOCKS_PER_TRIP
    carry = lax.fori_loop(
        0, n_trips, lambda t, c: run(FOX_BLOCKS_PER_TRIP * t, FOX_BLOCKS_PER_TRIP, c), init)
    def tail(n_left):
        return lambda c: block(i, run(i - n_left, n_left, c), True)

    left = i - FOX_BLOCKS_PER_TRIP * n_trips
    _, accs = lax.switch(left, [tail(n) for n in range(FOX_BLOCKS_PER_TRIP)], carry)
    out0 = accs[0] / accs[0][:, HEAD_DIM:HEAD_DIM + 1]
    out1 = accs[1] / accs[1][:, 0:1]
    o_ref[...] = jnp.where(first, out0, out1).astype(o_ref.dtype)


def fox_attention(qkv, cum_blocks, *, tq=512):
    b, s, w3 = qkv.shape
    n_pairs = w3 // 3 // LANES
    return pl.pallas_call(
        functools.partial(_fox_kernel, tq=tq),
        out_shape=jax.ShapeDtypeStruct((b, s, w3 // 3), BF16),
        grid=(b, n_pairs, s // tq),
        in_specs=[pl.BlockSpec((None, tq, LANES), lambda bi, p, i: (bi, i, p)),
                  pl.BlockSpec((None, s, LANES), lambda bi, p, i: (bi, 0, n_pairs + p)),
                  pl.BlockSpec((None, s, LANES), lambda bi, p, i: (bi, 0, 2 * n_pairs + p)),
                  pl.BlockSpec((None, None, s // tq, 2, tq), lambda bi, p, i: (bi, p, 0, 0, 0))],
        out_specs=pl.BlockSpec((None, tq, LANES), lambda bi, p, i: (bi, i, p)),
        compiler_params=_params("parallel", "parallel", "arbitrary"),
        name="fox_attention",
    )(qkv, qkv, qkv, cum_blocks)


def _sb_kernel(q_ref, k_ref, v_ref, o_ref, *, tq, tk):
    i = pl.program_id(2)
    per = tq // tk
    lane = lax.broadcasted_iota(jnp.int32, (1, LANES), 1)
    first = lane < HEAD_DIM
    q = q_ref[...]
    q_heads = (jnp.where(first, q, jnp.zeros_like(q)), jnp.where(first, jnp.zeros_like(q), q))
    row = lax.broadcasted_iota(jnp.int32, (tq, tk), 0)
    col = lax.broadcasted_iota(jnp.int32, (tq, tk), 1)
    neg_suffix = jnp.where(lax.broadcasted_iota(jnp.int32, (tk, tk), 0)
                           > lax.broadcasted_iota(jnp.int32, (tk, tk), 1), -1.0, 0.0).astype(BF16)

    def logits(j, r0=0):
        start = pl.multiple_of(j * tk, tk)
        k = k_ref[pl.ds(start, tk), :]
        return tuple(lax.dot_general(q_heads[h][r0:], k, NT_DIMS, preferred_element_type=F32)
                     for h in range(2))

    def block(j, zs, carry, diagonal, r0=0):
        used, acc = carry
        start = pl.multiple_of(j * tk, tk)
        v = v_ref[pl.ds(start, tk), :]
        if diagonal:
            valid = (col < row)[:tq - r0]
        new_used, pvs = [], []
        for h in range(2):
            z = zs[h]
            sp = jnp.maximum(z, jnp.log2(1.0 + jnp.exp2(jnp.minimum(z, SOFTPLUS2_CLAMP))))
            if diagonal:
                sp = jnp.where(valid, sp, 0.0)
            later_in = jnp.dot(sp.astype(BF16), neg_suffix, preferred_element_type=F32)
            w = jnp.exp2((z - sp + (later_in - used[h][r0:])).astype(BF16))
            if diagonal:
                w = jnp.where(valid, w, jnp.zeros_like(w))
            pvs.append(jnp.dot(w, v, preferred_element_type=F32))
            grown = used[h][r0:] + (sp[:, 0:1] - later_in[:, 0:1])
            new_used.append(jnp.concatenate([used[h][:r0], grown], axis=0) if r0 else grown)
        pv = acc[r0:] + jnp.where(first, pvs[0], pvs[1])
        return tuple(new_used), (jnp.concatenate([acc[:r0], pv], axis=0) if r0 else pv)

    def run(first_block, count, carry):
        for d in range(count):
            carry = block(first_block - d, logits(first_block - d), carry, False)
        return carry

    assert per == 2 and SB_BLOCKS_PER_TRIP == 4
    n_full = i * per
    left = n_full % SB_BLOCKS_PER_TRIP

    def head(n_left):
        def f(carry):
            for d in range(per):
                j, r0 = (i + 1) * per - 1 - d, (per - 1 - d) * tk
                carry = block(j, logits(j, r0), carry, True, r0)
            return run(n_full - 1, n_left, carry)
        return f

    carry = ((jnp.zeros((tq, 1), F32),) * 2, jnp.zeros((tq, LANES), F32))
    carry = lax.switch(left // 2, [head(0), head(2)], carry)
    rest = n_full - left
    _, acc = lax.fori_loop(
        0, rest // SB_BLOCKS_PER_TRIP,
        lambda t, c: run(rest - 1 - SB_BLOCKS_PER_TRIP * t, SB_BLOCKS_PER_TRIP, c), carry)
    o_ref[...] = acc.astype(o_ref.dtype)


def sb_attention(qkv, *, tq=512, tk=256):
    b, s, w3 = qkv.shape
    n_pairs = w3 // 3 // LANES
    return pl.pallas_call(
        functools.partial(_sb_kernel, tq=tq, tk=tk),
        out_shape=jax.ShapeDtypeStruct((b, s, w3 // 3), BF16),
        grid=(b, n_pairs, s // tq),
        in_specs=[pl.BlockSpec((None, tq, LANES), lambda bi, p, i: (bi, i, p)),
                  pl.BlockSpec((None, s, LANES), lambda bi, p, i: (bi, 0, n_pairs + p)),
                  pl.BlockSpec((None, s, LANES), lambda bi, p, i: (bi, 0, 2 * n_pairs + p))],
        out_specs=pl.BlockSpec((None, tq, LANES), lambda bi, p, i: (bi, i, p)),
        compiler_params=_params("parallel", "parallel", "arbitrary"),
        name="sb_attention",
    )(qkv, qkv, qkv)


def _conv_kernel(x_ref, halo_ref, w_ref, colscale_ref, o_ref):
    i = pl.program_id(1)
    halo = jnp.where(i == 0, 0.0, halo_ref[...])
    ext = jnp.concatenate([halo, x_ref[...]], axis=0)
    y = ext * w_ref[CONV_WIDTH - 1:CONV_WIDTH, :]
    for back in range(1, CONV_WIDTH):
        y = y + pltpu.roll(ext, back, 0) * w_ref[CONV_WIDTH - 1 - back:CONV_WIDTH - back, :]
    y = y[CONV_HALO:, :]
    o_ref[...] = (y * _sigmoid(y) * colscale_ref[...]).astype(o_ref.dtype)


def conv_silu(x, conv_w, colscale, *, rows=512):
    b, s, c = x.shape
    per = rows // CONV_HALO
    return pl.pallas_call(
        _conv_kernel,
        out_shape=jax.ShapeDtypeStruct((b, s, c), BF16),
        grid=(b, s // rows),
        in_specs=[pl.BlockSpec((None, rows, c), lambda bi, i: (bi, i, 0)),
                  pl.BlockSpec((None, CONV_HALO, c),
                               lambda bi, i: (bi, jnp.maximum(i * per - 1, 0), 0)),
                  pl.BlockSpec((CONV_WIDTH, c), lambda bi, i: (0, 0)),
                  pl.BlockSpec((1, c), lambda bi, i: (0, 0))],
        out_specs=pl.BlockSpec((None, rows, c), lambda bi, i: (bi, i, 0)),
        compiler_params=_params("parallel", "parallel"),
        name="conv_silu",
    )(x, x, conv_w, colscale)


def _mlstm_kernel(bi_ref, bf_ref, q_ref, k_ref, v_ref, vt_ref, li_ref, lf_ref, o_ref, g_ref,
                  out_ref, c_scr, n_scr, m_scr):
    head = pl.program_id(1)
    step = pl.program_id(2)
    L = q_ref.shape[0]

    @pl.when(step == 0)
    def _():
        c_scr[...] = jnp.zeros_like(c_scr)
        n_scr[...] = jnp.zeros_like(n_scr)
        m_scr[...] = jnp.zeros_like(m_scr)

    r_idx = lax.broadcasted_iota(jnp.int32, (L, L), 0)
    c_idx = lax.broadcasted_iota(jnp.int32, (L, L), 1)
    lower = r_idx >= c_idx
    tri_u = (r_idx <= c_idx).astype(F32)

    q = q_ref[...]
    k = k_ref[...]
    c_st = c_scr[...]
    n_st = n_scr[...]
    m_st = m_scr[...]
    lf_row = _log_sigmoid(lf_ref[...] + bf_ref[head])
    li_row = li_ref[...] + bi_ref[head]
    bcum_row = jnp.dot(jnp.broadcast_to(lf_row, (8, L)), tri_u, preferred_element_type=F32,
                       precision=HIGHEST)[0:1, :]
    bcum_col = jnp.broadcast_to(bcum_row, (LANES, L)).T[:, 0:1]
    g = jnp.sum(lf_row, axis=-1, keepdims=True)

    log_d = jnp.where(lower, bcum_col - bcum_row + li_row, NEG_INF)
    inter = bcum_col + m_st
    m_out = jnp.maximum(inter, jnp.max(log_d, axis=-1, keepdims=True))
    d_mat = jnp.exp(log_d - m_out)
    inter_w = jnp.exp(inter - m_out)
    qk = lax.dot_general(q, k, NT_DIMS, preferred_element_type=F32) * d_mat
    num = (jnp.dot(qk.astype(BF16), v_ref[...], preferred_element_type=F32)
           + inter_w * lax.dot_general(q, c_st.astype(BF16), NT_DIMS, preferred_element_type=F32))
    den = (jnp.sum(qk, axis=-1, keepdims=True)
           + inter_w * jnp.sum(q.astype(F32) * n_st, axis=-1, keepdims=True))
    hh = num / jnp.maximum(jnp.abs(den), jnp.exp(-m_out))
    hn = _rms(hh, g_ref[...])
    out_ref[...] = (hn * _sigmoid(o_ref[...])).astype(out_ref.dtype)

    a_row = g - bcum_row + li_row
    m_loc = jnp.max(a_row, axis=-1, keepdims=True)
    w_row = jnp.exp(a_row - m_loc)
    c_loc = jnp.dot((vt_ref[...].astype(F32) * w_row).astype(BF16), k, preferred_element_type=F32)
    n_loc = jnp.dot(jnp.broadcast_to(w_row, (8, L)).astype(BF16), k,
                    preferred_element_type=F32)[0:1, :]
    m_new = jnp.maximum(g + m_st, m_loc)
    decay = jnp.exp(g + m_st - m_new)
    fresh = jnp.exp(m_loc - m_new)
    c_scr[...] = decay * c_st + fresh * c_loc
    n_scr[...] = decay * n_st + fresh * n_loc
    m_scr[...] = m_new


def mlstm(qk, v, v_t, li, lf, o_gate, b_i, b_f, head_g):
    b, s, w = v.shape
    n_heads = w // LANES
    rows = v_t.shape[-1]
    grid_spec = pltpu.PrefetchScalarGridSpec(
        num_scalar_prefetch=2,
        grid=(b, n_heads, s // rows),
        in_specs=[pl.BlockSpec((None, rows, LANES), lambda bi, h, c, *_: (bi, c, h)),
                  pl.BlockSpec((None, rows, LANES), lambda bi, h, c, *_: (bi, c, n_heads + h)),
                  pl.BlockSpec((None, rows, LANES), lambda bi, h, c, *_: (bi, c, h)),
                  pl.BlockSpec((None, None, None, LANES, rows),
                               lambda bi, h, c, *_: (bi, h, c, 0, 0)),
                  pl.BlockSpec((None, None, None, 1, rows), lambda bi, h, c, *_: (bi, h, c, 0, 0)),
                  pl.BlockSpec((None, None, None, 1, rows), lambda bi, h, c, *_: (bi, h, c, 0, 0)),
                  pl.BlockSpec((None, rows, LANES), lambda bi, h, c, *_: (bi, c, h)),
                  pl.BlockSpec((None, 1, LANES), lambda bi, h, c, *_: (h, 0, 0))],
        out_specs=pl.BlockSpec((None, rows, LANES), lambda bi, h, c, *_: (bi, c, h)),
        scratch_shapes=[pltpu.VMEM((LANES, LANES), F32), pltpu.VMEM((1, LANES), F32),
                        pltpu.VMEM((1, 1), F32)])
    return pl.pallas_call(
        _mlstm_kernel,
        out_shape=jax.ShapeDtypeStruct((b, s, w), BF16),
        grid_spec=grid_spec,
        compiler_params=_params("parallel", "parallel", "arbitrary"),
        name="mlstm",
    )(b_i, b_f, qk, qk, v, v_t, li, lf, o_gate, head_g.reshape(n_heads, 1, LANES))


def _ff_chunks(tf):
    return [(c0, min(c0 + FF_CHUNK, tf)) for c0 in range(0, tf, FF_CHUNK)]


def _swiglu_accumulate(h_scr, wg_ref, wu_ref, wd_ref, act_scr, o_ref, after_chunk=None):
    h = h_scr[...]
    for c, (c0, c1) in enumerate(_ff_chunks(wg_ref.shape[1])):
        gate = jnp.dot(h, wg_ref[:, c0:c1], preferred_element_type=F32)
        up = jnp.dot(h, wu_ref[:, c0:c1], preferred_element_type=F32)
        act_scr[:, c0:c1] = (gate * _sigmoid(gate) * up).astype(BF16)
        if after_chunk is not None:
            after_chunk(c)
    o_ref[...] += jnp.dot(act_scr[...], wd_ref[...], preferred_element_type=F32)


def _ffn_kernel(x_ref, a_ref, b_ref, wo_ref, g_ref, wg_ref, wu_ref, wd_ref, o_ref, h_scr, act_scr):
    @pl.when(pl.program_id(1) == 0)
    def _():
        na = a_ref.shape[1]
        x = x_ref[...] + jnp.dot(a_ref[...], wo_ref[0:na, :], preferred_element_type=F32)
        x = x + jnp.dot(b_ref[...], wo_ref[na:, :], preferred_element_type=F32)
        h_scr[...] = _rms(x, g_ref[...]).astype(BF16)
        o_ref[...] = x

    _swiglu_accumulate(h_scr, wg_ref, wu_ref, wd_ref, act_scr, o_ref)


def out_proj_ffn(x, a, b_part, w_out, g, w_gate, w_up, w_down, layer, *, tm=1024, tf=1792):
    t, d = x.shape
    ff = w_gate.shape[2]
    return pl.pallas_call(
        _ffn_kernel,
        out_shape=jax.ShapeDtypeStruct((t, d), F32),
        grid=(t // tm, ff // tf),
        in_specs=[pl.BlockSpec((tm, d), lambda i, j: (i, 0)),
                  pl.BlockSpec((tm, a.shape[1]), lambda i, j: (i, 0)),
                  pl.BlockSpec((tm, b_part.shape[1]), lambda i, j: (i, 0)),
                  pl.BlockSpec(w_out.shape, lambda i, j: (0, 0)),
                  pl.BlockSpec((1, d), lambda i, j: (0, 0)),
                  pl.BlockSpec((None, d, tf), lambda i, j: (layer, 0, j)),
                  pl.BlockSpec((None, d, tf), lambda i, j: (layer, 0, j)),
                  pl.BlockSpec((None, tf, d), lambda i, j: (layer, j, 0))],
        out_specs=pl.BlockSpec((tm, d), lambda i, j: (i, 0)),
        scratch_shapes=[pltpu.VMEM((tm, d), BF16), pltpu.VMEM((tm, tf), BF16)],
        compiler_params=_params("parallel", "arbitrary"),
        name="out_proj_ffn",
    )(x, a, b_part, w_out, g.reshape(1, d), w_gate, w_up, w_down)


def _store_row_tiles(ref, x):
    rows, d = x.shape
    per = d // LANES
    for s in range(per):
        ref[pl.ds(s, rows, stride=per), :] = x[:, s * LANES:(s + 1) * LANES]


def _load_row_tiles(ref, rows, per):
    return [ref[pl.ds(s, rows, stride=per), :] for s in range(per)]


def _router_kernel(x_ref, a_ref, b_ref, wo_ref, g_ref, wr_ref, x1_ref, h_ref, route_ref):
    na = a_ref.shape[1]
    x1 = x_ref[...] + jnp.dot(a_ref[...], wo_ref[0:na, :], preferred_element_type=F32)
    x1 = x1 + jnp.dot(b_ref[...], wo_ref[na:, :], preferred_element_type=F32)
    x1_ref[...] = x1
    h = _rms(x1, g_ref[...])
    _store_row_tiles(h_ref, h)
    h_hi = h.astype(BF16)
    h_lo = (h - h_hi.astype(F32)).astype(BF16)
    hi_terms = jnp.dot(h_hi, wr_ref[...], preferred_element_type=F32)
    logits = (hi_terms[:, :LANES] + hi_terms[:, LANES:]
              + jnp.dot(h_lo, wr_ref[:, :LANES], preferred_element_type=F32))
    lane = lax.broadcasted_iota(jnp.int32, logits.shape, 1)
    logits = jnp.where(lane < N_EXPERTS, logits, NEG_INF)
    m1 = jnp.max(logits, axis=-1, keepdims=True)
    i1 = jnp.min(jnp.where(logits == m1, lane, LANES), axis=-1, keepdims=True)
    rest = jnp.where(lane == i1, NEG_INF, logits)
    m2 = jnp.max(rest, axis=-1, keepdims=True)
    i2 = jnp.min(jnp.where(rest == m2, lane, LANES), axis=-1, keepdims=True)
    e2 = jnp.exp(m2 - m1)
    g1 = 1.0 / (1.0 + e2)
    g2 = e2 / (1.0 + e2)
    route_ref[...] = jnp.where(lane == 0, i1.astype(F32),
                               jnp.where(lane == 1, i2.astype(F32),
                                         jnp.where(lane == 2, g1, jnp.where(lane == 3, g2, 0.0))))


def out_proj_router(x, a, b_part, w_out, g, w_router_padded, *, tm=512):
    t, d = x.shape
    per = d // LANES
    w_hi = w_router_padded.astype(BF16)
    w_lo = (w_router_padded - w_hi.astype(F32)).astype(BF16)
    w_pieces = jnp.concatenate([w_hi, w_lo], axis=1)
    return pl.pallas_call(
        _router_kernel,
        out_shape=[jax.ShapeDtypeStruct((t, d), F32),
                   jax.ShapeDtypeStruct((t * per, LANES), F32),
                   jax.ShapeDtypeStruct((t, LANES), F32)],
        grid=(t // tm,),
        in_specs=[pl.BlockSpec((tm, d), lambda i: (i, 0)),
                  pl.BlockSpec((tm, a.shape[1]), lambda i: (i, 0)),
                  pl.BlockSpec((tm, b_part.shape[1]), lambda i: (i, 0)),
                  pl.BlockSpec(w_out.shape, lambda i: (0, 0)),
                  pl.BlockSpec((1, d), lambda i: (0, 0)),
                  pl.BlockSpec((d, 2 * LANES), lambda i: (0, 0))],
        out_specs=[pl.BlockSpec((tm, d), lambda i: (i, 0)),
                   pl.BlockSpec((tm * per, LANES), lambda i: (i, 0)),
                   pl.BlockSpec((tm, LANES), lambda i: (i, 0))],
        compiler_params=_params("parallel"),
        name="out_proj_router",
    )(x, a, b_part, w_out, g.reshape(1, d), w_pieces)


def _expert_kernel(be_ref, nused_ref, tok_cur_ref, tok_next_ref, h_hbm, wg_ref, wu_ref, wd_ref,
                   o_ref, xbuf, sems, h_scr, act_scr, acc_scr, *, tm):
    i = pl.program_id(0)
    j = pl.program_id(1)
    n_used = nused_ref[0]
    used = i < n_used
    per = h_scr.shape[1] // LANES

    def issue_gather(tok_ref, slot):
        def body(rr, _):
            for prio in range(DMA_PRIORITIES):
                r = rr * DMA_PRIORITIES + prio
                src = pl.multiple_of(tok_ref[0, r] * per, per)
                dst = pl.multiple_of(r * per, per)
                pltpu.make_async_copy(h_hbm.at[pl.ds(src, per)], xbuf.at[slot, pl.ds(dst, per)],
                                      sems.at[slot]).start(priority=prio)
            return 0
        lax.fori_loop(0, tm // DMA_PRIORITIES, body, 0, unroll=4)

    @pl.when(jnp.logical_and(i == 0, j == 0))
    def _():
        issue_gather(tok_cur_ref, 0)

    @pl.when(jnp.logical_and(used, j == 0))
    def _():
        slot = i % 2
        pltpu.make_async_copy(xbuf.at[slot], xbuf.at[slot], sems.at[slot]).wait()
        for s, piece in enumerate(_load_row_tiles(xbuf.at[slot], tm, per)):
            h_scr[:, s * LANES:(s + 1) * LANES] = piece.astype(BF16)
        acc_scr[...] = jnp.zeros_like(acc_scr)

    @pl.when(used)
    def _():
        _swiglu_accumulate(h_scr, wg_ref, wu_ref, wd_ref, act_scr, acc_scr)

    @pl.when(jnp.logical_and(i + 1 < n_used, j == 0))
    def _():
        issue_gather(tok_next_ref, (i + 1) % 2)

    @pl.when(jnp.logical_and(used, j == pl.num_programs(1) - 1))
    def _():
        _store_row_tiles(o_ref, acc_scr[...])

    @pl.when(jnp.logical_and(jnp.logical_not(used), j == 0))
    def _():
        o_ref[...] = jnp.zeros_like(o_ref)


def expert_ffn(h, row_tok, block_expert, n_used, w_gate, w_up, w_down, layer, *, tm, tf=1792):
    n_rows = row_tok.shape[0]
    d = w_gate.shape[2]
    per = d // LANES
    ff = w_gate.shape[3]
    n_blocks = n_rows // tm
    last_j = ff // tf - 1

    def up_block(i, j, be, nu):
        return (layer, be[jnp.minimum(i, nu[0] - 1)], 0, jnp.where(i < nu[0], j, last_j))

    def down_block(i, j, be, nu):
        return (layer, be[jnp.minimum(i, nu[0] - 1)], jnp.where(i < nu[0], j, last_j), 0)

    grid_spec = pltpu.PrefetchScalarGridSpec(
        num_scalar_prefetch=2,
        grid=(n_blocks, ff // tf),
        in_specs=[pl.BlockSpec((None, 1, tm), lambda i, j, be, nu: (i, 0, 0),
                               memory_space=pltpu.SMEM),
                  pl.BlockSpec((None, 1, tm),
                               lambda i, j, be, nu: (jnp.minimum(i + 1, n_blocks - 1), 0, 0),
                               memory_space=pltpu.SMEM),
                  pl.BlockSpec(memory_space=pl.ANY),
                  pl.BlockSpec((None, None, d, tf), up_block),
                  pl.BlockSpec((None, None, d, tf), up_block),
                  pl.BlockSpec((None, None, tf, d), down_block)],
        out_specs=pl.BlockSpec((tm * per, LANES), lambda i, j, be, nu: (i, 0)),
        scratch_shapes=[pltpu.VMEM((2, tm * per, LANES), F32), pltpu.SemaphoreType.DMA((2,)),
                        pltpu.VMEM((tm, d), BF16), pltpu.VMEM((tm, tf), BF16),
                        pltpu.VMEM((tm, d), F32)])
    tok_blocks = row_tok.reshape(n_blocks, 1, tm)
    return pl.pallas_call(
        functools.partial(_expert_kernel, tm=tm),
        out_shape=jax.ShapeDtypeStruct((n_rows * per, LANES), F32),
        grid_spec=grid_spec,
        compiler_params=_params("arbitrary", "arbitrary"),
        name="expert_ffn",
    )(block_expert, n_used, tok_blocks, tok_blocks, h, w_gate, w_up, w_down)


def _combine_kernel(pos_cur_ref, pos_next_ref, x_ref, route_ref, y_ref, *rest, rows, final):
    if final:
        g_ref, o_ref, buf, sems = rest
    else:
        o_ref, buf, sems = rest
    i = pl.program_id(0)
    n = pl.num_programs(0)
    per = x_ref.shape[1] // LANES

    def issue(pos_ref, slot):
        def body(r, _):
            dst = pl.multiple_of(r * per, per)
            for kk in range(TOP_K):
                src = pl.multiple_of(pos_ref[0, TOP_K * r + kk] * per, per)
                pltpu.make_async_copy(y_ref.at[pl.ds(src, per)], buf.at[slot, kk, pl.ds(dst, per)],
                                      sems.at[slot]).start(priority=kk % DMA_PRIORITIES)
            return 0
        lax.fori_loop(0, rows, body, 0, unroll=4)

    @pl.when(i == 0)
    def _():
        issue(pos_cur_ref, 0)

    @pl.when(i + 1 < n)
    def _():
        issue(pos_next_ref, (i + 1) % 2)

    slot = i % 2
    pltpu.make_async_copy(buf.at[slot], buf.at[slot], sems.at[slot]).wait()
    route = route_ref[...]
    g1 = route[:, 2:3]
    g2 = route[:, 3:4]
    y1 = jnp.concatenate(_load_row_tiles(buf.at[slot, 0], rows, per), axis=-1)
    y2 = jnp.concatenate(_load_row_tiles(buf.at[slot, 1], rows, per), axis=-1)
    out = x_ref[...] + g1 * y1 + g2 * y2
    o_ref[...] = _rms(out, g_ref[...]) if final else out


def combine_residual(x, route, y_rows, pos, final_g=None, *, rows=256):
    t, d = x.shape
    final = final_g is not None
    n_steps = t // rows
    grid_spec = pltpu.PrefetchScalarGridSpec(
        num_scalar_prefetch=0,
        grid=(n_steps,),
        in_specs=[pl.BlockSpec((None, 1, TOP_K * rows), lambda i: (i, 0, 0),
                               memory_space=pltpu.SMEM),
                  pl.BlockSpec((None, 1, TOP_K * rows),
                               lambda i: (jnp.minimum(i + 1, n_steps - 1), 0, 0),
                               memory_space=pltpu.SMEM),
                  pl.BlockSpec((rows, d), lambda i: (i, 0)),
                  pl.BlockSpec((rows, LANES), lambda i: (i, 0)),
                  pl.BlockSpec(memory_space=pl.ANY)]
                 + ([pl.BlockSpec((1, d), lambda i: (0, 0))] if final else []),
        out_specs=pl.BlockSpec((rows, d), lambda i: (i, 0)),
        scratch_shapes=[pltpu.VMEM((2, TOP_K, rows * (d // LANES), LANES), F32),
                        pltpu.SemaphoreType.DMA((2,))])
    pos_blocks = pos.reshape(n_steps, 1, TOP_K * rows)
    extra = (final_g.reshape(1, d),) if final else ()
    return pl.pallas_call(
        functools.partial(_combine_kernel, rows=rows, final=final),
        out_shape=jax.ShapeDtypeStruct((t, d), F32),
        grid_spec=grid_spec,
        compiler_params=_params("arbitrary"),
        name="combine_residual",
    )(pos_blocks, pos_blocks, x, route, y_rows, *extra)


def _pad_cols(w, n):
    return jnp.pad(w, ((0, 0), (0, n - w.shape[1])))


def _scale_q_cols(w_qkv, width):
    scale = jnp.concatenate([jnp.full((width,), LOG2E * HEAD_DIM ** -0.5, F32),
                             jnp.ones((2 * width,), F32)])
    return w_qkv * scale


def even_layer(x, b, s, g_mix, g_ffn, w_in, b_forget, pool_w, pool_scale, w_out,
               w_gate, w_up, w_down, e, *, fox_tq=512):
    t, d = x.shape
    pool_width = pool_w.shape[0] * pool_w.shape[1]
    n_heads = b_forget.shape[0]
    fox_width = n_heads * HEAD_DIM
    w_a = w_in[:, :pool_width].astype(BF16)
    w_qkv = _scale_q_cols(w_in[:, pool_width:pool_width + 3 * fox_width], fox_width).astype(BF16)
    w_f = _pad_cols(w_in[:, pool_width + 3 * fox_width:], LANES).astype(BF16)
    qkv, a_in, f_pre = norm_proj(x, g_mix, [w_qkv, w_a, w_f], [BF16, F32, F32])

    pool_bd = jax.scipy.linalg.block_diag(*[pool_w[i] for i in range(pool_w.shape[0])]).astype(BF16)
    a_out, cum = pool_and_cum(a_in.reshape(b, s, pool_width), f_pre.reshape(b, s, LANES),
                              _pad_cols(b_forget.reshape(1, n_heads), LANES), pool_bd,
                              pool_scale.reshape(1, pool_width))
    cum_blocks = cum[:, :, :n_heads].reshape(b, s // fox_tq, fox_tq, n_heads // 2, 2)
    cum_blocks = cum_blocks.transpose(0, 3, 1, 4, 2)
    attn = fox_attention(qkv.reshape(b, s, 3 * fox_width), cum_blocks, tq=fox_tq)
    return out_proj_ffn(x, a_out.reshape(t, pool_width), attn.reshape(t, fox_width),
                        w_out.astype(BF16), g_ffn, w_gate, w_up, w_down, e)


def odd_layer(x, b, s, g_mix, g_ffn, w_in, b_igate, b_fgate, conv_w, head_norm_g, w_out,
              w_router, w_gate, w_up, w_down, e, final_g=None):
    c_out, d_out = odd_mixers(x, b, s, g_mix, w_in, b_igate, b_fgate, conv_w, head_norm_g)
    return moe_block(x, c_out, d_out, w_out.astype(BF16), g_ffn, w_router, w_gate, w_up, w_down,
                     e, final_g)


def odd_mixers(x, b, s, g_mix, w_in, b_igate, b_fgate, conv_w, head_norm_g):
    t, d = x.shape
    n_ml = b_igate.shape[0]
    ml_width = n_ml * LANES
    sb_width = d - ml_width
    cols = [3 * sb_width, 2 * ml_width, ml_width, ml_width]
    edges = [0]
    for c in cols:
        edges.append(edges[-1] + c)
    w_sb, w_mqk, w_mv, w_mo = [w_in[:, edges[i]:edges[i + 1]] for i in range(4)]
    w_sb = _scale_q_cols(w_sb, sb_width)
    w_sb, w_mqk, w_mv, w_mo = [w.astype(BF16) for w in (w_sb, w_mqk, w_mv, w_mo)]
    w_gates = _pad_cols(w_in[:, edges[-1]:], LANES).astype(BF16)
    sqkv, ml_qk, ml_v, ml_o, gates = norm_proj(
        x, g_mix, [w_sb, w_mqk, w_mv, w_mo, w_gates], [BF16, F32, BF16, F32, F32])

    c_out = sb_attention(sqkv.reshape(b, s, 3 * sb_width))

    k_scale = jnp.concatenate([jnp.ones((ml_width,), F32),
                               jnp.full((ml_width,), LANES ** -0.5, F32)]).reshape(1, 2 * ml_width)
    qk = conv_silu(ml_qk.reshape(b, s, 2 * ml_width), conv_w, k_scale)
    nb = s // MLSTM_BLOCK
    v3 = ml_v.reshape(b, s, ml_width)
    v_t = v3.reshape(b, nb, MLSTM_BLOCK, n_ml, LANES).transpose(0, 3, 1, 4, 2)
    gate_rows = gates[:, :2 * n_ml].reshape(b, nb, MLSTM_BLOCK, 2 * n_ml).transpose(0, 3, 1, 2)
    gate_rows = gate_rows.reshape(b, 2 * n_ml, nb, 1, MLSTM_BLOCK)
    d_out = mlstm(qk, v3, v_t, gate_rows[:, :n_ml], gate_rows[:, n_ml:],
                  ml_o.reshape(b, s, ml_width), b_igate, b_fgate, head_norm_g)
    return c_out.reshape(t, sb_width), d_out.reshape(t, ml_width)


def moe_block(x, mix_a, mix_b, w_out, g_ffn, w_router, w_gate, w_up, w_down, e, final_g=None, *,
              expert_rows=1024):
    t, d = x.shape
    x, h, route = out_proj_router(x, mix_a, mix_b, w_out, g_ffn, _pad_cols(w_router, LANES))
    flat_e = route[:, :TOP_K].astype(jnp.int32).reshape(-1)
    n_assign = t * TOP_K
    onehot = jax.nn.one_hot(flat_e, N_EXPERTS, dtype=jnp.int32)
    rank = jnp.sum((jnp.cumsum(onehot, axis=0) - onehot) * onehot, axis=-1)
    counts = jnp.sum(onehot, axis=0)
    padded = (counts + expert_rows - 1) // expert_rows * expert_rows
    group_end = jnp.cumsum(padded)
    pos = (group_end[flat_e] - padded[flat_e] + rank).astype(jnp.int32)
    n_rows = n_assign + N_EXPERTS * expert_rows
    n_blocks = n_rows // expert_rows
    block_expert = jnp.minimum(
        jnp.searchsorted(group_end, jnp.arange(n_blocks) * expert_rows, side='right'),
        N_EXPERTS - 1).astype(jnp.int32)
    order = jnp.argsort(pos).astype(jnp.int32)
    row_expert = jnp.repeat(block_expert, expert_rows)
    compact = (jnp.arange(n_rows, dtype=jnp.int32) - (group_end - padded)[row_expert]
               + (jnp.cumsum(counts) - counts)[row_expert])
    row_tok = (order // TOP_K)[jnp.clip(compact, 0, n_assign - 1)]
    n_used = (group_end[-1:] // expert_rows).astype(jnp.int32)

    y_rows = expert_ffn(h, row_tok, block_expert, n_used, w_gate, w_up, w_down, e, tm=expert_rows)
    return combine_residual(x, route, y_rows, pos, final_g)


def kernel(x, norm_mix_g, norm_ffn_g, norm_final_g, ev_w_in, ev_b_forget, ev_pool_w, ev_pool_scale, ev_w_out, ffn_w_gate, ffn_w_up, ffn_w_down, od_w_in, od_b_igate, od_b_fgate, od_conv_w, od_head_norm_g, od_w_out, moe_w_router, moe_w_gate, moe_w_up, moe_w_down):
    b, s, d = x.shape
    depth = norm_mix_g.shape[0]
    xt = x.reshape(b * s, d)
    ffn_w = [w.astype(BF16) for w in (ffn_w_gate, ffn_w_up, ffn_w_down)]
    moe_w = [w.astype(BF16) for w in (moe_w_gate, moe_w_up, moe_w_down)]
    for layer in range(depth):
        e = layer // 2
        if layer % 2 == 0:
            xt = even_layer(xt, b, s, norm_mix_g[layer], norm_ffn_g[layer], ev_w_in[e],
                            ev_b_forget[e], ev_pool_w[e], ev_pool_scale[e], ev_w_out[e],
                            *ffn_w, e)
        else:
            xt = odd_layer(xt, b, s, norm_mix_g[layer], norm_ffn_g[layer], od_w_in[e],
                           od_b_igate[e], od_b_fgate[e], od_conv_w[e], od_head_norm_g[e],
                           od_w_out[e], moe_w_router[e], *moe_w, e,
                           norm_final_g if layer == depth - 1 else None)
    assert depth % 2 == 0
    return xt.reshape(b, s, d)
```

```python
import functools

import jax
import jax.numpy as jnp
from jax import lax
from jax.experimental import pallas as pl
from jax.experimental.pallas import tpu as pltpu

F32 = jnp.float32
BF16 = jnp.bfloat16
EPS = 1e-6
NEG_INF = float("-inf")

LANES = 128
HEAD_DIM = 64
POOL_WINDOWS = (2, 4, 8, 16)
POOL_HALO = 16
CONV_WIDTH = 4
CONV_HALO = 8
FF_CHUNK = 512
MLSTM_BLOCK = 512
FOX_BLOCKS_PER_TRIP = 4
SB_BLOCKS_PER_TRIP = 4
DMA_PRIORITIES = 2
N_EXPERTS = 8
TOP_K = 2
VMEM_LIMIT = 56 * 1024 * 1024

LOG2E = 1.4426950408889634
SOFTPLUS2_CLAMP = 64.0
HIGHEST = lax.Precision.HIGHEST
NT_DIMS = (((1,), (1,)), ((), ()))


def _params(*semantics):
    return pltpu.CompilerParams(dimension_semantics=semantics, vmem_limit_bytes=VMEM_LIMIT)


def _rms(x, g):
    return x * lax.rsqrt(jnp.mean(x * x, axis=-1, keepdims=True) + EPS) * g


def _log_sigmoid(x):
    return jnp.minimum(x, 0.0) - jnp.log(1.0 + jnp.exp(-jnp.abs(x)))


def _sigmoid(x):
    return 1.0 / (1.0 + jnp.exp(-x))


def _norm_proj_kernel(x_ref, g_ref, *refs, n_out, col_chunk):
    w_refs, o_refs = refs[:n_out], refs[n_out:]
    h = _rms(x_ref[...], g_ref[...]).astype(BF16)
    for w_ref, o_ref in zip(w_refs, o_refs):
        n = w_ref.shape[1]
        for c0 in range(0, n, col_chunk):
            c1 = min(c0 + col_chunk, n)
            o_ref[:, c0:c1] = jnp.dot(
                h, w_ref[:, c0:c1], preferred_element_type=F32).astype(o_ref.dtype)


def norm_proj(x, g, weights, out_dtypes, *, tm=512, col_chunk=512):
    t, d = x.shape
    n_out = len(weights)
    return pl.pallas_call(
        functools.partial(_norm_proj_kernel, n_out=n_out, col_chunk=col_chunk),
        out_shape=[jax.ShapeDtypeStruct((t, w.shape[1]), dt) for w, dt in zip(weights, out_dtypes)],
        grid=(t // tm,),
        in_specs=[pl.BlockSpec((tm, d), lambda i: (i, 0)),
                  pl.BlockSpec((1, d), lambda i: (0, 0))]
                 + [pl.BlockSpec(w.shape, lambda i: (0, 0)) for w in weights],
        out_specs=[pl.BlockSpec((tm, w.shape[1]), lambda i: (i, 0)) for w in weights],
        compiler_params=_params("parallel"),
        name="norm_proj",
    )(x, g.reshape(1, d), *weights)


def _pool_kernel(a_ref, f_ref, bf_ref, w_ref, scale_ref, o_ref, cum_ref, *, rows):
    s, width = a_ref.shape
    group = lax.broadcasted_iota(jnp.int32, (1, width), 1) // (width // len(POOL_WINDOWS))
    window = jnp.where(group == 0, POOL_WINDOWS[0],
                       jnp.where(group == 1, POOL_WINDOWS[1],
                                 jnp.where(group == 2, POOL_WINDOWS[2], POOL_WINDOWS[3])))
    tri = (lax.broadcasted_iota(jnp.int32, (rows, rows), 0)
           >= lax.broadcasted_iota(jnp.int32, (rows, rows), 1)).astype(F32)
    carry = jnp.zeros((1, f_ref.shape[1]), F32)
    for r in range(s // rows):
        r0 = r * rows
        if r == 0:
            ext = jnp.concatenate([jnp.zeros((POOL_HALO, width), F32), a_ref[0:rows, :]], axis=0)
        else:
            ext = a_ref[r0 - POOL_HALO:r0 + rows, :]
        s2 = ext + pltpu.roll(ext, 1, 0)
        s4 = s2 + pltpu.roll(s2, 2, 0)
        s8 = s4 + pltpu.roll(s4, 4, 0)
        s16 = s8 + pltpu.roll(s8, 8, 0)
        wsum = jnp.where(group == 0, s2, jnp.where(group == 1, s4, jnp.where(group == 2, s8, s16)))
        wsum = wsum[POOL_HALO:, :]
        a = ext[POOL_HALO:, :]
        t1 = r0 + 1 + lax.broadcasted_iota(jnp.int32, (rows, 1), 0)
        count = jnp.minimum(t1, window).astype(F32)
        pooled = wsum / count - a
        mixed = jnp.dot(pooled.astype(BF16), w_ref[...], preferred_element_type=F32)
        o_ref[r0:r0 + rows, :] = (mixed * scale_ref[...]).astype(o_ref.dtype)
        lf = _log_sigmoid(f_ref[r0:r0 + rows, :] + bf_ref[...]) * LOG2E
        cum = jnp.dot(tri, lf, preferred_element_type=F32, precision=HIGHEST) + carry
        cum_ref[r0:r0 + rows, :] = cum
        carry = cum[rows - 1:rows, :]


def pool_and_cum(a_in, f_pre, b_forget, pool_w_bd, pool_scale, *, rows=256):
    b, s, width = a_in.shape
    fw = f_pre.shape[-1]
    return pl.pallas_call(
        functools.partial(_pool_kernel, rows=rows),
        out_shape=[jax.ShapeDtypeStruct((b, s, width), BF16),
                   jax.ShapeDtypeStruct((b, s, fw), F32)],
        grid=(b,),
        in_specs=[pl.BlockSpec((None, s, width), lambda i: (i, 0, 0)),
                  pl.BlockSpec((None, s, fw), lambda i: (i, 0, 0)),
                  pl.BlockSpec((1, fw), lambda i: (0, 0)),
                  pl.BlockSpec((width, width), lambda i: (0, 0)),
                  pl.BlockSpec((1, width), lambda i: (0, 0))],
        out_specs=[pl.BlockSpec((None, s, width), lambda i: (i, 0, 0)),
                   pl.BlockSpec((None, s, fw), lambda i: (i, 0, 0))],
        compiler_params=_params("parallel"),
        name="pool_and_cum",
    )(a_in, f_pre, b_forget, pool_w_bd, pool_scale)


def _fox_kernel(q_ref, k_ref, v_ref, cum_ref, o_ref, *, tq):
    i = pl.program_id(2)
    lane = lax.broadcasted_iota(jnp.int32, (1, LANES), 1)
    first = lane < HEAD_DIM
    q = q_ref[...]
    q_heads = (jnp.where(first, q, jnp.zeros_like(q)), jnp.where(first, jnp.zeros_like(q), q))
    row = lax.broadcasted_iota(jnp.int32, (tq, tq), 0)
    col = lax.broadcasted_iota(jnp.int32, (tq, tq), 1)

    def block(j, carry, masked):
        ms, accs = carry
        start = pl.multiple_of(j * tq, tq)
        k = k_ref[pl.ds(start, tq), :]
        v = v_ref[pl.ds(start, tq), :]
        ones = jnp.ones_like(v)
        v_heads = (jnp.where(first, v, ones), jnp.where(first, ones, v))
        ck = cum_ref[j]
        new_ms, new_accs = [], []
        scs = [lax.dot_general(q_heads[h], k, NT_DIMS, preferred_element_type=F32)
               for h in range(2)]
        for h in range(2):
            sc = scs[h] - ck[h:h + 1, :]
            if masked:
                sc = jnp.where(col <= row, sc, NEG_INF)
            m_new = jnp.maximum(ms[h], jnp.max(sc, axis=-1, keepdims=True))
            alpha = jnp.exp2(ms[h] - m_new)
            p = jnp.exp2((sc - m_new).astype(BF16))
            new_ms.append(m_new)
            new_accs.append(alpha * accs[h] + jnp.dot(p, v_heads[h], preferred_element_type=F32))
        return tuple(new_ms), tuple(new_accs)

    init = ((jnp.full((tq, 1), NEG_INF, F32),) * 2, (jnp.zeros((tq, LANES), F32),) * 2)
    def run(first_block, count, c):
        for d in range(count):
            c = block(first_block + d, c, False)
        return c

    n_trips = i // FOX_BLOCKS_PER_TRIP
    carry = lax.fori_loop(
        0, n_trips, lambda t, c: run(FOX_BLOCKS_PER_TRIP * t, FOX_BLOCKS_PER_TRIP, c), init)
    def tail(n_left):
        return lambda c: block(i, run(i - n_left, n_left, c), True)

    left = i - FOX_BLOCKS_PER_TRIP * n_trips
    _, accs = lax.switch(left, [tail(n) for n in range(FOX_BLOCKS_PER_TRIP)], carry)
    out0 = accs[0] / accs[0][:, HEAD_DIM:HEAD_DIM + 1]
    out1 = accs[1] / accs[1][:, 0:1]
    o_ref[...] = jnp.where(first, out0, out1).astype(o_ref.dtype)


def fox_attention(qkv, cum_blocks, *, tq=512):
    b, s, w3 = qkv.shape
    n_pairs = w3 // 3 // LANES
    return pl.pallas_call(
        functools.partial(_fox_kernel, tq=tq),
        out_shape=jax.ShapeDtypeStruct((b, s, w3 // 3), BF16),
        grid=(b, n_pairs, s // tq),
        in_specs=[pl.BlockSpec((None, tq, LANES), lambda bi, p, i: (bi, i, p)),
                  pl.BlockSpec((None, s, LANES), lambda bi, p, i: (bi, 0, n_pairs + p)),
                  pl.BlockSpec((None, s, LANES), lambda bi, p, i: (bi, 0, 2 * n_pairs + p)),
                  pl.BlockSpec((None, None, s // tq, 2, tq), lambda bi, p, i: (bi, p, 0, 0, 0))],
        out_specs=pl.BlockSpec((None, tq, LANES), lambda bi, p, i: (bi, i, p)),
        compiler_params=_params("parallel", "parallel", "arbitrary"),
        name="fox_attention",
    )(qkv, qkv, qkv, cum_blocks)


def _sb_kernel(q_ref, k_ref, v_ref, o_ref, *, tq, tk):
    i = pl.program_id(2)
    per = tq // tk
    lane = lax.broadcasted_iota(jnp.int32, (1, LANES), 1)
    first = lane < HEAD_DIM
    q = q_ref[...]
    q_heads = (jnp.where(first, q, jnp.zeros_like(q)), jnp.where(first, jnp.zeros_like(q), q))
    row = lax.broadcasted_iota(jnp.int32, (tq, tk), 0)
    col = lax.broadcasted_iota(jnp.int32, (tq, tk), 1)
    neg_suffix = jnp.where(lax.broadcasted_iota(jnp.int32, (tk, tk), 0)
                           > lax.broadcasted_iota(jnp.int32, (tk, tk), 1), -1.0, 0.0).astype(BF16)

    def logits(j, r0=0):
        start = pl.multiple_of(j * tk, tk)
        k = k_ref[pl.ds(start, tk), :]
        return tuple(lax.dot_general(q_heads[h][r0:], k, NT_DIMS, preferred_element_type=F32)
                     for h in range(2))

    def block(j, zs, carry, diagonal, r0=0):
        used, acc = carry
        start = pl.multiple_of(j * tk, tk)
        v = v_ref[pl.ds(start, tk), :]
        if diagonal:
            valid = (col < row)[:tq - r0]
        new_used, pvs = [], []
        for h in range(2):
            z = zs[h]
            sp = jnp.maximum(z, jnp.log2(1.0 + jnp.exp2(jnp.minimum(z, SOFTPLUS2_CLAMP))))
            if diagonal:
                sp = jnp.where(valid, sp, 0.0)
            later_in = jnp.dot(sp.astype(BF16), neg_suffix, preferred_element_type=F32)
            w = jnp.exp2((z - sp + (later_in - used[h][r0:])).astype(BF16))
            if diagonal:
                w = jnp.where(valid, w, jnp.zeros_like(w))
            pvs.append(jnp.dot(w, v, preferred_element_type=F32))
            grown = used[h][r0:] + (sp[:, 0:1] - later_in[:, 0:1])
            new_used.append(jnp.concatenate([used[h][:r0], grown], axis=0) if r0 else grown)
        pv = acc[r0:] + jnp.where(first, pvs[0], pvs[1])
        return tuple(new_used), (jnp.concatenate([acc[:r0], pv], axis=0) if r0 else pv)

    def run(first_block, count, carry):
        for d in range(count):
            carry = block(first_block - d, logits(first_block - d), carry, False)
        return carry

    assert per == 2 and SB_BLOCKS_PER_TRIP == 4
    n_full = i * per
    left = n_full % SB_BLOCKS_PER_TRIP

    def head(n_left):
        def f(carry):
            for d in range(per):
                j, r0 = (i + 1) * per - 1 - d, (per - 1 - d) * tk
                carry = block(j, logits(j, r0), carry, True, r0)
            return run(n_full - 1, n_left, carry)
        return f

    carry = ((jnp.zeros((tq, 1), F32),) * 2, jnp.zeros((tq, LANES), F32))
    carry = lax.switch(left // 2, [head(0), head(2)], carry)
    rest = n_full - left
    _, acc = lax.fori_loop(
        0, rest // SB_BLOCKS_PER_TRIP,
        lambda t, c: run(rest - 1 - SB_BLOCKS_PER_TRIP * t, SB_BLOCKS_PER_TRIP, c), carry)
    o_ref[...] = acc.astype(o_ref.dtype)


def sb_attention(qkv, *, tq=512, tk=256):
    b, s, w3 = qkv.shape
    n_pairs = w3 // 3 // LANES
    return pl.pallas_call(
        functools.partial(_sb_kernel, tq=tq, tk=tk),
        out_shape=jax.ShapeDtypeStruct((b, s, w3 // 3), BF16),
        grid=(b, n_pairs, s // tq),
        in_specs=[pl.BlockSpec((None, tq, LANES), lambda bi, p, i: (bi, i, p)),
                  pl.BlockSpec((None, s, LANES), lambda bi, p, i: (bi, 0, n_pairs + p)),
                  pl.BlockSpec((None, s, LANES), lambda bi, p, i: (bi, 0, 2 * n_pairs + p))],
        out_specs=pl.BlockSpec((None, tq, LANES), lambda bi, p, i: (bi, i, p)),
        compiler_params=_params("parallel", "parallel", "arbitrary"),
        name="sb_attention",
    )(qkv, qkv, qkv)


def _conv_kernel(x_ref, halo_ref, w_ref, colscale_ref, o_ref):
    i = pl.program_id(1)
    halo = jnp.where(i == 0, 0.0, halo_ref[...])
    ext = jnp.concatenate([halo, x_ref[...]], axis=0)
    y = ext * w_ref[CONV_WIDTH - 1:CONV_WIDTH, :]
    for back in range(1, CONV_WIDTH):
        y = y + pltpu.roll(ext, back, 0) * w_ref[CONV_WIDTH - 1 - back:CONV_WIDTH - back, :]
    y = y[CONV_HALO:, :]
    o_ref[...] = (y * _sigmoid(y) * colscale_ref[...]).astype(o_ref.dtype)


def conv_silu(x, conv_w, colscale, *, rows=512):
    b, s, c = x.shape
    per = rows // CONV_HALO
    return pl.pallas_call(
        _conv_kernel,
        out_shape=jax.ShapeDtypeStruct((b, s, c), BF16),
        grid=(b, s // rows),
        in_specs=[pl.BlockSpec((None, rows, c), lambda bi, i: (bi, i, 0)),
                  pl.BlockSpec((None, CONV_HALO, c),
                               lambda bi, i: (bi, jnp.maximum(i * per - 1, 0), 0)),
                  pl.BlockSpec((CONV_WIDTH, c), lambda bi, i: (0, 0)),
                  pl.BlockSpec((1, c), lambda bi, i: (0, 0))],
        out_specs=pl.BlockSpec((None, rows, c), lambda bi, i: (bi, i, 0)),
        compiler_params=_params("parallel", "parallel"),
        name="conv_silu",
    )(x, x, conv_w, colscale)


def _mlstm_kernel(bi_ref, bf_ref, q_ref, k_ref, v_ref, vt_ref, li_ref, lf_ref, o_ref, g_ref,
                  out_ref, c_scr, n_scr, m_scr):
    head = pl.program_id(1)
    step = pl.program_id(2)
    L = q_ref.shape[0]

    @pl.when(step == 0)
    def _():
        c_scr[...] = jnp.zeros_like(c_scr)
        n_scr[...] = jnp.zeros_like(n_scr)
        m_scr[...] = jnp.zeros_like(m_scr)

    r_idx = lax.broadcasted_iota(jnp.int32, (L, L), 0)
    c_idx = lax.broadcasted_iota(jnp.int32, (L, L), 1)
    lower = r_idx >= c_idx
    tri_u = (r_idx <= c_idx).astype(F32)

    q = q_ref[...]
    k = k_ref[...]
    c_st = c_scr[...]
    n_st = n_scr[...]
    m_st = m_scr[...]
    lf_row = _log_sigmoid(lf_ref[...] + bf_ref[head])
    li_row = li_ref[...] + bi_ref[head]
    bcum_row = jnp.dot(jnp.broadcast_to(lf_row, (8, L)), tri_u, preferred_element_type=F32,
                       precision=HIGHEST)[0:1, :]
    bcum_col = jnp.broadcast_to(bcum_row, (LANES, L)).T[:, 0:1]
    g = jnp.sum(lf_row, axis=-1, keepdims=True)

    log_d = jnp.where(lower, bcum_col - bcum_row + li_row, NEG_INF)
    inter = bcum_col + m_st
    m_out = jnp.maximum(inter, jnp.max(log_d, axis=-1, keepdims=True))
    d_mat = jnp.exp(log_d - m_out)
    inter_w = jnp.exp(inter - m_out)
    qk = lax.dot_general(q, k, NT_DIMS, preferred_element_type=F32) * d_mat
    num = (jnp.dot(qk.astype(BF16), v_ref[...], preferred_element_type=F32)
           + inter_w * lax.dot_general(q, c_st.astype(BF16), NT_DIMS, preferred_element_type=F32))
    den = (jnp.sum(qk, axis=-1, keepdims=True)
           + inter_w * jnp.sum(q.astype(F32) * n_st, axis=-1, keepdims=True))
    hh = num / jnp.maximum(jnp.abs(den), jnp.exp(-m_out))
    hn = _rms(hh, g_ref[...])
    out_ref[...] = (hn * _sigmoid(o_ref[...])).astype(out_ref.dtype)

    a_row = g - bcum_row + li_row
    m_loc = jnp.max(a_row, axis=-1, keepdims=True)
    w_row = jnp.exp(a_row - m_loc)
    c_loc = jnp.dot((vt_ref[...].astype(F32) * w_row).astype(BF16), k, preferred_element_type=F32)
    n_loc = jnp.dot(jnp.broadcast_to(w_row, (8, L)).astype(BF16), k,
                    preferred_element_type=F32)[0:1, :]
    m_new = jnp.maximum(g + m_st, m_loc)
    decay = jnp.exp(g + m_st - m_new)
    fresh = jnp.exp(m_loc - m_new)
    c_scr[...] = decay * c_st + fresh * c_loc
    n_scr[...] = decay * n_st + fresh * n_loc
    m_scr[...] = m_new


def mlstm(qk, v, v_t, li, lf, o_gate, b_i, b_f, head_g):
    b, s, w = v.shape
    n_heads = w // LANES
    rows = v_t.shape[-1]
    grid_spec = pltpu.PrefetchScalarGridSpec(
        num_scalar_prefetch=2,
        grid=(b, n_heads, s // rows),
        in_specs=[pl.BlockSpec((None, rows, LANES), lambda bi, h, c, *_: (bi, c, h)),
                  pl.BlockSpec((None, rows, LANES), lambda bi, h, c, *_: (bi, c, n_heads + h)),
                  pl.BlockSpec((None, rows, LANES), lambda bi, h, c, *_: (bi, c, h)),
                  pl.BlockSpec((None, None, None, LANES, rows),
                               lambda bi, h, c, *_: (bi, h, c, 0, 0)),
                  pl.BlockSpec((None, None, None, 1, rows), lambda bi, h, c, *_: (bi, h, c, 0, 0)),
                  pl.BlockSpec((None, None, None, 1, rows), lambda bi, h, c, *_: (bi, h, c, 0, 0)),
                  pl.BlockSpec((None, rows, LANES), lambda bi, h, c, *_: (bi, c, h)),
                  pl.BlockSpec((None, 1, LANES), lambda bi, h, c, *_: (h, 0, 0))],
        out_specs=pl.BlockSpec((None, rows, LANES), lambda bi, h, c, *_: (bi, c, h)),
        scratch_shapes=[pltpu.VMEM((LANES, LANES), F32), pltpu.VMEM((1, LANES), F32),
                        pltpu.VMEM((1, 1), F32)])
    return pl.pallas_call(
        _mlstm_kernel,
        out_shape=jax.ShapeDtypeStruct((b, s, w), BF16),
        grid_spec=grid_spec,
        compiler_params=_params("parallel", "parallel", "arbitrary"),
        name="mlstm",
    )(b_i, b_f, qk, qk, v, v_t, li, lf, o_gate, head_g.reshape(n_heads, 1, LANES))


def _ff_chunks(tf):
    return [(c0, min(c0 + FF_CHUNK, tf)) for c0 in range(0, tf, FF_CHUNK)]


def _swiglu_accumulate(h_scr, wg_ref, wu_ref, wd_ref, act_scr, o_ref, after_chunk=None):
    h = h_scr[...]
    for c, (c0, c1) in enumerate(_ff_chunks(wg_ref.shape[1])):
        gate = jnp.dot(h, wg_ref[:, c0:c1], preferred_element_type=F32)
        up = jnp.dot(h, wu_ref[:, c0:c1], preferred_element_type=F32)
        act_scr[:, c0:c1] = (gate * _sigmoid(gate) * up).astype(BF16)
        if after_chunk is not None:
            after_chunk(c)
    o_ref[...] += jnp.dot(act_scr[...], wd_ref[...], preferred_element_type=F32)


def _ffn_kernel(x_ref, a_ref, b_ref, wo_ref, g_ref, wg_ref, wu_ref, wd_ref, o_ref, h_scr, act_scr):
    @pl.when(pl.program_id(1) == 0)
    def _():
        na = a_ref.shape[1]
        x = x_ref[...] + jnp.dot(a_ref[...], wo_ref[0:na, :], preferred_element_type=F32)
        x = x + jnp.dot(b_ref[...], wo_ref[na:, :], preferred_element_type=F32)
        h_scr[...] = _rms(x, g_ref[...]).astype(BF16)
        o_ref[...] = x

    _swiglu_accumulate(h_scr, wg_ref, wu_ref, wd_ref, act_scr, o_ref)


def out_proj_ffn(x, a, b_part, w_out, g, w_gate, w_up, w_down, layer, *, tm=1024, tf=1792):
    t, d = x.shape
    ff = w_gate.shape[2]
    return pl.pallas_call(
        _ffn_kernel,
        out_shape=jax.ShapeDtypeStruct((t, d), F32),
        grid=(t // tm, ff // tf),
        in_specs=[pl.BlockSpec((tm, d), lambda i, j: (i, 0)),
                  pl.BlockSpec((tm, a.shape[1]), lambda i, j: (i, 0)),
                  pl.BlockSpec((tm, b_part.shape[1]), lambda i, j: (i, 0)),
                  pl.BlockSpec(w_out.shape, lambda i, j: (0, 0)),
                  pl.BlockSpec((1, d), lambda i, j: (0, 0)),
                  pl.BlockSpec((None, d, tf), lambda i, j: (layer, 0, j)),
                  pl.BlockSpec((None, d, tf), lambda i, j: (layer, 0, j)),
                  pl.BlockSpec((None, tf, d), lambda i, j: (layer, j, 0))],
        out_specs=pl.BlockSpec((tm, d), lambda i, j: (i, 0)),
        scratch_shapes=[pltpu.VMEM((tm, d), BF16), pltpu.VMEM((tm, tf), BF16)],
        compiler_params=_params("parallel", "arbitrary"),
        name="out_proj_ffn",
    )(x, a, b_part, w_out, g.reshape(1, d), w_gate, w_up, w_down)


def _store_row_tiles(ref, x):
    rows, d = x.shape
    per = d // LANES
    for s in range(per):
        ref[pl.ds(s, rows, stride=per), :] = x[:, s * LANES:(s + 1) * LANES]


def _load_row_tiles(ref, rows, per):
    return [ref[pl.ds(s, rows, stride=per), :] for s in range(per)]


def _router_kernel(x_ref, a_ref, b_ref, wo_ref, g_ref, wr_ref, x1_ref, h_ref, route_ref):
    na = a_ref.shape[1]
    x1 = x_ref[...] + jnp.dot(a_ref[...], wo_ref[0:na, :], preferred_element_type=F32)
    x1 = x1 + jnp.dot(b_ref[...], wo_ref[na:, :], preferred_element_type=F32)
    x1_ref[...] = x1
    h = _rms(x1, g_ref[...])
    _store_row_tiles(h_ref, h)
    h_hi = h.astype(BF16)
    h_lo = (h - h_hi.astype(F32)).astype(BF16)
    hi_terms = jnp.dot(h_hi, wr_ref[...], preferred_element_type=F32)
    logits = (hi_terms[:, :LANES] + hi_terms[:, LANES:]
              + jnp.dot(h_lo, wr_ref[:, :LANES], preferred_element_type=F32))
    lane = lax.broadcasted_iota(jnp.int32, logits.shape, 1)
    logits = jnp.where(lane < N_EXPERTS, logits, NEG_INF)
    m1 = jnp.max(logits, axis=-1, keepdims=True)
    i1 = jnp.min(jnp.where(logits == m1, lane, LANES), axis=-1, keepdims=True)
    rest = jnp.where(lane == i1, NEG_INF, logits)
    m2 = jnp.max(rest, axis=-1, keepdims=True)
    i2 = jnp.min(jnp.where(rest == m2, lane, LANES), axis=-1, keepdims=True)
    e2 = jnp.exp(m2 - m1)
    g1 = 1.0 / (1.0 + e2)
    g2 = e2 / (1.0 + e2)
    route_ref[...] = jnp.where(lane == 0, i1.astype(F32),
                               jnp.where(lane == 1, i2.astype(F32),
                                         jnp.where(lane == 2, g1, jnp.where(lane == 3, g2, 0.0))))


def out_proj_router(x, a, b_part, w_out, g, w_router_padded, *, tm=512):
    t, d = x.shape
    per = d // LANES
    w_hi = w_router_padded.astype(BF16)
    w_lo = (w_router_padded - w_hi.astype(F32)).astype(BF16)
    w_pieces = jnp.concatenate([w_hi, w_lo], axis=1)
    return pl.pallas_call(
        _router_kernel,
        out_shape=[jax.ShapeDtypeStruct((t, d), F32),
                   jax.ShapeDtypeStruct((t * per, LANES), F32),
                   jax.ShapeDtypeStruct((t, LANES), F32)],
        grid=(t // tm,),
        in_specs=[pl.BlockSpec((tm, d), lambda i: (i, 0)),
                  pl.BlockSpec((tm, a.shape[1]), lambda i: (i, 0)),
                  pl.BlockSpec((tm, b_part.shape[1]), lambda i: (i, 0)),
                  pl.BlockSpec(w_out.shape, lambda i: (0, 0)),
                  pl.BlockSpec((1, d), lambda i: (0, 0)),
                  pl.BlockSpec((d, 2 * LANES), lambda i: (0, 0))],
        out_specs=[pl.BlockSpec((tm, d), lambda i: (i, 0)),
                   pl.BlockSpec((tm * per, LANES), lambda i: (i, 0)),
                   pl.BlockSpec((tm, LANES), lambda i: (i, 0))],
        compiler_params=_params("parallel"),
        name="out_proj_router",
    )(x, a, b_part, w_out, g.reshape(1, d), w_pieces)


def _expert_kernel(be_ref, nused_ref, tok_cur_ref, tok_next_ref, h_hbm, wg_ref, wu_ref, wd_ref,
                   o_ref, xbuf, sems, h_scr, act_scr, acc_scr, *, tm):
    i = pl.program_id(0)
    j = pl.program_id(1)
    n_used = nused_ref[0]
    used = i < n_used
    per = h_scr.shape[1] // LANES

    def issue_gather(tok_ref, slot):
        def body(rr, _):
            for prio in range(DMA_PRIORITIES):
                r = rr * DMA_PRIORITIES + prio
                src = pl.multiple_of(tok_ref[0, r] * per, per)
                dst = pl.multiple_of(r * per, per)
                pltpu.make_async_copy(h_hbm.at[pl.ds(src, per)], xbuf.at[slot, pl.ds(dst, per)],
                                      sems.at[slot]).start(priority=prio)
            return 0
        lax.fori_loop(0, tm // DMA_PRIORITIES, body, 0, unroll=4)

    @pl.when(jnp.logical_and(i == 0, j == 0))
    def _():
        issue_gather(tok_cur_ref, 0)

    @pl.when(jnp.logical_and(used, j == 0))
    def _():
        slot = i % 2
        pltpu.make_async_copy(xbuf.at[slot], xbuf.at[slot], sems.at[slot]).wait()
        for s, piece in enumerate(_load_row_tiles(xbuf.at[slot], tm, per)):
            h_scr[:, s * LANES:(s + 1) * LANES] = piece.astype(BF16)
        acc_scr[...] = jnp.zeros_like(acc_scr)

    @pl.when(used)
    def _():
        _swiglu_accumulate(h_scr, wg_ref, wu_ref, wd_ref, act_scr, acc_scr)

    @pl.when(jnp.logical_and(i + 1 < n_used, j == 0))
    def _():
        issue_gather(tok_next_ref, (i + 1) % 2)

    @pl.when(jnp.logical_and(used, j == pl.num_programs(1) - 1))
    def _():
        _store_row_tiles(o_ref, acc_scr[...])

    @pl.when(jnp.logical_and(jnp.logical_not(used), j == 0))
    def _():
        o_ref[...] = jnp.zeros_like(o_ref)


def expert_ffn(h, row_tok, block_expert, n_used, w_gate, w_up, w_down, layer, *, tm, tf=1792):
    n_rows = row_tok.shape[0]
    d = w_gate.shape[2]
    per = d // LANES
    ff = w_gate.shape[3]
    n_blocks = n_rows // tm
    last_j = ff // tf - 1

    def up_block(i, j, be, nu):
        return (layer, be[jnp.minimum(i, nu[0] - 1)], 0, jnp.where(i < nu[0], j, last_j))

    def down_block(i, j, be, nu):
        return (layer, be[jnp.minimum(i, nu[0] - 1)], jnp.where(i < nu[0], j, last_j), 0)

    grid_spec = pltpu.PrefetchScalarGridSpec(
        num_scalar_prefetch=2,
        grid=(n_blocks, ff // tf),
        in_specs=[pl.BlockSpec((None, 1, tm), lambda i, j, be, nu: (i, 0, 0),
                               memory_space=pltpu.SMEM),
                  pl.BlockSpec((None, 1, tm),
                               lambda i, j, be, nu: (jnp.minimum(i + 1, n_blocks - 1), 0, 0),
                               memory_space=pltpu.SMEM),
                  pl.BlockSpec(memory_space=pl.ANY),
                  pl.BlockSpec((None, None, d, tf), up_block),
                  pl.BlockSpec((None, None, d, tf), up_block),
                  pl.BlockSpec((None, None, tf, d), down_block)],
        out_specs=pl.BlockSpec((tm * per, LANES), lambda i, j, be, nu: (i, 0)),
        scratch_shapes=[pltpu.VMEM((2, tm * per, LANES), F32), pltpu.SemaphoreType.DMA((2,)),
                        pltpu.VMEM((tm, d), BF16), pltpu.VMEM((tm, tf), BF16),
                        pltpu.VMEM((tm, d), F32)])
    tok_blocks = row_tok.reshape(n_blocks, 1, tm)
    return pl.pallas_call(
        functools.partial(_expert_kernel, tm=tm),
        out_shape=jax.ShapeDtypeStruct((n_rows * per, LANES), F32),
        grid_spec=grid_spec,
        compiler_params=_params("arbitrary", "arbitrary"),
        name="expert_ffn",
    )(block_expert, n_used, tok_blocks, tok_blocks, h, w_gate, w_up, w_down)


def _combine_kernel(pos_cur_ref, pos_next_ref, x_ref, route_ref, y_ref, *rest, rows, final):
    if final:
        g_ref, o_ref, buf, sems = rest
    else:
        o_ref, buf, sems = rest
    i = pl.program_id(0)
    n = pl.num_programs(0)
    per = x_ref.shape[1] // LANES

    def issue(pos_ref, slot):
        def body(r, _):
            dst = pl.multiple_of(r * per, per)
            for kk in range(TOP_K):
                src = pl.multiple_of(pos_ref[0, TOP_K * r + kk] * per, per)
                pltpu.make_async_copy(y_ref.at[pl.ds(src, per)], buf.at[slot, kk, pl.ds(dst, per)],
                                      sems.at[slot]).start(priority=kk % DMA_PRIORITIES)
            return 0
        lax.fori_loop(0, rows, body, 0, unroll=4)

    @pl.when(i == 0)
    def _():
        issue(pos_cur_ref, 0)

    @pl.when(i + 1 < n)
    def _():
        issue(pos_next_ref, (i + 1) % 2)

    slot = i % 2
    pltpu.make_async_copy(buf.at[slot], buf.at[slot], sems.at[slot]).wait()
    route = route_ref[...]
    g1 = route[:, 2:3]
    g2 = route[:, 3:4]
    y1 = jnp.concatenate(_load_row_tiles(buf.at[slot, 0], rows, per), axis=-1)
    y2 = jnp.concatenate(_load_row_tiles(buf.at[slot, 1], rows, per), axis=-1)
    out = x_ref[...] + g1 * y1 + g2 * y2
    o_ref[...] = _rms(out, g_ref[...]) if final else out


def combine_residual(x, route, y_rows, pos, final_g=None, *, rows=256):
    t, d = x.shape
    final = final_g is not None
    n_steps = t // rows
    grid_spec = pltpu.PrefetchScalarGridSpec(
        num_scalar_prefetch=0,
        grid=(n_steps,),
        in_specs=[pl.BlockSpec((None, 1, TOP_K * rows), lambda i: (i, 0, 0),
                               memory_space=pltpu.SMEM),
                  pl.BlockSpec((None, 1, TOP_K * rows),
                               lambda i: (jnp.minimum(i + 1, n_steps - 1), 0, 0),
                               memory_space=pltpu.SMEM),
                  pl.BlockSpec((rows, d), lambda i: (i, 0)),
                  pl.BlockSpec((rows, LANES), lambda i: (i, 0)),
                  pl.BlockSpec(memory_space=pl.ANY)]
                 + ([pl.BlockSpec((1, d), lambda i: (0, 0))] if final else []),
        out_specs=pl.BlockSpec((rows, d), lambda i: (i, 0)),
        scratch_shapes=[pltpu.VMEM((2, TOP_K, rows * (d // LANES), LANES), F32),
                        pltpu.SemaphoreType.DMA((2,))])
    pos_blocks = pos.reshape(n_steps, 1, TOP_K * rows)
    extra = (final_g.reshape(1, d),) if final else ()
    return pl.pallas_call(
        functools.partial(_combine_kernel, rows=rows, final=final),
        out_shape=jax.ShapeDtypeStruct((t, d), F32),
        grid_spec=grid_spec,
        compiler_params=_params("arbitrary"),
        name="combine_residual",
    )(pos_blocks, pos_blocks, x, route, y_rows, *extra)


def _pad_cols(w, n):
    return jnp.pad(w, ((0, 0), (0, n - w.shape[1])))


def _scale_q_cols(w_qkv, width):
    scale = jnp.concatenate([jnp.full((width,), LOG2E * HEAD_DIM ** -0.5, F32),
                             jnp.ones((2 * width,), F32)])
    return w_qkv * scale


def even_layer(x, b, s, g_mix, g_ffn, w_in, b_forget, pool_w, pool_scale, w_out,
               w_gate, w_up, w_down, e, *, fox_tq=512):
    t, d = x.shape
    pool_width = pool_w.shape[0] * pool_w.shape[1]
    n_heads = b_forget.shape[0]
    fox_width = n_heads * HEAD_DIM
    w_a = w_in[:, :pool_width].astype(BF16)
    w_qkv = _scale_q_cols(w_in[:, pool_width:pool_width + 3 * fox_width], fox_width).astype(BF16)
    w_f = _pad_cols(w_in[:, pool_width + 3 * fox_width:], LANES).astype(BF16)
    qkv, a_in, f_pre = norm_proj(x, g_mix, [w_qkv, w_a, w_f], [BF16, F32, F32])

    pool_bd = jax.scipy.linalg.block_diag(*[pool_w[i] for i in range(pool_w.shape[0])]).astype(BF16)
    a_out, cum = pool_and_cum(a_in.reshape(b, s, pool_width), f_pre.reshape(b, s, LANES),
                              _pad_cols(b_forget.reshape(1, n_heads), LANES), pool_bd,
                              pool_scale.reshape(1, pool_width))
    cum_blocks = cum[:, :, :n_heads].reshape(b, s // fox_tq, fox_tq, n_heads // 2, 2)
    cum_blocks = cum_blocks.transpose(0, 3, 1, 4, 2)
    attn = fox_attention(qkv.reshape(b, s, 3 * fox_width), cum_blocks, tq=fox_tq)
    return out_proj_ffn(x, a_out.reshape(t, pool_width), attn.reshape(t, fox_width),
                        w_out.astype(BF16), g_ffn, w_gate, w_up, w_down, e)


def odd_layer(x, b, s, g_mix, g_ffn, w_in, b_igate, b_fgate, conv_w, head_norm_g, w_out,
              w_router, w_gate, w_up, w_down, e, final_g=None):
    c_out, d_out = odd_mixers(x, b, s, g_mix, w_in, b_igate, b_fgate, conv_w, head_norm_g)
    return moe_block(x, c_out, d_out, w_out.astype(BF16), g_ffn, w_router, w_gate, w_up, w_down,
                     e, final_g)


def odd_mixers(x, b, s, g_mix, w_in, b_igate, b_fgate, conv_w, head_norm_g):
    t, d = x.shape
    n_ml = b_igate.shape[0]
    ml_width = n_ml * LANES
    sb_width = d - ml_width
    cols = [3 * sb_width, 2 * ml_width, ml_width, ml_width]
    edges = [0]
    for c in cols:
        edges.append(edges[-1] + c)
    w_sb, w_mqk, w_mv, w_mo = [w_in[:, edges[i]:edges[i + 1]] for i in range(4)]
    w_sb = _scale_q_cols(w_sb, sb_width)
    w_sb, w_mqk, w_mv, w_mo = [w.astype(BF16) for w in (w_sb, w_mqk, w_mv, w_mo)]
    w_gates = _pad_cols(w_in[:, edges[-1]:], LANES).astype(BF16)
    sqkv, ml_qk, ml_v, ml_o, gates = norm_proj(
        x, g_mix, [w_sb, w_mqk, w_mv, w_mo, w_gates], [BF16, F32, BF16, F32, F32])

    c_out = sb_attention(sqkv.reshape(b, s, 3 * sb_width))

    k_scale = jnp.concatenate([jnp.ones((ml_width,), F32),
                               jnp.full((ml_width,), LANES ** -0.5, F32)]).reshape(1, 2 * ml_width)
    qk = conv_silu(ml_qk.reshape(b, s, 2 * ml_width), conv_w, k_scale)
    nb = s // MLSTM_BLOCK
    v3 = ml_v.reshape(b, s, ml_width)
    v_t = v3.reshape(b, nb, MLSTM_BLOCK, n_ml, LANES).transpose(0, 3, 1, 4, 2)
    gate_rows = gates[:, :2 * n_ml].reshape(b, nb, MLSTM_BLOCK, 2 * n_ml).transpose(0, 3, 1, 2)
    gate_rows = gate_rows.reshape(b, 2 * n_ml, nb, 1, MLSTM_BLOCK)
    d_out = mlstm(qk, v3, v_t, gate_rows[:, :n_ml], gate_rows[:, n_ml:],
                  ml_o.reshape(b, s, ml_width), b_igate, b_fgate, head_norm_g)
    return c_out.reshape(t, sb_width), d_out.reshape(t, ml_width)


def moe_block(x, mix_a, mix_b, w_out, g_ffn, w_router, w_gate, w_up, w_down, e, final_g=None, *,
              expert_rows=1024):
    t, d = x.shape
    x, h, route = out_proj_router(x, mix_a, mix_b, w_out, g_ffn, _pad_cols(w_router, LANES))
    flat_e = route[:, :TOP_K].astype(jnp.int32).reshape(-1)
    n_assign = t * TOP_K
    onehot = jax.nn.one_hot(flat_e, N_EXPERTS, dtype=jnp.int32)
    rank = jnp.sum((jnp.cumsum(onehot, axis=0) - onehot) * onehot, axis=-1)
    counts = jnp.sum(onehot, axis=0)
    padded = (counts + expert_rows - 1) // expert_rows * expert_rows
    group_end = jnp.cumsum(padded)
    pos = (group_end[flat_e] - padded[flat_e] + rank).astype(jnp.int32)
    n_rows = n_assign + N_EXPERTS * expert_rows
    n_blocks = n_rows // expert_rows
    block_expert = jnp.minimum(
        jnp.searchsorted(group_end, jnp.arange(n_blocks) * expert_rows, side='right'),
        N_EXPERTS - 1).astype(jnp.int32)
    order = jnp.argsort(pos).astype(jnp.int32)
    row_expert = jnp.repeat(block_expert, expert_rows)
    compact = (jnp.arange(n_rows, dtype=jnp.int32) - (group_end - padded)[row_expert]
               + (jnp.cumsum(counts) - counts)[row_expert])
    row_tok = (order // TOP_K)[jnp.clip(compact, 0, n_assign - 1)]
    n_used = (group_end[-1:] // expert_rows).astype(jnp.int32)

    y_rows = expert_ffn(h, row_tok, block_expert, n_used, w_gate, w_up, w_down, e, tm=expert_rows)
    return combine_residual(x, route, y_rows, pos, final_g)


def kernel(x, norm_mix_g, norm_ffn_g, norm_final_g, ev_w_in, ev_b_forget, ev_pool_w, ev_pool_scale, ev_w_out, ffn_w_gate, ffn_w_up, ffn_w_down, od_w_in, od_b_igate, od_b_fgate, od_conv_w, od_head_norm_g, od_w_out, moe_w_router, moe_w_gate, moe_w_up, moe_w_down):
    b, s, d = x.shape
    depth = norm_mix_g.shape[0]
    xt = x.reshape(b * s, d)
    ffn_w = [w.astype(BF16) for w in (ffn_w_gate, ffn_w_up, ffn_w_down)]
    moe_w = [w.astype(BF16) for w in (moe_w_gate, moe_w_up, moe_w_down)]
    for layer in range(depth):
        e = layer // 2
        if layer % 2 == 0:
            xt = even_layer(xt, b, s, norm_mix_g[layer], norm_ffn_g[layer], ev_w_in[e],
                            ev_b_forget[e], ev_pool_w[e], ev_pool_scale[e], ev_w_out[e],
                            *ffn_w, e)
        else:
            xt = odd_layer(xt, b, s, norm_mix_g[layer], norm_ffn_g[layer], od_w_in[e],
                           od_b_igate[e], od_b_fgate[e], od_conv_w[e], od_head_norm_g[e],
                           od_w_out[e], moe_w_router[e], *moe_w, e,
                           norm_final_g if layer == depth - 1 else None)
    assert depth % 2 == 0
    return xt.reshape(b, s, d)
```

```python
import functools

import jax
import jax.numpy as jnp
from jax import lax
from jax.experimental import pallas as pl
from jax.experimental.pallas import tpu as pltpu

F32 = jnp.float32
BF16 = jnp.bfloat16
EPS = 1e-6
NEG_INF = float("-inf")

LANES = 128
HEAD_DIM = 64
POOL_WINDOWS = (2, 4, 8, 16)
POOL_HALO = 16
CONV_WIDTH = 4
CONV_HALO = 8
FF_CHUNK = 512
MLSTM_BLOCK = 512
FOX_BLOCKS_PER_TRIP = 4
SB_BLOCKS_PER_TRIP = 4
DMA_PRIORITIES = 2
N_EXPERTS = 8
TOP_K = 2
VMEM_LIMIT = 56 * 1024 * 1024

LOG2E = 1.4426950408889634
SOFTPLUS2_CLAMP = 64.0
HIGHEST = lax.Precision.HIGHEST
NT_DIMS = (((1,), (1,)), ((), ()))


def _params(*semantics):
    return pltpu.CompilerParams(dimension_semantics=semantics, vmem_limit_bytes=VMEM_LIMIT)


def _rms(x, g):
    return x * lax.rsqrt(jnp.mean(x * x, axis=-1, keepdims=True) + EPS) * g


def _log_sigmoid(x):
    return jnp.minimum(x, 0.0) - jnp.log(1.0 + jnp.exp(-jnp.abs(x)))


def _sigmoid(x):
    return 1.0 / (1.0 + jnp.exp(-x))


def _norm_proj_kernel(x_ref, g_ref, *refs, n_out, col_chunk):
    w_refs, o_refs = refs[:n_out], refs[n_out:]
    h = _rms(x_ref[...], g_ref[...]).astype(BF16)
    for w_ref, o_ref in zip(w_refs, o_refs):
        n = w_ref.shape[1]
        for c0 in range(0, n, col_chunk):
            c1 = min(c0 + col_chunk, n)
            o_ref[:, c0:c1] = jnp.dot(
                h, w_ref[:, c0:c1], preferred_element_type=F32).astype(o_ref.dtype)


def norm_proj(x, g, weights, out_dtypes, *, tm=512, col_chunk=512):
    t, d = x.shape
    n_out = len(weights)
    return pl.pallas_call(
        functools.partial(_norm_proj_kernel, n_out=n_out, col_chunk=col_chunk),
        out_shape=[jax.ShapeDtypeStruct((t, w.shape[1]), dt) for w, dt in zip(weights, out_dtypes)],
        grid=(t // tm,),
        in_specs=[pl.BlockSpec((tm, d), lambda i: (i, 0)),
                  pl.BlockSpec((1, d), lambda i: (0, 0))]
                 + [pl.BlockSpec(w.shape, lambda i: (0, 0)) for w in weights],
        out_specs=[pl.BlockSpec((tm, w.shape[1]), lambda i: (i, 0)) for w in weights],
        compiler_params=_params("parallel"),
        name="norm_proj",
    )(x, g.reshape(1, d), *weights)


def _pool_kernel(a_ref, f_ref, bf_ref, w_ref, scale_ref, o_ref, cum_ref, *, rows):
    s, width = a_ref.shape
    group = lax.broadcasted_iota(jnp.int32, (1, width), 1) // (width // len(POOL_WINDOWS))
    window = jnp.where(group == 0, POOL_WINDOWS[0],
                       jnp.where(group == 1, POOL_WINDOWS[1],
                                 jnp.where(group == 2, POOL_WINDOWS[2], POOL_WINDOWS[3])))
    tri = (lax.broadcasted_iota(jnp.int32, (rows, rows), 0)
           >= lax.broadcasted_iota(jnp.int32, (rows, rows), 1)).astype(F32)
    carry = jnp.zeros((1, f_ref.shape[1]), F32)
    for r in range(s // rows):
        r0 = r * rows
        if r == 0:
            ext = jnp.concatenate([jnp.zeros((POOL_HALO, width), F32), a_ref[0:rows, :]], axis=0)
        else:
            ext = a_ref[r0 - POOL_HALO:r0 + rows, :]
        s2 = ext + pltpu.roll(ext, 1, 0)
        s4 = s2 + pltpu.roll(s2, 2, 0)
        s8 = s4 + pltpu.roll(s4, 4, 0)
        s16 = s8 + pltpu.roll(s8, 8, 0)
        wsum = jnp.where(group == 0, s2, jnp.where(group == 1, s4, jnp.where(group == 2, s8, s16)))
        wsum = wsum[POOL_HALO:, :]
        a = ext[POOL_HALO:, :]
        t1 = r0 + 1 + lax.broadcasted_iota(jnp.int32, (rows, 1), 0)
        count = jnp.minimum(t1, window).astype(F32)
        pooled = wsum / count - a
        mixed = jnp.dot(pooled.astype(BF16), w_ref[...], preferred_element_type=F32)
        o_ref[r0:r0 + rows, :] = (mixed * scale_ref[...]).astype(o_ref.dtype)
        lf = _log_sigmoid(f_ref[r0:r0 + rows, :] + bf_ref[...]) * LOG2E
        cum = jnp.dot(tri, lf, preferred_element_type=F32, precision=HIGHEST) + carry
        cum_ref[r0:r0 + rows, :] = cum
        carry = cum[rows - 1:rows, :]


def pool_and_cum(a_in, f_pre, b_forget, pool_w_bd, pool_scale, *, rows=256):
    b, s, width = a_in.shape
    fw = f_pre.shape[-1]
    return pl.pallas_call(
        functools.partial(_pool_kernel, rows=rows),
        out_shape=[jax.ShapeDtypeStruct((b, s, width), BF16),
                   jax.ShapeDtypeStruct((b, s, fw), F32)],
        grid=(b,),
        in_specs=[pl.BlockSpec((None, s, width), lambda i: (i, 0, 0)),
                  pl.BlockSpec((None, s, fw), lambda i: (i, 0, 0)),
                  pl.BlockSpec((1, fw), lambda i: (0, 0)),
                  pl.BlockSpec((width, width), lambda i: (0, 0)),
                  pl.BlockSpec((1, width), lambda i: (0, 0))],
        out_specs=[pl.BlockSpec((None, s, width), lambda i: (i, 0, 0)),
                   pl.BlockSpec((None, s, fw), lambda i: (i, 0, 0))],
        compiler_params=_params("parallel"),
        name="pool_and_cum",
    )(a_in, f_pre, b_forget, pool_w_bd, pool_scale)


def _fox_kernel(q_ref, k_ref, v_ref, cum_ref, o_ref, *, tq):
    i = pl.program_id(2)
    lane = lax.broadcasted_iota(jnp.int32, (1, LANES), 1)
    first = lane < HEAD_DIM
    q = q_ref[...]
    q_heads = (jnp.where(first, q, jnp.zeros_like(q)), jnp.where(first, jnp.zeros_like(q), q))
    row = lax.broadcasted_iota(jnp.int32, (tq, tq), 0)
    col = lax.broadcasted_iota(jnp.int32, (tq, tq), 1)

    def block(j, carry, masked):
        ms, accs = carry
        start = pl.multiple_of(j * tq, tq)
        k = k_ref[pl.ds(start, tq), :]
        v = v_ref[pl.ds(start, tq), :]
        ones = jnp.ones_like(v)
        v_heads = (jnp.where(first, v, ones), jnp.where(first, ones, v))
        ck = cum_ref[j]
        new_ms, new_accs = [], []
        scs = [lax.dot_general(q_heads[h], k, NT_DIMS, preferred_element_type=F32)
               for h in range(2)]
        for h in range(2):
            sc = scs[h] - ck[h:h + 1, :]
            if masked:
                sc = jnp.where(col <= row, sc, NEG_INF)
            m_new = jnp.maximum(ms[h], jnp.max(sc, axis=-1, keepdims=True))
            alpha = jnp.exp2(ms[h] - m_new)
            p = jnp.exp2((sc - m_new).astype(BF16))
            new_ms.append(m_new)
            new_accs.append(alpha * accs[h] + jnp.dot(p, v_heads[h], preferred_element_type=F32))
        return tuple(new_ms), tuple(new_accs)

    init = ((jnp.full((tq, 1), NEG_INF, F32),) * 2, (jnp.zeros((tq, LANES), F32),) * 2)
    def run(first_block, count, c):
        for d in range(count):
            c = block(first_block + d, c, False)
        return c

    n_trips = i // FOX_BLOCKS_PER_TRIP
    carry = lax.fori_loop(
        0, n_trips, lambda t, c: run(FOX_BLOCKS_PER_TRIP * t, FOX_BLOCKS_PER_TRIP, c), init)
    def tail(n_left):
        return lambda c: block(i, run(i - n_left, n_left, c), True)

    left = i - FOX_BLOCKS_PER_TRIP * n_trips
    _, accs = lax.switch(left, [tail(n) for n in range(FOX_BLOCKS_PER_TRIP)], carry)
    out0 = accs[0] / accs[0][:, HEAD_DIM:HEAD_DIM + 1]
    out1 = accs[1] / accs[1][:, 0:1]
    o_ref[...] = jnp.where(first, out0, out1).astype(o_ref.dtype)


def fox_attention(qkv, cum_blocks, *, tq=512):
    b, s, w3 = qkv.shape
    n_pairs = w3 // 3 // LANES
    return pl.pallas_call(
        functools.partial(_fox_kernel, tq=tq),
        out_shape=jax.ShapeDtypeStruct((b, s, w3 // 3), BF16),
        grid=(b, n_pairs, s // tq),
        in_specs=[pl.BlockSpec((None, tq, LANES), lambda bi, p, i: (bi, i, p)),
                  pl.BlockSpec((None, s, LANES), lambda bi, p, i: (bi, 0, n_pairs + p)),
                  pl.BlockSpec((None, s, LANES), lambda bi, p, i: (bi, 0, 2 * n_pairs + p)),
                  pl.BlockSpec((None, None, s // tq, 2, tq), lambda bi, p, i: (bi, p, 0, 0, 0))],
        out_specs=pl.BlockSpec((None, tq, LANES), lambda bi, p, i: (bi, i, p)),
        compiler_params=_params("parallel", "parallel", "arbitrary"),
        name="fox_attention",
    )(qkv, qkv, qkv, cum_blocks)


def _sb_kernel(q_ref, k_ref, v_ref, o_ref, *, tq, tk):
    i = pl.program_id(2)
    per = tq // tk
    lane = lax.broadcasted_iota(jnp.int32, (1, LANES), 1)
    first = lane < HEAD_DIM
    q = q_ref[...]
    q_heads = (jnp.where(first, q, jnp.zeros_like(q)), jnp.where(first, jnp.zeros_like(q), q))
    row = lax.broadcasted_iota(jnp.int32, (tq, tk), 0)
    col = lax.broadcasted_iota(jnp.int32, (tq, tk), 1)
    neg_suffix = jnp.where(lax.broadcasted_iota(jnp.int32, (tk, tk), 0)
                           > lax.broadcasted_iota(jnp.int32, (tk, tk), 1), -1.0, 0.0).astype(BF16)

    def logits(j, r0=0):
        kt = k_ref[j]
        return tuple(jnp.dot(q_heads[h][r0:], kt, preferred_element_type=F32) for h in range(2))

    def block(j, zs, carry, diagonal, r0=0):
        used, acc = carry
        start = pl.multiple_of(j * tk, tk)
        v = v_ref[pl.ds(start, tk), :]
        if diagonal:
            valid = (col < row)[:tq - r0]
        new_used, pvs = [], []
        for h in range(2):
            z = zs[h]
            sp = jnp.maximum(z, jnp.log2(1.0 + jnp.exp2(jnp.minimum(z, SOFTPLUS2_CLAMP))))
            if diagonal:
                sp = jnp.where(valid, sp, 0.0)
            later_in = jnp.dot(sp.astype(BF16), neg_suffix, preferred_element_type=F32)
            w = jnp.exp2((z - sp + (later_in - used[h][r0:])).astype(BF16))
            if diagonal:
                w = jnp.where(valid, w, jnp.zeros_like(w))
            pvs.append(jnp.dot(w, v, preferred_element_type=F32))
            grown = used[h][r0:] + (sp[:, 0:1] - later_in[:, 0:1])
            new_used.append(jnp.concatenate([used[h][:r0], grown], axis=0) if r0 else grown)
        pv = acc[r0:] + jnp.where(first, pvs[0], pvs[1])
        return tuple(new_used), (jnp.concatenate([acc[:r0], pv], axis=0) if r0 else pv)

    def run(first_block, count, carry):
        for d in range(count):
            carry = block(first_block - d, logits(first_block - d), carry, False)
        return carry

    assert per == 2 and SB_BLOCKS_PER_TRIP == 4
    n_full = i * per
    left = n_full % SB_BLOCKS_PER_TRIP

    def head(n_left):
        def f(carry):
            for d in range(per):
                j, r0 = (i + 1) * per - 1 - d, (per - 1 - d) * tk
                carry = block(j, logits(j, r0), carry, True, r0)
            return run(n_full - 1, n_left, carry)
        return f

    carry = ((jnp.zeros((tq, 1), F32),) * 2, jnp.zeros((tq, LANES), F32))
    carry = lax.switch(left // 2, [head(0), head(2)], carry)
    rest = n_full - left
    _, acc = lax.fori_loop(
        0, rest // SB_BLOCKS_PER_TRIP,
        lambda t, c: run(rest - 1 - SB_BLOCKS_PER_TRIP * t, SB_BLOCKS_PER_TRIP, c), carry)
    o_ref[...] = acc.astype(o_ref.dtype)


def sb_attention(qkv, *, tq=512, tk=256):
    b, s, w3 = qkv.shape
    width = w3 // 3
    n_pairs = width // LANES
    k_t = qkv[:, :, width:2 * width].reshape(b, s // tk, tk, n_pairs, LANES).transpose(0, 3, 1, 4, 2)
    return pl.pallas_call(
        functools.partial(_sb_kernel, tq=tq, tk=tk),
        out_shape=jax.ShapeDtypeStruct((b, s, w3 // 3), BF16),
        grid=(b, n_pairs, s // tq),
        in_specs=[pl.BlockSpec((None, tq, LANES), lambda bi, p, i: (bi, i, p)),
                  pl.BlockSpec((None, None, s // tk, LANES, tk), lambda bi, p, i: (bi, p, 0, 0, 0)),
                  pl.BlockSpec((None, s, LANES), lambda bi, p, i: (bi, 0, 2 * n_pairs + p))],
        out_specs=pl.BlockSpec((None, tq, LANES), lambda bi, p, i: (bi, i, p)),
        compiler_params=_params("parallel", "parallel", "arbitrary"),
        name="sb_attention",
    )(qkv, k_t, qkv)


def _conv_kernel(x_ref, halo_ref, w_ref, colscale_ref, o_ref):
    i = pl.program_id(1)
    halo = jnp.where(i == 0, 0.0, halo_ref[...])
    ext = jnp.concatenate([halo, x_ref[...]], axis=0)
    y = ext * w_ref[CONV_WIDTH - 1:CONV_WIDTH, :]
    for back in range(1, CONV_WIDTH):
        y = y + pltpu.roll(ext, back, 0) * w_ref[CONV_WIDTH - 1 - back:CONV_WIDTH - back, :]
    y = y[CONV_HALO:, :]
    o_ref[...] = (y * _sigmoid(y) * colscale_ref[...]).astype(o_ref.dtype)


def conv_silu(x, conv_w, colscale, *, rows=512):
    b, s, c = x.shape
    per = rows // CONV_HALO
    return pl.pallas_call(
        _conv_kernel,
        out_shape=jax.ShapeDtypeStruct((b, s, c), BF16),
        grid=(b, s // rows),
        in_specs=[pl.BlockSpec((None, rows, c), lambda bi, i: (bi, i, 0)),
                  pl.BlockSpec((None, CONV_HALO, c),
                               lambda bi, i: (bi, jnp.maximum(i * per - 1, 0), 0)),
                  pl.BlockSpec((CONV_WIDTH, c), lambda bi, i: (0, 0)),
                  pl.BlockSpec((1, c), lambda bi, i: (0, 0))],
        out_specs=pl.BlockSpec((None, rows, c), lambda bi, i: (bi, i, 0)),
        compiler_params=_params("parallel", "parallel"),
        name="conv_silu",
    )(x, x, conv_w, colscale)


def _mlstm_kernel(bi_ref, bf_ref, q_ref, k_ref, v_ref, vt_ref, li_ref, lf_ref, o_ref, g_ref,
                  out_ref, c_scr, n_scr, m_scr):
    head = pl.program_id(1)
    step = pl.program_id(2)
    L = q_ref.shape[0]

    @pl.when(step == 0)
    def _():
        c_scr[...] = jnp.zeros_like(c_scr)
        n_scr[...] = jnp.zeros_like(n_scr)
        m_scr[...] = jnp.zeros_like(m_scr)

    r_idx = lax.broadcasted_iota(jnp.int32, (L, L), 0)
    c_idx = lax.broadcasted_iota(jnp.int32, (L, L), 1)
    lower = r_idx >= c_idx
    tri_u = (r_idx <= c_idx).astype(F32)

    q = q_ref[...]
    k = k_ref[...]
    c_st = c_scr[...]
    n_st = n_scr[...]
    m_st = m_scr[...]
    lf_row = _log_sigmoid(lf_ref[...] + bf_ref[head])
    li_row = li_ref[...] + bi_ref[head]
    bcum_row = jnp.dot(jnp.broadcast_to(lf_row, (8, L)), tri_u, preferred_element_type=F32,
                       precision=HIGHEST)[0:1, :]
    bcum_col = jnp.broadcast_to(bcum_row, (LANES, L)).T[:, 0:1]
    g = jnp.sum(lf_row, axis=-1, keepdims=True)

    log_d = jnp.where(lower, bcum_col - bcum_row + li_row, NEG_INF)
    inter = bcum_col + m_st
    m_out = jnp.maximum(inter, jnp.max(log_d, axis=-1, keepdims=True))
    d_mat = jnp.exp(log_d - m_out)
    inter_w = jnp.exp(inter - m_out)
    qk = lax.dot_general(q, k, NT_DIMS, preferred_element_type=F32) * d_mat
    num = (jnp.dot(qk.astype(BF16), v_ref[...], preferred_element_type=F32)
           + inter_w * lax.dot_general(q, c_st.astype(BF16), NT_DIMS, preferred_element_type=F32))
    den = (jnp.sum(qk, axis=-1, keepdims=True)
           + inter_w * jnp.sum(q.astype(F32) * n_st, axis=-1, keepdims=True))
    hh = num / jnp.maximum(jnp.abs(den), jnp.exp(-m_out))
    hn = _rms(hh, g_ref[...])
    out_ref[...] = (hn * _sigmoid(o_ref[...])).astype(out_ref.dtype)

    a_row = g - bcum_row + li_row
    m_loc = jnp.max(a_row, axis=-1, keepdims=True)
    w_row = jnp.exp(a_row - m_loc)
    c_loc = jnp.dot((vt_ref[...].astype(F32) * w_row).astype(BF16), k, preferred_element_type=F32)
    n_loc = jnp.dot(jnp.broadcast_to(w_row, (8, L)).astype(BF16), k,
                    preferred_element_type=F32)[0:1, :]
    m_new = jnp.maximum(g + m_st, m_loc)
    decay = jnp.exp(g + m_st - m_new)
    fresh = jnp.exp(m_loc - m_new)
    c_scr[...] = decay * c_st + fresh * c_loc
    n_scr[...] = decay * n_st + fresh * n_loc
    m_scr[...] = m_new


def mlstm(qk, v, v_t, li, lf, o_gate, b_i, b_f, head_g):
    b, s, w = v.shape
    n_heads = w // LANES
    rows = v_t.shape[-1]
    grid_spec = pltpu.PrefetchScalarGridSpec(
        num_scalar_prefetch=2,
        grid=(b, n_heads, s // rows),
        in_specs=[pl.BlockSpec((None, rows, LANES), lambda bi, h, c, *_: (bi, c, h)),
                  pl.BlockSpec((None, rows, LANES), lambda bi, h, c, *_: (bi, c, n_heads + h)),
                  pl.BlockSpec((None, rows, LANES), lambda bi, h, c, *_: (bi, c, h)),
                  pl.BlockSpec((None, None, None, LANES, rows),
                               lambda bi, h, c, *_: (bi, h, c, 0, 0)),
                  pl.BlockSpec((None, None, None, 1, rows), lambda bi, h, c, *_: (bi, h, c, 0, 0)),
                  pl.BlockSpec((None, None, None, 1, rows), lambda bi, h, c, *_: (bi, h, c, 0, 0)),
                  pl.BlockSpec((None, rows, LANES), lambda bi, h, c, *_: (bi, c, h)),
                  pl.BlockSpec((None, 1, LANES), lambda bi, h, c, *_: (h, 0, 0))],
        out_specs=pl.BlockSpec((None, rows, LANES), lambda bi, h, c, *_: (bi, c, h)),
        scratch_shapes=[pltpu.VMEM((LANES, LANES), F32), pltpu.VMEM((1, LANES), F32),
                        pltpu.VMEM((1, 1), F32)])
    return pl.pallas_call(
        _mlstm_kernel,
        out_shape=jax.ShapeDtypeStruct((b, s, w), BF16),
        grid_spec=grid_spec,
        compiler_params=_params("parallel", "parallel", "arbitrary"),
        name="mlstm",
    )(b_i, b_f, qk, qk, v, v_t, li, lf, o_gate, head_g.reshape(n_heads, 1, LANES))


def _ff_chunks(tf):
    return [(c0, min(c0 + FF_CHUNK, tf)) for c0 in range(0, tf, FF_CHUNK)]


def _swiglu_accumulate(h_scr, wg_ref, wu_ref, wd_ref, act_scr, o_ref, after_chunk=None):
    h = h_scr[...]
    for c, (c0, c1) in enumerate(_ff_chunks(wg_ref.shape[1])):
        gate = jnp.dot(h, wg_ref[:, c0:c1], preferred_element_type=F32)
        up = jnp.dot(h, wu_ref[:, c0:c1], preferred_element_type=F32)
        act_scr[:, c0:c1] = (gate * _sigmoid(gate) * up).astype(BF16)
        if after_chunk is not None:
            after_chunk(c)
    o_ref[...] += jnp.dot(act_scr[...], wd_ref[...], preferred_element_type=F32)


def _ffn_kernel(x_ref, a_ref, b_ref, wo_ref, g_ref, wg_ref, wu_ref, wd_ref, o_ref, h_scr, act_scr):
    @pl.when(pl.program_id(1) == 0)
    def _():
        na = a_ref.shape[1]
        x = x_ref[...] + jnp.dot(a_ref[...], wo_ref[0:na, :], preferred_element_type=F32)
        x = x + jnp.dot(b_ref[...], wo_ref[na:, :], preferred_element_type=F32)
        h_scr[...] = _rms(x, g_ref[...]).astype(BF16)
        o_ref[...] = x

    _swiglu_accumulate(h_scr, wg_ref, wu_ref, wd_ref, act_scr, o_ref)


def out_proj_ffn(x, a, b_part, w_out, g, w_gate, w_up, w_down, layer, *, tm=1024, tf=1792):
    t, d = x.shape
    ff = w_gate.shape[2]
    return pl.pallas_call(
        _ffn_kernel,
        out_shape=jax.ShapeDtypeStruct((t, d), F32),
        grid=(t // tm, ff // tf),
        in_specs=[pl.BlockSpec((tm, d), lambda i, j: (i, 0)),
                  pl.BlockSpec((tm, a.shape[1]), lambda i, j: (i, 0)),
                  pl.BlockSpec((tm, b_part.shape[1]), lambda i, j: (i, 0)),
                  pl.BlockSpec(w_out.shape, lambda i, j: (0, 0)),
                  pl.BlockSpec((1, d), lambda i, j: (0, 0)),
                  pl.BlockSpec((None, d, tf), lambda i, j: (layer, 0, j)),
                  pl.BlockSpec((None, d, tf), lambda i, j: (layer, 0, j)),
                  pl.BlockSpec((None, tf, d), lambda i, j: (layer, j, 0))],
        out_specs=pl.BlockSpec((tm, d), lambda i, j: (i, 0)),
        scratch_shapes=[pltpu.VMEM((tm, d), BF16), pltpu.VMEM((tm, tf), BF16)],
        compiler_params=_params("parallel", "arbitrary"),
        name="out_proj_ffn",
    )(x, a, b_part, w_out, g.reshape(1, d), w_gate, w_up, w_down)


def _store_row_tiles(ref, x):
    rows, d = x.shape
    per = d // LANES
    for s in range(per):
        ref[pl.ds(s, rows, stride=per), :] = x[:, s * LANES:(s + 1) * LANES]


def _load_row_tiles(ref, rows, per):
    return [ref[pl.ds(s, rows, stride=per), :] for s in range(per)]


def _router_kernel(x_ref, a_ref, b_ref, wo_ref, g_ref, wr_ref, x1_ref, h_ref, route_ref):
    na = a_ref.shape[1]
    x1 = x_ref[...] + jnp.dot(a_ref[...], wo_ref[0:na, :], preferred_element_type=F32)
    x1 = x1 + jnp.dot(b_ref[...], wo_ref[na:, :], preferred_element_type=F32)
    x1_ref[...] = x1
    h = _rms(x1, g_ref[...])
    _store_row_tiles(h_ref, h)
    h_hi = h.astype(BF16)
    h_lo = (h - h_hi.astype(F32)).astype(BF16)
    hi_terms = jnp.dot(h_hi, wr_ref[...], preferred_element_type=F32)
    logits = (hi_terms[:, :LANES] + hi_terms[:, LANES:]
              + jnp.dot(h_lo, wr_ref[:, :LANES], preferred_element_type=F32))
    lane = lax.broadcasted_iota(jnp.int32, logits.shape, 1)
    logits = jnp.where(lane < N_EXPERTS, logits, NEG_INF)
    m1 = jnp.max(logits, axis=-1, keepdims=True)
    i1 = jnp.min(jnp.where(logits == m1, lane, LANES), axis=-1, keepdims=True)
    rest = jnp.where(lane == i1, NEG_INF, logits)
    m2 = jnp.max(rest, axis=-1, keepdims=True)
    i2 = jnp.min(jnp.where(rest == m2, lane, LANES), axis=-1, keepdims=True)
    e2 = jnp.exp(m2 - m1)
    g1 = 1.0 / (1.0 + e2)
    g2 = e2 / (1.0 + e2)
    route_ref[...] = jnp.where(lane == 0, i1.astype(F32),
                               jnp.where(lane == 1, i2.astype(F32),
                                         jnp.where(lane == 2, g1, jnp.where(lane == 3, g2, 0.0))))


def out_proj_router(x, a, b_part, w_out, g, w_router_padded, *, tm=512):
    t, d = x.shape
    per = d // LANES
    w_hi = w_router_padded.astype(BF16)
    w_lo = (w_router_padded - w_hi.astype(F32)).astype(BF16)
    w_pieces = jnp.concatenate([w_hi, w_lo], axis=1)
    return pl.pallas_call(
        _router_kernel,
        out_shape=[jax.ShapeDtypeStruct((t, d), F32),
                   jax.ShapeDtypeStruct((t * per, LANES), F32),
                   jax.ShapeDtypeStruct((t, LANES), F32)],
        grid=(t // tm,),
        in_specs=[pl.BlockSpec((tm, d), lambda i: (i, 0)),
                  pl.BlockSpec((tm, a.shape[1]), lambda i: (i, 0)),
                  pl.BlockSpec((tm, b_part.shape[1]), lambda i: (i, 0)),
                  pl.BlockSpec(w_out.shape, lambda i: (0, 0)),
                  pl.BlockSpec((1, d), lambda i: (0, 0)),
                  pl.BlockSpec((d, 2 * LANES), lambda i: (0, 0))],
        out_specs=[pl.BlockSpec((tm, d), lambda i: (i, 0)),
                   pl.BlockSpec((tm * per, LANES), lambda i: (i, 0)),
                   pl.BlockSpec((tm, LANES), lambda i: (i, 0))],
        compiler_params=_params("parallel"),
        name="out_proj_router",
    )(x, a, b_part, w_out, g.reshape(1, d), w_pieces)


def _expert_kernel(be_ref, nused_ref, tok_cur_ref, tok_next_ref, h_hbm, wg_ref, wu_ref, wd_ref,
                   o_ref, xbuf, sems, h_scr, act_scr, acc_scr, *, tm):
    i = pl.program_id(0)
    j = pl.program_id(1)
    n_used = nused_ref[0]
    used = i < n_used
    per = h_scr.shape[1] // LANES

    def issue_gather(tok_ref, slot):
        def body(rr, _):
            for prio in range(DMA_PRIORITIES):
                r = rr * DMA_PRIORITIES + prio
                src = pl.multiple_of(tok_ref[0, r] * per, per)
                dst = pl.multiple_of(r * per, per)
                pltpu.make_async_copy(h_hbm.at[pl.ds(src, per)], xbuf.at[slot, pl.ds(dst, per)],
                                      sems.at[slot]).start(priority=prio)
            return 0
        lax.fori_loop(0, tm // DMA_PRIORITIES, body, 0, unroll=4)

    @pl.when(jnp.logical_and(i == 0, j == 0))
    def _():
        issue_gather(tok_cur_ref, 0)

    @pl.when(jnp.logical_and(used, j == 0))
    def _():
        slot = i % 2
        pltpu.make_async_copy(xbuf.at[slot], xbuf.at[slot], sems.at[slot]).wait()
        for s, piece in enumerate(_load_row_tiles(xbuf.at[slot], tm, per)):
            h_scr[:, s * LANES:(s + 1) * LANES] = piece.astype(BF16)
        acc_scr[...] = jnp.zeros_like(acc_scr)

    @pl.when(used)
    def _():
        _swiglu_accumulate(h_scr, wg_ref, wu_ref, wd_ref, act_scr, acc_scr)

    @pl.when(jnp.logical_and(i + 1 < n_used, j == 0))
    def _():
        issue_gather(tok_next_ref, (i + 1) % 2)

    @pl.when(jnp.logical_and(used, j == pl.num_programs(1) - 1))
    def _():
        _store_row_tiles(o_ref, acc_scr[...])

    @pl.when(jnp.logical_and(jnp.logical_not(used), j == 0))
    def _():
        o_ref[...] = jnp.zeros_like(o_ref)


def expert_ffn(h, row_tok, block_expert, n_used, w_gate, w_up, w_down, layer, *, tm, tf=1792):
    n_rows = row_tok.shape[0]
    d = w_gate.shape[2]
    per = d // LANES
    ff = w_gate.shape[3]
    n_blocks = n_rows // tm
    last_j = ff // tf - 1

    def up_block(i, j, be, nu):
        return (layer, be[jnp.minimum(i, nu[0] - 1)], 0, jnp.where(i < nu[0], j, last_j))

    def down_block(i, j, be, nu):
        return (layer, be[jnp.minimum(i, nu[0] - 1)], jnp.where(i < nu[0], j, last_j), 0)

    grid_spec = pltpu.PrefetchScalarGridSpec(
        num_scalar_prefetch=2,
        grid=(n_blocks, ff // tf),
        in_specs=[pl.BlockSpec((None, 1, tm), lambda i, j, be, nu: (i, 0, 0),
                               memory_space=pltpu.SMEM),
                  pl.BlockSpec((None, 1, tm),
                               lambda i, j, be, nu: (jnp.minimum(i + 1, n_blocks - 1), 0, 0),
                               memory_space=pltpu.SMEM),
                  pl.BlockSpec(memory_space=pl.ANY),
                  pl.BlockSpec((None, None, d, tf), up_block),
                  pl.BlockSpec((None, None, d, tf), up_block),
                  pl.BlockSpec((None, None, tf, d), down_block)],
        out_specs=pl.BlockSpec((tm * per, LANES), lambda i, j, be, nu: (i, 0)),
        scratch_shapes=[pltpu.VMEM((2, tm * per, LANES), F32), pltpu.SemaphoreType.DMA((2,)),
                        pltpu.VMEM((tm, d), BF16), pltpu.VMEM((tm, tf), BF16),
                        pltpu.VMEM((tm, d), F32)])
    tok_blocks = row_tok.reshape(n_blocks, 1, tm)
    return pl.pallas_call(
        functools.partial(_expert_kernel, tm=tm),
        out_shape=jax.ShapeDtypeStruct((n_rows * per, LANES), F32),
        grid_spec=grid_spec,
        compiler_params=_params("arbitrary", "arbitrary"),
        name="expert_ffn",
    )(block_expert, n_used, tok_blocks, tok_blocks, h, w_gate, w_up, w_down)


def _combine_kernel(pos_cur_ref, pos_next_ref, x_ref, route_ref, y_ref, *rest, rows, final):
    if final:
        g_ref, o_ref, buf, sems = rest
    else:
        o_ref, buf, sems = rest
    i = pl.program_id(0)
    n = pl.num_programs(0)
    per = x_ref.shape[1] // LANES

    def issue(pos_ref, slot):
        def body(r, _):
            dst = pl.multiple_of(r * per, per)
            for kk in range(TOP_K):
                src = pl.multiple_of(pos_ref[0, TOP_K * r + kk] * per, per)
                pltpu.make_async_copy(y_ref.at[pl.ds(src, per)], buf.at[slot, kk, pl.ds(dst, per)],
                                      sems.at[slot]).start(priority=kk % DMA_PRIORITIES)
            return 0
        lax.fori_loop(0, rows, body, 0, unroll=4)

    @pl.when(i == 0)
    def _():
        issue(pos_cur_ref, 0)

    @pl.when(i + 1 < n)
    def _():
        issue(pos_next_ref, (i + 1) % 2)

    slot = i % 2
    pltpu.make_async_copy(buf.at[slot], buf.at[slot], sems.at[slot]).wait()
    route = route_ref[...]
    g1 = route[:, 2:3]
    g2 = route[:, 3:4]
    y1 = jnp.concatenate(_load_row_tiles(buf.at[slot, 0], rows, per), axis=-1)
    y2 = jnp.concatenate(_load_row_tiles(buf.at[slot, 1], rows, per), axis=-1)
    out = x_ref[...] + g1 * y1 + g2 * y2
    o_ref[...] = _rms(out, g_ref[...]) if final else out


def combine_residual(x, route, y_rows, pos, final_g=None, *, rows=256):
    t, d = x.shape
    final = final_g is not None
    n_steps = t // rows
    grid_spec = pltpu.PrefetchScalarGridSpec(
        num_scalar_prefetch=0,
        grid=(n_steps,),
        in_specs=[pl.BlockSpec((None, 1, TOP_K * rows), lambda i: (i, 0, 0),
                               memory_space=pltpu.SMEM),
                  pl.BlockSpec((None, 1, TOP_K * rows),
                               lambda i: (jnp.minimum(i + 1, n_steps - 1), 0, 0),
                               memory_space=pltpu.SMEM),
                  pl.BlockSpec((rows, d), lambda i: (i, 0)),
                  pl.BlockSpec((rows, LANES), lambda i: (i, 0)),
                  pl.BlockSpec(memory_space=pl.ANY)]
                 + ([pl.BlockSpec((1, d), lambda i: (0, 0))] if final else []),
        out_specs=pl.BlockSpec((rows, d), lambda i: (i, 0)),
        scratch_shapes=[pltpu.VMEM((2, TOP_K, rows * (d // LANES), LANES), F32),
                        pltpu.SemaphoreType.DMA((2,))])
    pos_blocks = pos.reshape(n_steps, 1, TOP_K * rows)
    extra = (final_g.reshape(1, d),) if final else ()
    return pl.pallas_call(
        functools.partial(_combine_kernel, rows=rows, final=final),
        out_shape=jax.ShapeDtypeStruct((t, d), F32),
        grid_spec=grid_spec,
        compiler_params=_params("arbitrary"),
        name="combine_residual",
    )(pos_blocks, pos_blocks, x, route, y_rows, *extra)


def _pad_cols(w, n):
    return jnp.pad(w, ((0, 0), (0, n - w.shape[1])))


def _scale_q_cols(w_qkv, width):
    scale = jnp.concatenate([jnp.full((width,), LOG2E * HEAD_DIM ** -0.5, F32),
                             jnp.ones((2 * width,), F32)])
    return w_qkv * scale


def even_layer(x, b, s, g_mix, g_ffn, w_in, b_forget, pool_w, pool_scale, w_out,
               w_gate, w_up, w_down, e, *, fox_tq=512):
    t, d = x.shape
    pool_width = pool_w.shape[0] * pool_w.shape[1]
    n_heads = b_forget.shape[0]
    fox_width = n_heads * HEAD_DIM
    w_a = w_in[:, :pool_width].astype(BF16)
    w_qkv = _scale_q_cols(w_in[:, pool_width:pool_width + 3 * fox_width], fox_width).astype(BF16)
    w_f = _pad_cols(w_in[:, pool_width + 3 * fox_width:], LANES).astype(BF16)
    qkv, a_in, f_pre = norm_proj(x, g_mix, [w_qkv, w_a, w_f], [BF16, F32, F32])

    pool_bd = jax.scipy.linalg.block_diag(*[pool_w[i] for i in range(pool_w.shape[0])]).astype(BF16)
    a_out, cum = pool_and_cum(a_in.reshape(b, s, pool_width), f_pre.reshape(b, s, LANES),
                              _pad_cols(b_forget.reshape(1, n_heads), LANES), pool_bd,
                              pool_scale.reshape(1, pool_width))
    cum_blocks = cum[:, :, :n_heads].reshape(b, s // fox_tq, fox_tq, n_heads // 2, 2)
    cum_blocks = cum_blocks.transpose(0, 3, 1, 4, 2)
    attn = fox_attention(qkv.reshape(b, s, 3 * fox_width), cum_blocks, tq=fox_tq)
    return out_proj_ffn(x, a_out.reshape(t, pool_width), attn.reshape(t, fox_width),
                        w_out.astype(BF16), g_ffn, w_gate, w_up, w_down, e)


def odd_layer(x, b, s, g_mix, g_ffn, w_in, b_igate, b_fgate, conv_w, head_norm_g, w_out,
              w_router, w_gate, w_up, w_down, e, final_g=None):
    c_out, d_out = odd_mixers(x, b, s, g_mix, w_in, b_igate, b_fgate, conv_w, head_norm_g)
    return moe_block(x, c_out, d_out, w_out.astype(BF16), g_ffn, w_router, w_gate, w_up, w_down,
                     e, final_g)


def odd_mixers(x, b, s, g_mix, w_in, b_igate, b_fgate, conv_w, head_norm_g):
    t, d = x.shape
    n_ml = b_igate.shape[0]
    ml_width = n_ml * LANES
    sb_width = d - ml_width
    cols = [3 * sb_width, 2 * ml_width, ml_width, ml_width]
    edges = [0]
    for c in cols:
        edges.append(edges[-1] + c)
    w_sb, w_mqk, w_mv, w_mo = [w_in[:, edges[i]:edges[i + 1]] for i in range(4)]
    w_sb = _scale_q_cols(w_sb, sb_width)
    w_sb, w_mqk, w_mv, w_mo = [w.astype(BF16) for w in (w_sb, w_mqk, w_mv, w_mo)]
    w_gates = _pad_cols(w_in[:, edges[-1]:], LANES).astype(BF16)
    sqkv, ml_qk, ml_v, ml_o, gates = norm_proj(
        x, g_mix, [w_sb, w_mqk, w_mv, w_mo, w_gates], [BF16, F32, BF16, F32, F32])

    c_out = sb_attention(sqkv.reshape(b, s, 3 * sb_width))

    k_scale = jnp.concatenate([jnp.ones((ml_width,), F32),
                               jnp.full((ml_width,), LANES ** -0.5, F32)]).reshape(1, 2 * ml_width)
    qk = conv_silu(ml_qk.reshape(b, s, 2 * ml_width), conv_w, k_scale)
    nb = s // MLSTM_BLOCK
    v3 = ml_v.reshape(b, s, ml_width)
    v_t = v3.reshape(b, nb, MLSTM_BLOCK, n_ml, LANES).transpose(0, 3, 1, 4, 2)
    gate_rows = gates[:, :2 * n_ml].reshape(b, nb, MLSTM_BLOCK, 2 * n_ml).transpose(0, 3, 1, 2)
    gate_rows = gate_rows.reshape(b, 2 * n_ml, nb, 1, MLSTM_BLOCK)
    d_out = mlstm(qk, v3, v_t, gate_rows[:, :n_ml], gate_rows[:, n_ml:],
                  ml_o.reshape(b, s, ml_width), b_igate, b_fgate, head_norm_g)
    return c_out.reshape(t, sb_width), d_out.reshape(t, ml_width)


def moe_block(x, mix_a, mix_b, w_out, g_ffn, w_router, w_gate, w_up, w_down, e, final_g=None, *,
              expert_rows=1024):
    t, d = x.shape
    x, h, route = out_proj_router(x, mix_a, mix_b, w_out, g_ffn, _pad_cols(w_router, LANES))
    flat_e = route[:, :TOP_K].astype(jnp.int32).reshape(-1)
    n_assign = t * TOP_K
    onehot = jax.nn.one_hot(flat_e, N_EXPERTS, dtype=jnp.int32)
    rank = jnp.sum((jnp.cumsum(onehot, axis=0) - onehot) * onehot, axis=-1)
    counts = jnp.sum(onehot, axis=0)
    padded = (counts + expert_rows - 1) // expert_rows * expert_rows
    group_end = jnp.cumsum(padded)
    pos = (group_end[flat_e] - padded[flat_e] + rank).astype(jnp.int32)
    n_rows = n_assign + N_EXPERTS * expert_rows
    n_blocks = n_rows // expert_rows
    block_expert = jnp.minimum(
        jnp.searchsorted(group_end, jnp.arange(n_blocks) * expert_rows, side='right'),
        N_EXPERTS - 1).astype(jnp.int32)
    order = jnp.argsort(pos).astype(jnp.int32)
    row_expert = jnp.repeat(block_expert, expert_rows)
    compact = (jnp.arange(n_rows, dtype=jnp.int32) - (group_end - padded)[row_expert]
               + (jnp.cumsum(counts) - counts)[row_expert])
    row_tok = (order // TOP_K)[jnp.clip(compact, 0, n_assign - 1)]
    n_used = (group_end[-1:] // expert_rows).astype(jnp.int32)

    y_rows = expert_ffn(h, row_tok, block_expert, n_used, w_gate, w_up, w_down, e, tm=expert_rows)
    return combine_residual(x, route, y_rows, pos, final_g)


def kernel(x, norm_mix_g, norm_ffn_g, norm_final_g, ev_w_in, ev_b_forget, ev_pool_w, ev_pool_scale, ev_w_out, ffn_w_gate, ffn_w_up, ffn_w_down, od_w_in, od_b_igate, od_b_fgate, od_conv_w, od_head_norm_g, od_w_out, moe_w_router, moe_w_gate, moe_w_up, moe_w_down):
    b, s, d = x.shape
    depth = norm_mix_g.shape[0]
    xt = x.reshape(b * s, d)
    ffn_w = [w.astype(BF16) for w in (ffn_w_gate, ffn_w_up, ffn_w_down)]
    moe_w = [w.astype(BF16) for w in (moe_w_gate, moe_w_up, moe_w_down)]
    for layer in range(depth):
        e = layer // 2
        if layer % 2 == 0:
            xt = even_layer(xt, b, s, norm_mix_g[layer], norm_ffn_g[layer], ev_w_in[e],
                            ev_b_forget[e], ev_pool_w[e], ev_pool_scale[e], ev_w_out[e],
                            *ffn_w, e)
        else:
            xt = odd_layer(xt, b, s, norm_mix_g[layer], norm_ffn_g[layer], od_w_in[e],
                           od_b_igate[e], od_b_fgate[e], od_conv_w[e], od_head_norm_g[e],
                           od_w_out[e], moe_w_router[e], *moe_w, e,
                           norm_final_g if layer == depth - 1 else None)
    assert depth % 2 == 0
    return xt.reshape(b, s, d)
```
